```python
import math
import jax, jax.numpy as jnp
from jax import lax
import numpy as np

D_MODEL = 2048
BATCH = 8
SEQ = 2048
DEPTH = 4
DEC_BATCH = 32
DEC_SEQ = 64
PAST_LEN = 1024

CHUNK = 64
D_MIX = D_MODEL
A_HEADS = 4
A_HEAD_DIM = 128
A_WIDTH = A_HEADS * A_HEAD_DIM
IDX_HEADS = 16
IDX_DIM = 64
DSA_TOPK = 256
B_HEADS = 4
B_HEAD_DIM = 128
B_WIDTH = B_HEADS * B_HEAD_DIM
B_PREV_CHUNKS = 8
B_WIN = B_PREV_CHUNKS * CHUNK
REL_CLIP = 128
C_WIDTH = D_MIX - A_WIDTH - B_WIDTH
C_HEAD_DIM = 64
C_HEADS = C_WIDTH // C_HEAD_DIM
C_GROUPS = 4
C_STATE = 128
CONV_W = 4
CONV_DIM = C_WIDTH + 2 * C_GROUPS * C_STATE
D_FF = 4 * D_MODEL
ROPE_THETA = 500000.0
EPS = 1e-5

_COLS = (A_WIDTH, A_WIDTH, A_WIDTH, IDX_HEADS * IDX_DIM, IDX_DIM, IDX_HEADS,
         B_WIDTH, B_WIDTH, B_WIDTH,
         C_WIDTH, CONV_DIM, C_HEADS)
SPLIT_OFFSETS = tuple(int(v) for v in np.cumsum(_COLS)[:-1])
IN_COLS = int(sum(_COLS))

kernel_name = "hybrid_stream_dsa_band_ssd_step"


def rms_norm(x, w):
    x32 = x.astype(jnp.float32)
    y = x32 * lax.rsqrt(jnp.mean(x32 * x32, axis=-1, keepdims=True) + EPS)
    return y.astype(x.dtype) * w


def partial_rope(x, pos):
    d = x.shape[-1]
    rot = d // 4
    half = rot // 2
    inv = ROPE_THETA ** (-jnp.arange(half, dtype=jnp.float32) * 2.0 / rot)
    ang = pos.astype(jnp.float32)[:, None] * inv[None, :]
    cos = jnp.cos(ang)[None, :, None, :]
    sin = jnp.sin(ang)[None, :, None, :]
    x32 = x.astype(jnp.float32)
    x1, x2 = x32[..., :half], x32[..., half:rot]
    out = jnp.concatenate([x1 * cos - x2 * sin, x2 * cos + x1 * sin, x32[..., rot:]], axis=-1)
    return out.astype(x.dtype)


def dsa_select_attend(q, qi, wi, k, v, ki, limit, topk):
    S = k.shape[1]
    d = q.shape[-1]
    idx_logits = jnp.einsum('bqhd,bsd->bqhs', qi, ki).astype(jnp.float32) * IDX_DIM ** -0.5
    score = jnp.einsum('bqhs,bqh->bqs', jax.nn.relu(idx_logits), wi.astype(jnp.float32))
    admissible = jnp.arange(S) < limit
    score = jnp.where(admissible[None, None, :], score, -jnp.inf)
    _, sel = lax.top_k(score, topk)
    valid = sel < limit
    kg = jax.vmap(lambda kk, ii: kk[ii])(k, sel)
    vg = jax.vmap(lambda vv, ii: vv[ii])(v, sel)
    logits = jnp.einsum('bqhd,bqkhd->bhqk', q, kg).astype(jnp.float32) * d ** -0.5
    logits = jnp.where(valid[:, None, :, :], logits, -jnp.inf)
    p = jax.nn.softmax(logits, axis=-1).astype(v.dtype)
    return jnp.einsum('bhqk,bqkhd->bqhd', p, vg)


def dsa_prompt(q, qi, wi, k, v, ki):
    b, t = q.shape[:2]
    nc = t // CHUNK
    topk = min(DSA_TOPK, t // 4)

    def one_chunk(c):
        s0 = c * CHUNK
        sl = lambda a: lax.dynamic_slice_in_dim(a, s0, CHUNK, axis=1)
        return dsa_select_attend(sl(q), sl(qi), sl(wi), k, v, ki, s0 + CHUNK, topk)

    o = lax.map(one_chunk, jnp.arange(nc))
    return jnp.moveaxis(o, 0, 1).reshape(b, t, A_HEADS, A_HEAD_DIM)


def dsa_sample(q, qi, wi, k, v, ki):
    L = k.shape[1]
    topk = min(DSA_TOPK, L // 4)

    def one_seq(a):
        return dsa_select_attend(*[z[None] for z in a], L, topk)[0]

    return lax.map(one_seq, (q, qi, wi, k, v, ki))


def rel_bias(table, rel):
    return table[:, jnp.clip(rel, -REL_CLIP, REL_CLIP) + REL_CLIP]


def band_attn_prompt(q, k, v, table):
    b, t, h, d = q.shape
    nc = t // CHUNK
    nb = B_PREV_CHUNKS + 1
    qc = q.reshape(b, nc, CHUNK, h, d)
    pad = ((0, 0), (B_WIN, 0), (0, 0), (0, 0))
    kc = jnp.pad(k, pad).reshape(b, nc + B_PREV_CHUNKS, CHUNK, h, d)
    vc = jnp.pad(v, pad).reshape(b, nc + B_PREV_CHUNKS, CHUNK, h, d)
    band_idx = jnp.arange(nc)[:, None] + jnp.arange(nb)[None, :]
    kb = kc[:, band_idx].reshape(b, nc, nb * CHUNK, h, d)
    vb = vc[:, band_idx].reshape(b, nc, nb * CHUNK, h, d)
    j = jnp.arange(nb * CHUNK)
    rel = (B_WIN + jnp.arange(CHUNK))[:, None] - j[None, :]
    valid = (jnp.arange(nc)[:, None] * CHUNK - B_WIN + j[None, :]) >= 0
    logits = jnp.einsum('bcqhd,bckhd->bchqk', qc, kb).astype(jnp.float32) * d ** -0.5
    logits = logits + rel_bias(table, rel).astype(jnp.float32)[None, None]
    logits = jnp.where(valid[None, :, None, None, :], logits, -jnp.inf)
    p = jax.nn.softmax(logits, axis=-1).astype(v.dtype)
    o = jnp.einsum('bchqk,bckhd->bcqhd', p, vb)
    return o.reshape(b, t, h, d)


def band_attn_sample(q, k_new, v_new, k_buf, v_buf, table):
    t = q.shape[1]
    d = q.shape[-1]
    nbuf = k_buf.shape[1]
    k = jnp.concatenate([k_buf, k_new], axis=1)
    v = jnp.concatenate([v_buf, v_new], axis=1)
    rel = (nbuf + jnp.arange(t))[:, None] - jnp.arange(nbuf + t)[None, :]
    logits = jnp.einsum('bqhd,bkhd->bhqk', q, k).astype(jnp.float32) * d ** -0.5
    logits = logits + rel_bias(table, rel).astype(jnp.float32)[None]
    p = jax.nn.softmax(logits, axis=-1).astype(v.dtype)
    return jnp.einsum('bhqk,bkhd->bqhd', p, v)


def causal_dwconv(x, buf, w, bias):
    t = x.shape[1]
    xp = jnp.concatenate([buf, x], axis=1)
    y = bias + xp[:, 0:t] * w[0]
    for j in range(1, CONV_W):
        y = y + xp[:, j:j + t] * w[j]
    return y, xp[:, -(CONV_W - 1):]


def segsum(a):
    T = a.shape[-1]
    cs = jnp.cumsum(a, axis=-1)
    seg = cs[..., :, None] - cs[..., None, :]
    mask = jnp.tril(jnp.ones((T, T), dtype=bool))
    return jnp.where(mask, seg, -jnp.inf)


def ssd_scan(x, dt, a, bm, cm, h0, block):
    b, L, H, P = x.shape
    nc = L // block
    xd = (x * dt[..., None]).reshape(b, nc, block, H, P)
    ad = (dt * a).reshape(b, nc, block, H).transpose(0, 3, 1, 2)
    bm = bm.reshape(b, nc, block, H, -1)
    cm = cm.reshape(b, nc, block, H, -1)
    a_cs = jnp.cumsum(ad, axis=-1)
    decay_in = jnp.exp(segsum(ad))
    cb = jnp.einsum('bclhn,bcshn->bhcls', cm, bm)
    y_diag = jnp.einsum('bhcls,bcshp->bclhp', cb * decay_in, xd)
    decay_to_end = jnp.exp(a_cs[..., -1:] - a_cs)
    chunk_states = jnp.einsum('bclhn,bhcl,bclhp->bchpn', bm, decay_to_end, xd)
    states = jnp.concatenate([h0[:, None], chunk_states], axis=1)
    chunk_decay = jnp.exp(segsum(jnp.pad(a_cs[..., -1], ((0, 0), (0, 0), (1, 0)))))
    states = jnp.einsum('bhzc,bchpn->bzhpn', chunk_decay, states)
    h_in, h_last = states[:, :-1], states[:, -1]
    y_off = jnp.einsum('bclhn,bchpn,bhcl->bclhp', cm, h_in, jnp.exp(a_cs))
    return (y_diag + y_off).reshape(b, L, H, P), h_last


def ssd_mixer(z, xbc, dt_raw, conv_buf, h0, conv_w, conv_b, dt_bias, a_log, d_skip, gate_norm, block):
    f32 = jnp.float32
    b, t = z.shape[:2]
    xbc_c, new_buf = causal_dwconv(xbc, conv_buf, conv_w, conv_b)
    xbc_c = jax.nn.silu(xbc_c)
    xs, bs, cs = jnp.split(xbc_c, [C_WIDTH, C_WIDTH + C_GROUPS * C_STATE], axis=-1)
    rep = C_HEADS // C_GROUPS
    xs = xs.reshape(b, t, C_HEADS, C_HEAD_DIM).astype(f32)
    bs = jnp.repeat(bs.reshape(b, t, C_GROUPS, C_STATE).astype(f32), rep, axis=2)
    cs = jnp.repeat(cs.reshape(b, t, C_GROUPS, C_STATE).astype(f32), rep, axis=2)
    dt = jax.nn.softplus(dt_raw.astype(f32) + dt_bias.astype(f32))
    a = -jnp.exp(a_log.astype(f32))
    y, h_last = ssd_scan(xs, dt, a, bs, cs, h0.astype(f32), block)
    y = (y + d_skip.astype(f32)[:, None] * xs).reshape(b, t, C_WIDTH)
    g = (y * jax.nn.silu(z.astype(f32))).reshape(b, t, C_GROUPS, C_WIDTH // C_GROUPS)
    g = g * lax.rsqrt(jnp.mean(g * g, axis=-1, keepdims=True) + EPS)
    out = g.reshape(b, t, C_WIDTH).astype(z.dtype) * gate_norm
    return out, h_last.astype(z.dtype), new_buf


def trunk_layer(x, pos, lw, past):
    (norm1, w_in, w_out, b_rel, conv_w, conv_b, dt_bias, a_log, d_skip, gate_norm,
     norm2, w_up, w_down) = lw
    b, t, _ = x.shape
    h = rms_norm(x, norm1)
    proj = jnp.einsum('btd,de->bte', h, w_in)
    aq, ak, av, iq, ik, iw, bq, bk, bv, cz, cxbc, cdt = jnp.split(proj, SPLIT_OFFSETS, axis=-1)
    aq = partial_rope(aq.reshape(b, t, A_HEADS, A_HEAD_DIM), pos)
    ak = partial_rope(ak.reshape(b, t, A_HEADS, A_HEAD_DIM), pos)
    av = av.reshape(b, t, A_HEADS, A_HEAD_DIM)
    iq = partial_rope(iq.reshape(b, t, IDX_HEADS, IDX_DIM), pos)
    ik = partial_rope(ik.reshape(b, t, 1, IDX_DIM), pos)[:, :, 0]
    iw = iw * IDX_HEADS ** -0.5
    bq = bq.reshape(b, t, B_HEADS, B_HEAD_DIM)
    bk = bk.reshape(b, t, B_HEADS, B_HEAD_DIM)
    bv = bv.reshape(b, t, B_HEADS, B_HEAD_DIM)
    if past is None:
        o_a = dsa_prompt(aq, iq, iw, ak, av, ik)
        o_b = band_attn_prompt(bq, bk, bv, b_rel)
        conv_buf = jnp.zeros((b, CONV_W - 1, CONV_DIM), x.dtype)
        h0 = jnp.zeros((b, C_HEADS, C_HEAD_DIM, C_STATE), jnp.float32)
        block = CHUNK
        nkeep = min(B_WIN, t)
        b_k_rows, b_v_rows = bk[:, t - nkeep:], bv[:, t - nkeep:]
    else:
        pk, pv, pki, pbk, pbv, ph, pconv = past
        o_a = dsa_sample(aq, iq, iw,
                         jnp.concatenate([pk, ak], axis=1),
                         jnp.concatenate([pv, av], axis=1),
                         jnp.concatenate([pki, ik], axis=1))
        o_b = band_attn_sample(bq, bk, bv, pbk, pbv, b_rel)
        conv_buf, h0, block = pconv, ph, t
        b_k_rows, b_v_rows = bk, bv
    o_c, h_last, conv_new = ssd_mixer(cz, cxbc, cdt, conv_buf, h0, conv_w, conv_b,
                                      dt_bias, a_log, d_skip, gate_norm, block)
    mix = jnp.concatenate([o_a.reshape(b, t, A_WIDTH), o_b.reshape(b, t, B_WIDTH), o_c], axis=-1)
    x = x + jnp.einsum('bte,ed->btd', mix, w_out)
    u = jax.nn.relu(jnp.einsum('btd,df->btf', rms_norm(x, norm2), w_up))
    x = x + jnp.einsum('btf,fd->btd', jnp.square(u), w_down)
    return x, (ak, av, ik, b_k_rows, b_v_rows, h_last, conv_new)


def stack_states(states):
    return [jnp.stack([s[i] for s in states], axis=0) for i in range(7)]


def setup_inputs(seed: int = 0) -> dict:
    key = jax.random.key(seed)
    ks = jax.random.split(key, 24)
    f32 = jnp.float32

    def nrm(k, shape, s):
        return jax.random.normal(k, shape, f32) * s

    b_cache = min(B_WIN, PAST_LEN)
    dt0 = jnp.exp(jax.random.uniform(ks[15], (DEPTH, C_HEADS), f32, math.log(1e-3), math.log(1e-1)))
    return {
        "x_prompt": nrm(ks[0], (BATCH, SEQ, D_MODEL), 1.0),
        "x_sample": nrm(ks[1], (DEC_BATCH, DEC_SEQ, D_MODEL), 1.0),
        "cache_a_k": nrm(ks[2], (DEPTH, DEC_BATCH, PAST_LEN, A_HEADS, A_HEAD_DIM), 1.0),
        "cache_a_v": nrm(ks[3], (DEPTH, DEC_BATCH, PAST_LEN, A_HEADS, A_HEAD_DIM), 1.0),
        "cache_a_kidx": nrm(ks[4], (DEPTH, DEC_BATCH, PAST_LEN, IDX_DIM), 1.0),
        "cache_b_k": nrm(ks[5], (DEPTH, DEC_BATCH, b_cache, B_HEADS, B_HEAD_DIM), 1.0),
        "cache_b_v": nrm(ks[6], (DEPTH, DEC_BATCH, b_cache, B_HEADS, B_HEAD_DIM), 1.0),
        "state_ssm": nrm(ks[7], (DEPTH, DEC_BATCH, C_HEADS, C_HEAD_DIM, C_STATE), 0.5),
        "state_conv": nrm(ks[8], (DEPTH, DEC_BATCH, CONV_W - 1, CONV_DIM), 1.0),
        "norm1": 1.0 + nrm(ks[9], (DEPTH, D_MODEL), 0.02),
        "w_in": nrm(ks[10], (DEPTH, D_MODEL, IN_COLS), D_MODEL ** -0.5),
        "w_out": nrm(ks[11], (DEPTH, D_MIX, D_MODEL), 0.5 * D_MIX ** -0.5),
        "b_rel": nrm(ks[12], (DEPTH, B_HEADS, 2 * REL_CLIP + 1), 0.1),
        "conv_w": nrm(ks[13], (DEPTH, CONV_W, CONV_DIM), CONV_W ** -0.5),
        "conv_b": nrm(ks[14], (DEPTH, CONV_DIM), 0.01),
        "dt_bias": dt0 + jnp.log(-jnp.expm1(-dt0)),
        "a_log": jnp.log(jax.random.uniform(ks[16], (DEPTH, C_HEADS), f32, 1.0, 16.0)),
        "d_skip": 1.0 + nrm(ks[17], (DEPTH, C_HEADS), 0.1),
        "gate_norm": 1.0 + nrm(ks[18], (DEPTH, C_WIDTH), 0.02),
        "norm2": 1.0 + nrm(ks[19], (DEPTH, D_MODEL), 0.02),
        "w_up": nrm(ks[20], (DEPTH, D_MODEL, D_FF), D_MODEL ** -0.5),
        "w_down": nrm(ks[21], (DEPTH, D_FF, D_MODEL), 0.5 * D_FF ** -0.5),
        "final_norm": 1.0 + nrm(ks[22], (D_MODEL,), 0.02),
    }


def reference(x_prompt, x_sample, cache_a_k, cache_a_v, cache_a_kidx, cache_b_k, cache_b_v,
              state_ssm, state_conv, norm1, w_in, w_out, b_rel, conv_w, conv_b, dt_bias,
              a_log, d_skip, gate_norm, norm2, w_up, w_down, final_norm):
    past_len = cache_a_k.shape[2]
    pos_p = jnp.arange(x_prompt.shape[1])
    pos_s = past_len + jnp.arange(x_sample.shape[1])
    xp, xs = x_prompt, x_sample
    new_p, new_s = [], []
    for l in range(DEPTH):
        lw = (norm1[l], w_in[l], w_out[l], b_rel[l], conv_w[l], conv_b[l], dt_bias[l], a_log[l],
              d_skip[l], gate_norm[l], norm2[l], w_up[l], w_down[l])
        xp, st_p = trunk_layer(xp, pos_p, lw, None)
        past = (cache_a_k[l], cache_a_v[l], cache_a_kidx[l], cache_b_k[l], cache_b_v[l],
                state_ssm[l], state_conv[l])
        xs, st_s = trunk_layer(xs, pos_s, lw, past)
        new_p.append(st_p)
        new_s.append(st_s)
    y_prompt = rms_norm(xp, final_norm)
    y_sample = rms_norm(xs, final_norm)
    p_a_k, p_a_v, p_a_kidx, p_b_k, p_b_v, p_ssm, p_conv = stack_states(new_p)
    s_a_k, s_a_v, s_a_kidx, s_b_k, s_b_v, s_ssm, s_conv = stack_states(new_s)
    return (y_prompt, y_sample, p_a_k, p_a_v, p_a_kidx, p_b_k, p_b_v, p_ssm, p_conv,
            s_a_k, s_a_v, s_a_kidx, s_b_k, s_b_v, s_ssm, s_conv)
```

```python
import functools

import jax
import jax.numpy as jnp
import numpy as np
from jax import lax
from jax.experimental import pallas as pl
from jax.experimental.pallas import tpu as pltpu

F32 = jnp.float32
BF16 = jnp.bfloat16
I32 = jnp.int32

D_MODEL = 2048
CHUNK = 64
A_HEADS = 4
A_HEAD_DIM = 128
A_WIDTH = A_HEADS * A_HEAD_DIM
IDX_HEADS = 16
IDX_DIM = 64
DSA_TOPK = 256
B_HEADS = 4
B_HEAD_DIM = 128
B_WIDTH = B_HEADS * B_HEAD_DIM
B_PREV_CHUNKS = 8
B_WIN = B_PREV_CHUNKS * CHUNK
REL_CLIP = 128
C_WIDTH = 1024
C_HEAD_DIM = 64
C_HEADS = C_WIDTH // C_HEAD_DIM
C_GROUPS = 4
C_STATE = 128
CONV_W = 4
CONV_DIM = C_WIDTH + 2 * C_GROUPS * C_STATE
D_FF = 4 * D_MODEL
ROPE_THETA = 500000.0
EPS = 1e-5

LANES = 128
INT_MIN = -(2 ** 31)
NEG_INF = float("-inf")

COL_CXBC = 0
COL_CZ = 2048
COL_IQ = 3072
COL_AQ = 4096
COL_AK = 4608
COL_AV = 5120
COL_BQ = 5632
COL_BK = 6144
COL_BV = 6656
COL_MISC = 7168
MISC_IW = 64
MISC_DT = 80
IN_COLS_PAD = 7680

VMEM_LIMIT = 56 * 1024 * 1024


def _cparams(*sem):
    return pltpu.CompilerParams(dimension_semantics=sem, vmem_limit_bytes=VMEM_LIMIT)


def _row_tile(t, cap):
    tm = cap
    while t % tm:
        tm //= 2
    return tm


def _nt_dot(a, b):
    return lax.dot_general(a, b, (((1,), (1,)), ((), ())), preferred_element_type=F32)


def _split3(x):
    hi = x.astype(BF16)
    r1 = x - hi.astype(F32)
    mid = r1.astype(BF16)
    lo = (r1 - mid.astype(F32)).astype(BF16)
    return hi, mid, lo


def _inproj_kernel(x_ref, nw_ref, w_ref, o_ref, h_scr):
    @pl.when(pl.program_id(1) == 0)
    def _():
        x = x_ref[...]
        ms = jnp.mean(x * x, axis=-1, keepdims=True)
        h_scr[...] = (x * lax.rsqrt(ms + EPS) * nw_ref[...]).astype(BF16)

    o_ref[...] = jnp.dot(h_scr[...], w_ref[...], preferred_element_type=F32)


def _in_proj(x, norm_w, w_bf16):
    t, d = x.shape
    n = w_bf16.shape[1]
    tm = _row_tile(t, 1024)
    tn = 512
    return pl.pallas_call(
        _inproj_kernel,
        grid=(t // tm, n // tn),
        in_specs=[
            pl.BlockSpec((tm, d), lambda i, j: (i, 0)),
            pl.BlockSpec((1, d), lambda i, j: (0, 0)),
            pl.BlockSpec((d, tn), lambda i, j: (0, j)),
        ],
        out_specs=pl.BlockSpec((tm, tn), lambda i, j: (i, j)),
        out_shape=jax.ShapeDtypeStruct((t, n), F32),
        scratch_shapes=[pltpu.VMEM((tm, d), BF16)],
        compiler_params=_cparams("parallel", "arbitrary"),
        name="in_proj",
    )(x, norm_w.reshape(1, d), w_bf16)


def _rope_block(x, c, sm, sp, half):
    return x * c + pltpu.roll(x, LANES - half, 1) * sm + pltpu.roll(x, half, 1) * sp


def _rope_kernel(aq_ref, ak_ref, iq_ref, misc_ref, ta_ref, ti_ref, aq_o, ak_o, iq_o, misc_o):
    ca, sma, spa = ta_ref[0], ta_ref[1], ta_ref[2]
    ci, smi, spi = ti_ref[0], ti_ref[1], ti_ref[2]
    for h in range(A_HEADS):
        sl = slice(h * LANES, (h + 1) * LANES)
        aq_o[:, sl] = _rope_block(aq_ref[:, sl], ca, sma, spa, A_HEAD_DIM // 8)
        ak_o[:, sl] = _rope_block(ak_ref[:, sl], ca, sma, spa, A_HEAD_DIM // 8)
    for h in range(IDX_HEADS * IDX_DIM // LANES):
        sl = slice(h * LANES, (h + 1) * LANES)
        iq_o[:, sl] = _rope_block(iq_ref[:, sl], ci, smi, spi, IDX_DIM // 8)
    m = misc_ref[...]
    lane = lax.broadcasted_iota(I32, m.shape, 1)
    misc_o[...] = jnp.where(lane < IDX_DIM, _rope_block(m, ci, smi, spi, IDX_DIM // 8), m)


def _rope_tables(pos, head_dim):
    rot = head_dim // 4
    half = rot // 2
    inv = ROPE_THETA ** (-jnp.arange(half, dtype=F32) * 2.0 / rot)
    ang = pos.astype(F32)[:, None] * inv[None, :]
    cos, sin = jnp.cos(ang), jnp.sin(ang)
    n = pos.shape[0]
    rest = head_dim - rot
    c = jnp.concatenate([cos, cos, jnp.ones((n, rest), F32)], axis=1)
    sm = jnp.concatenate([-sin, jnp.zeros((n, half + rest), F32)], axis=1)
    sp = jnp.concatenate([jnp.zeros((n, half), F32), sin, jnp.zeros((n, rest), F32)], axis=1)
    tab = jnp.stack([c, sm, sp], axis=0)
    return jnp.tile(tab, (1, 1, LANES // head_dim))


def _rope(proj, tab_a, tab_i):
    t = proj.shape[0]
    tm = _row_tile(t, 512)
    col = lambda w, off: pl.BlockSpec((tm, w), lambda i: (i, off // w))
    tab = pl.BlockSpec((3, tm, LANES), lambda i: (0, i, 0))
    return pl.pallas_call(
        _rope_kernel,
        grid=(t // tm,),
        in_specs=[col(A_WIDTH, COL_AQ), col(A_WIDTH, COL_AK), col(IDX_HEADS * IDX_DIM, COL_IQ),
                  col(LANES, COL_MISC), tab, tab],
        out_specs=[pl.BlockSpec((tm, A_WIDTH), lambda i: (i, 0)),
                   pl.BlockSpec((tm, A_WIDTH), lambda i: (i, 0)),
                   pl.BlockSpec((tm, IDX_HEADS * IDX_DIM), lambda i: (i, 0)),
                   pl.BlockSpec((tm, LANES), lambda i: (i, 0))],
        out_shape=[jax.ShapeDtypeStruct((t, A_WIDTH), F32),
                   jax.ShapeDtypeStruct((t, A_WIDTH), F32),
                   jax.ShapeDtypeStruct((t, IDX_HEADS * IDX_DIM), F32),
                   jax.ShapeDtypeStruct((t, LANES), F32)],
        compiler_params=_cparams("parallel"),
        name="rope",
    )(proj, proj, proj, proj, tab_a, tab_i)


TIE_BLOCK = 64


def _dsa_kernel(q_ref, qi_ref, mq_ref, k_ref, v_ref, mk_ref, o_ref,
                kb_scr, vt_scr, kib_scr, score_scr, key_scr, mask_scr,
                *, s_keys, qs, topk, causal, limit_all):
    j = pl.program_id(1)

    @pl.when(j == 0)
    def _():
        kb_scr[...] = k_ref[...].astype(BF16)
        vt_scr[...] = v_ref[...].T.astype(BF16)
        kib_scr[...] = mk_ref[:, 0:IDX_DIM].astype(BF16)

    mq_t = mq_ref[...].T
    kib = kib_scr[...]
    for h in range(IDX_HEADS):
        qh = qi_ref[:, h * IDX_DIM:(h + 1) * IDX_DIM].astype(BF16)
        lg = _nt_dot(kib, qh) * (IDX_DIM ** -0.5)
        w = mq_t[MISC_IW + h:MISC_IW + h + 1, :] * (IDX_HEADS ** -0.5)
        term = jnp.maximum(lg, 0.0) * w
        if h == 0:
            score_scr[...] = term
        else:
            score_scr[...] += term

    s_iota = lax.broadcasted_iota(I32, (s_keys, qs), 0)
    if causal:
        q_pos = j * qs + lax.broadcasted_iota(I32, (1, qs), 1)
        limit = ((q_pos >> 6) + 1) * CHUNK
    else:
        limit = jnp.full((1, qs), limit_all, I32)
    sc = score_scr[...]
    sc = jnp.where(sc == 0.0, 0.0, sc)
    bits = lax.bitcast_convert_type(sc, I32)
    key = bits ^ ((bits >> 31) & 0x7FFFFFFF)
    key_scr[...] = jnp.where(s_iota < limit, key, INT_MIN)

    def count_ge(t):
        return jnp.sum((key_scr[...] >= t).astype(I32), axis=0, keepdims=True)

    ans0 = jnp.where(count_ge(jnp.zeros((1, qs), I32)) >= topk, 0, INT_MIN).astype(I32)

    def bit_step(i, ans):
        cand = ans | (jnp.int32(1) << (30 - i))
        return jnp.where(count_ge(cand) >= topk, cand, ans)

    thr = lax.fori_loop(0, 31, bit_step, ans0)
    key = key_scr[...]
    gt = key > thr
    eq = key == thr
    cnt_gt = jnp.sum(gt.astype(I32), axis=0, keepdims=True)
    cnt_eq = jnp.sum(eq.astype(I32), axis=0, keepdims=True)
    room = topk - cnt_gt
    all_ties = (cnt_eq <= room) & (thr > INT_MIN)
    mask_scr[...] = jnp.where(gt | (eq & all_ties), 0.0, NEG_INF)
    need_ties = jnp.max(((cnt_eq > room) & (thr > INT_MIN)).astype(I32)) > 0

    @pl.when(need_ties)
    def _():
        tri = (lax.broadcasted_iota(I32, (TIE_BLOCK, TIE_BLOCK), 0)
               >= lax.broadcasted_iota(I32, (TIE_BLOCK, TIE_BLOCK), 1)).astype(BF16)
        live = thr > INT_MIN

        def blk(b, carry):
            r0 = pl.multiple_of(b * TIE_BLOCK, TIE_BLOCK)
            kblk = key_scr[pl.ds(r0, TIE_BLOCK), :]
            eqb = (kblk == thr) & live
            eqf = eqb.astype(F32)
            prefix = jnp.dot(tri, eqf.astype(BF16), preferred_element_type=F32) + carry
            keep = (kblk > thr) | (eqb & (prefix <= room.astype(F32)))
            mask_scr[pl.ds(r0, TIE_BLOCK), :] = jnp.where(keep, 0.0, NEG_INF)
            return carry + jnp.sum(eqf, axis=0, keepdims=True)

        lax.fori_loop(0, s_keys // TIE_BLOCK, blk, jnp.zeros((1, qs), F32))

    for h in range(A_HEADS):
        sl = slice(h * A_HEAD_DIM, (h + 1) * A_HEAD_DIM)
        qh = q_ref[:, sl].astype(BF16)
        lg = _nt_dot(kb_scr[:, sl], qh) * (A_HEAD_DIM ** -0.5) + mask_scr[...]
        m = jnp.max(lg, axis=0, keepdims=True)
        e = jnp.exp(lg - m)
        den = jnp.sum(e, axis=0, keepdims=True)
        o_t = jnp.dot(vt_scr[sl, :], e.astype(BF16), preferred_element_type=F32)
        o_ref[:, sl] = (o_t / den).T


def _dsa(q, qi, mq, k, v, mk, *, nb, tq, s_keys, qs, q_row0, v_col, mk_col, causal, limit_all):
    nsteps = tq // qs
    qb0 = q_row0 // qs
    qrow = lambda b, j: (qb0 + b * nsteps + j, 0)
    kern = functools.partial(_dsa_kernel, s_keys=s_keys, qs=qs, topk=min(DSA_TOPK, limit_all // 4),
                             causal=causal, limit_all=limit_all)
    return pl.pallas_call(
        kern,
        grid=(nb, nsteps),
        in_specs=[
            pl.BlockSpec((qs, A_WIDTH), qrow),
            pl.BlockSpec((qs, IDX_HEADS * IDX_DIM), qrow),
            pl.BlockSpec((qs, LANES), qrow),
            pl.BlockSpec((s_keys, A_WIDTH), lambda b, j: (b, 0)),
            pl.BlockSpec((s_keys, A_WIDTH), lambda b, j: (b, v_col)),
            pl.BlockSpec((s_keys, LANES), lambda b, j: (b, mk_col)),
        ],
        out_specs=pl.BlockSpec((qs, A_WIDTH), lambda b, j: (b * nsteps + j, 0)),
        out_shape=jax.ShapeDtypeStruct((nb * nsteps * qs, A_WIDTH), F32),
        scratch_shapes=[
            pltpu.VMEM((s_keys, A_WIDTH), BF16),
            pltpu.VMEM((A_WIDTH, s_keys), BF16),
            pltpu.VMEM((s_keys, IDX_DIM), BF16),
            pltpu.VMEM((s_keys, qs), F32),
            pltpu.VMEM((s_keys, qs), I32),
            pltpu.VMEM((s_keys, qs), F32),
        ],
        compiler_params=_cparams("parallel", "arbitrary"),
        name="dsa_prompt" if causal else "dsa_sample",
    )(q, qi, mq, k, v, mk)


def _band_bias_kernel(tab_ref, o_ref, *, qs, w):
    layer = pl.program_id(0)
    q = lax.broadcasted_iota(I32, (qs, w), 0)
    jx = lax.broadcasted_iota(I32, (qs, w), 1)
    idx = jnp.clip(B_WIN + q - jx, -REL_CLIP, REL_CLIP) + REL_CLIP
    back = (q >> 6) + B_PREV_CHUNKS - (jx >> 6)
    allowed = (back >= 0) & (back <= B_PREV_CHUNKS)
    for h in range(B_HEADS):
        row = layer * B_HEADS + h

        def body(v, acc):
            return jnp.where(idx == v, tab_ref[row, v], acc)

        acc = lax.fori_loop(0, 2 * REL_CLIP + 1, body, jnp.zeros((qs, w), F32))
        o_ref[0, h] = jnp.where(allowed, acc, NEG_INF)


def _band_bias(b_rel, qs):
    depth = b_rel.shape[0]
    w = qs + B_WIN
    return pl.pallas_call(
        functools.partial(_band_bias_kernel, qs=qs, w=w),
        grid=(depth,),
        in_specs=[pl.BlockSpec(memory_space=pltpu.SMEM)],
        out_specs=pl.BlockSpec((1, B_HEADS, qs, w), lambda l: (l, 0, 0, 0)),
        out_shape=jax.ShapeDtypeStruct((depth, B_HEADS, qs, w), F32),
        compiler_params=_cparams("arbitrary"),
        name=f"band_bias_{qs}",
    )(b_rel.reshape(depth * B_HEADS, 2 * REL_CLIP + 1))


def _band_kernel(q_ref, k_ref, v_ref, bias_ref, o_ref, kp_scr, vp_scr, *, s_keys, qs, pad):
    j = pl.program_id(1)
    w = qs + B_WIN

    @pl.when(j == 0)
    def _():
        if pad:
            kp_scr[0:pad, :] = jnp.zeros((pad, B_WIDTH), BF16)
            vp_scr[0:pad, :] = jnp.zeros((pad, B_WIDTH), BF16)
        kp_scr[pad:pad + s_keys, :] = k_ref[...].astype(BF16)
        vp_scr[pad:pad + s_keys, :] = v_ref[...].astype(BF16)

    start = pl.multiple_of(j * qs, qs)
    valid = lax.broadcasted_iota(I32, (qs, w), 1) >= pad - j * qs
    for h in range(B_HEADS):
        sl = slice(h * B_HEAD_DIM, (h + 1) * B_HEAD_DIM)
        qh = q_ref[:, sl].astype(BF16)
        lg = _nt_dot(qh, kp_scr[pl.ds(start, w), sl]) * (B_HEAD_DIM ** -0.5) + bias_ref[0, h]
        lg = jnp.where(valid, lg, NEG_INF)
        m = jnp.max(lg, axis=-1, keepdims=True)
        e = jnp.exp(lg - m)
        den = jnp.sum(e, axis=-1, keepdims=True)
        o = jnp.dot(e.astype(BF16), vp_scr[pl.ds(start, w), sl], preferred_element_type=F32)
        o_ref[:, sl] = o / den


def _band(q, k, v, bias, layer, *, nb, tq, s_keys, qs, q_row0, q_col, k_col, v_col, pad):
    nsteps = tq // qs
    qb0 = q_row0 // qs
    w = qs + B_WIN
    return pl.pallas_call(
        functools.partial(_band_kernel, s_keys=s_keys, qs=qs, pad=pad),
        grid=(nb, nsteps),
        in_specs=[
            pl.BlockSpec((qs, B_WIDTH), lambda b, j: (qb0 + b * nsteps + j, q_col)),
            pl.BlockSpec((s_keys, B_WIDTH), lambda b, j: (b, k_col)),
            pl.BlockSpec((s_keys, B_WIDTH), lambda b, j: (b, v_col)),
            pl.BlockSpec((1, B_HEADS, qs, w), lambda b, j: (layer, 0, 0, 0)),
        ],
        out_specs=pl.BlockSpec((qs, B_WIDTH), lambda b, j: (b * nsteps + j, 0)),
        out_shape=jax.ShapeDtypeStruct((nb * nsteps * qs, B_WIDTH), F32),
        scratch_shapes=[pltpu.VMEM((pad + s_keys, B_WIDTH), BF16),
                        pltpu.VMEM((pad + s_keys, B_WIDTH), BF16)],
        compiler_params=_cparams("parallel", "arbitrary"),
        name="band_prompt" if pad else "band_sample",
    )(q, k, v, bias)


XP_OFF = 8


def _ssd_kernel(z_ref, xbc_ref, misc_ref, h0_ref, cb0_ref, cw_ref, cbias_ref, prm_ref, gn_ref,
                o_ref, ht_ref, cnew_ref, xp_scr, y_scr):
    c = pl.program_id(1)
    tail = CONV_W - 1

    @pl.when(c == 0)
    def _():
        ht_ref[...] = h0_ref[...]
        xp_scr[XP_OFF - tail:XP_OFF, :] = cb0_ref[0]

    xp_scr[XP_OFF:XP_OFF + CHUNK, :] = xbc_ref[...]
    y = cbias_ref[...] + xp_scr[XP_OFF - tail:XP_OFF - tail + CHUNK, :] * cw_ref[0:1, :]
    for t in range(1, CONV_W):
        y = y + xp_scr[XP_OFF - tail + t:XP_OFF - tail + t + CHUNK, :] * cw_ref[t:t + 1, :]
    new_tail = xp_scr[XP_OFF + CHUNK - tail:XP_OFF + CHUNK, :]
    xp_scr[XP_OFF - tail:XP_OFF, :] = new_tail
    cnew_ref[0] = new_tail
    xc = y * (1.0 / (1.0 + jnp.exp(-y)))
    xs = xc[:, 0:C_WIDTH]

    pre = misc_ref[...] + prm_ref[0:1, :]
    dt = jnp.maximum(pre, 0.0) + jnp.log(1.0 + jnp.exp(-jnp.abs(pre)))
    ad = dt * (-jnp.exp(prm_ref[1:2, :]))
    li = lax.broadcasted_iota(I32, (CHUNK, CHUNK), 0)
    si = lax.broadcasted_iota(I32, (CHUNK, CHUNK), 1)
    causal = li >= si
    tri = causal.astype(BF16)
    acs = sum(jnp.dot(tri, part, preferred_element_type=F32) for part in _split3(ad))
    acs_t = acs.T
    acs_last = acs[CHUNK - 1:CHUNK, :]
    grow = jnp.exp(acs)
    to_end = jnp.exp(acs_last - acs)
    chunk_decay = jnp.exp(acs_last)

    rep = C_HEADS // C_GROUPS
    for g in range(C_GROUPS):
        b_g = xc[:, C_WIDTH + g * C_STATE:C_WIDTH + (g + 1) * C_STATE]
        c_g = xc[:, C_WIDTH + (C_GROUPS + g) * C_STATE:C_WIDTH + (C_GROUPS + g + 1) * C_STATE]
        b_gt = b_g.T.astype(BF16)
        c_gb = c_g.astype(BF16)
        cb = jnp.dot(c_gb, b_gt, preferred_element_type=F32)
        for r in range(rep):
            h = g * rep + r
            col = MISC_DT + h
            xh = xs[:, h * C_HEAD_DIM:(h + 1) * C_HEAD_DIM]
            xd = xh * dt[:, col:col + 1]
            seg = acs[:, col:col + 1] - acs_t[col:col + 1, :]
            decay_in = jnp.where(causal, jnp.exp(jnp.where(causal, seg, 0.0)), 0.0)
            y_diag = jnp.dot((cb * decay_in).astype(BF16), xd.astype(BF16), preferred_element_type=F32)
            st = ht_ref[0, h]
            y_off = jnp.dot(c_gb, st.astype(BF16), preferred_element_type=F32) * grow[:, col:col + 1]
            y_scr[:, h * C_HEAD_DIM:(h + 1) * C_HEAD_DIM] = (
                y_diag + y_off + prm_ref[2:3, col:col + 1] * xh)
            xde = (xd * to_end[:, col:col + 1]).astype(BF16)
            ht_ref[0, h] = st * chunk_decay[:, col:col + 1] + jnp.dot(
                b_gt, xde, preferred_element_type=F32)

    z = z_ref[...]
    gate = y_scr[...] * (z * (1.0 / (1.0 + jnp.exp(-z))))
    gw = C_WIDTH // C_GROUPS
    for g in range(C_GROUPS):
        gg = gate[:, g * gw:(g + 1) * gw]
        ms = jnp.mean(gg * gg, axis=-1, keepdims=True)
        o_ref[:, g * gw:(g + 1) * gw] = gg * lax.rsqrt(ms + EPS) * gn_ref[:, g * gw:(g + 1) * gw]


def _ssd(proj, h0_t, cbuf0, conv_w, conv_b, prm, gate_norm, *, nb, nchunks, row0):
    rb0 = row0 // CHUNK
    row = lambda b, c: rb0 + b * nchunks + c
    const = lambda shape: pl.BlockSpec(shape, lambda b, c: tuple(0 for _ in shape))
    return pl.pallas_call(
        _ssd_kernel,
        grid=(nb, nchunks),
        in_specs=[
            pl.BlockSpec((CHUNK, C_WIDTH), lambda b, c: (row(b, c), COL_CZ // C_WIDTH)),
            pl.BlockSpec((CHUNK, CONV_DIM), lambda b, c: (row(b, c), COL_CXBC // CONV_DIM)),
            pl.BlockSpec((CHUNK, LANES), lambda b, c: (row(b, c), COL_MISC // LANES)),
            pl.BlockSpec((1, C_HEADS, C_STATE, C_HEAD_DIM), lambda b, c: (b, 0, 0, 0)),
            pl.BlockSpec((1, CONV_W - 1, CONV_DIM), lambda b, c: (b, 0, 0)),
            const((CONV_W, CONV_DIM)),
            const((1, CONV_DIM)),
            const((8, LANES)),
            const((1, C_WIDTH)),
        ],
        out_specs=[
            pl.BlockSpec((CHUNK, C_WIDTH), lambda b, c: (b * nchunks + c, 0)),
            pl.BlockSpec((1, C_HEADS, C_STATE, C_HEAD_DIM), lambda b, c: (b, 0, 0, 0)),
            pl.BlockSpec((1, CONV_W - 1, CONV_DIM), lambda b, c: (b, 0, 0)),
        ],
        out_shape=[
            jax.ShapeDtypeStruct((nb * nchunks * CHUNK, C_WIDTH), F32),
            jax.ShapeDtypeStruct((nb, C_HEADS, C_STATE, C_HEAD_DIM), F32),
            jax.ShapeDtypeStruct((nb, CONV_W - 1, CONV_DIM), F32),
        ],
        scratch_shapes=[pltpu.VMEM((XP_OFF + CHUNK, CONV_DIM), F32),
                        pltpu.VMEM((CHUNK, C_WIDTH), F32)],
        compiler_params=_cparams("parallel", "arbitrary"),
        name="ssd_prompt" if nchunks > 1 else "ssd_sample",
    )(proj, proj, proj, h0_t, cbuf0, conv_w, conv_b.reshape(1, CONV_DIM), prm,
      gate_norm.reshape(1, C_WIDTH))


def _outproj_kernel(x_ref, oa_ref, ob_ref, oc_ref, w_ref, o_ref):
    acc = x_ref[...]
    acc = acc + jnp.dot(oa_ref[...].astype(BF16), w_ref[0:A_WIDTH, :], preferred_element_type=F32)
    acc = acc + jnp.dot(ob_ref[...].astype(BF16), w_ref[A_WIDTH:A_WIDTH + B_WIDTH, :],
                        preferred_element_type=F32)
    acc = acc + jnp.dot(oc_ref[...].astype(BF16), w_ref[A_WIDTH + B_WIDTH:, :],
                        preferred_element_type=F32)
    o_ref[...] = acc


def _out_proj(x, oa, ob, oc, w_bf16):
    t, d = x.shape
    tm = _row_tile(t, 512)
    rows = lambda w: pl.BlockSpec((tm, w), lambda i: (i, 0))
    return pl.pallas_call(
        _outproj_kernel,
        grid=(t // tm,),
        in_specs=[rows(d), rows(A_WIDTH), rows(B_WIDTH), rows(C_WIDTH),
                  pl.BlockSpec((d, d), lambda i: (0, 0))],
        out_specs=rows(d),
        out_shape=jax.ShapeDtypeStruct((t, d), F32),
        compiler_params=_cparams("parallel"),
        name="out_proj",
    )(x, oa, ob, oc, w_bf16)


def _ffn_kernel(x_ref, nw_ref, wu_ref, wd_ref, o_ref, h_scr):
    @pl.when(pl.program_id(1) == 0)
    def _():
        x = x_ref[...]
        ms = jnp.mean(x * x, axis=-1, keepdims=True)
        h_scr[...] = (x * lax.rsqrt(ms + EPS) * nw_ref[...]).astype(BF16)
        o_ref[...] = x

    u = jnp.maximum(jnp.dot(h_scr[...], wu_ref[...], preferred_element_type=F32), 0.0)
    o_ref[...] += jnp.dot((u * u).astype(BF16), wd_ref[...], preferred_element_type=F32)


def _ffn(x, norm_w, wu_bf16, wd_bf16):
    t, d = x.shape
    ff = wu_bf16.shape[1]
    tm = _row_tile(t, 512)
    tf = 512
    return pl.pallas_call(
        _ffn_kernel,
        grid=(t // tm, ff // tf),
        in_specs=[
            pl.BlockSpec((tm, d), lambda i, f: (i, 0)),
            pl.BlockSpec((1, d), lambda i, f: (0, 0)),
            pl.BlockSpec((d, tf), lambda i, f: (0, f)),
            pl.BlockSpec((tf, d), lambda i, f: (f, 0)),
        ],
        out_specs=pl.BlockSpec((tm, d), lambda i, f: (i, 0)),
        out_shape=jax.ShapeDtypeStruct((t, d), F32),
        scratch_shapes=[pltpu.VMEM((tm, d), BF16)],
        compiler_params=_cparams("parallel", "arbitrary"),
        name="ffn",
    )(x, norm_w.reshape(1, d), wu_bf16, wd_bf16)


def _norm_kernel(x_ref, nw_ref, o_ref):
    x = x_ref[...]
    ms = jnp.mean(x * x, axis=-1, keepdims=True)
    o_ref[...] = x * lax.rsqrt(ms + EPS) * nw_ref[...]


def _final_norm(x, norm_w):
    t, d = x.shape
    tm = _row_tile(t, 512)
    return pl.pallas_call(
        _norm_kernel,
        grid=(t // tm,),
        in_specs=[pl.BlockSpec((tm, d), lambda i: (i, 0)), pl.BlockSpec((1, d), lambda i: (0, 0))],
        out_specs=pl.BlockSpec((tm, d), lambda i: (i, 0)),
        out_shape=jax.ShapeDtypeStruct((t, d), F32),
        compiler_params=_cparams("parallel"),
        name="final_norm",
    )(x, norm_w.reshape(1, d))


def _regroup_w_in(w):
    d = w.shape[0]
    o = np.cumsum([0, A_WIDTH, A_WIDTH, A_WIDTH, IDX_HEADS * IDX_DIM, IDX_DIM, IDX_HEADS,
                   B_WIDTH, B_WIDTH, B_WIDTH, C_WIDTH, CONV_DIM, C_HEADS])
    aq, ak, av, iq, ik, iw, bq, bk, bv, cz, cxbc, cdt = [w[:, o[i]:o[i + 1]] for i in range(12)]
    pad = jnp.zeros((d, IN_COLS_PAD - COL_MISC - (IDX_DIM + IDX_HEADS + C_HEADS)), w.dtype)
    return jnp.concatenate([cxbc, cz, iq, aq, ak, av, bq, bk, bv, ik, iw, cdt, pad],
                           axis=1).astype(BF16)


def _lane_row(v, off):
    return jnp.zeros((LANES,), F32).at[off:off + v.shape[0]].set(v)


def _trunk(x_prompt, x_sample, cache_a_k, cache_a_v, cache_a_kidx, cache_b_k, cache_b_v,
           state_ssm, state_conv, norm1, w_in, w_out, b_rel, conv_w, conv_b, dt_bias,
           a_log, d_skip, gate_norm, norm2, w_up, w_down, final_norm):
    bp, tp, d = x_prompt.shape
    bs, ts, _ = x_sample.shape
    depth = w_in.shape[0]
    past = cache_a_k.shape[2]
    nbuf = cache_b_k.shape[2]
    n_p = bp * tp
    n_s = bs * ts
    assert ts == CHUNK and tp % LANES == 0 and nbuf == B_WIN
    qs_p = LANES
    s_sample = past + ts
    s_sample_pad = -(-s_sample // LANES) * LANES

    x = jnp.concatenate([x_prompt.reshape(n_p, d), x_sample.reshape(n_s, d)], axis=0)

    pos = jnp.concatenate([jnp.tile(jnp.arange(tp), bp), jnp.tile(past + jnp.arange(ts), bs)])
    tab_a = _rope_tables(pos, A_HEAD_DIM)
    tab_i = _rope_tables(pos, IDX_DIM)
    bias_p = _band_bias(b_rel, qs_p)
    bias_s = _band_bias(b_rel, ts)
    zeros_h = jnp.zeros((bp, C_HEADS, C_STATE, C_HEAD_DIM), F32)
    zeros_c = jnp.zeros((bp, CONV_W - 1, CONV_DIM), F32)

    outs_p, outs_s = [], []
    for l in range(depth):
        proj = _in_proj(x, norm1[l], _regroup_w_in(w_in[l]))
        aq_r, ak_r, iq_r, misc_r = _rope(proj, tab_a, tab_i)

        oa_p = _dsa(aq_r, iq_r, misc_r, ak_r, proj, misc_r, nb=bp, tq=tp, s_keys=tp, qs=qs_p, q_row0=0,
                    v_col=COL_AV // A_WIDTH, mk_col=0, causal=True, limit_all=tp)
        zpad = s_sample_pad - s_sample
        ks = jnp.concatenate([cache_a_k[l].reshape(bs, past, A_WIDTH),
                              ak_r[n_p:].reshape(bs, ts, A_WIDTH),
                              jnp.zeros((bs, zpad, A_WIDTH), F32)], axis=1)
        vs = jnp.concatenate([cache_a_v[l].reshape(bs, past, A_WIDTH),
                              proj[n_p:, COL_AV:COL_AV + A_WIDTH].reshape(bs, ts, A_WIDTH),
                              jnp.zeros((bs, zpad, A_WIDTH), F32)], axis=1)
        kis = jnp.concatenate([cache_a_kidx[l], misc_r[n_p:, :IDX_DIM].reshape(bs, ts, IDX_DIM),
                               jnp.zeros((bs, zpad, IDX_DIM), F32)], axis=1)
        kis = jnp.pad(kis, ((0, 0), (0, 0), (0, LANES - IDX_DIM)))
        oa_s = _dsa(aq_r, iq_r, misc_r, ks.reshape(bs * s_sample_pad, A_WIDTH),
                    vs.reshape(bs * s_sample_pad, A_WIDTH), kis.reshape(bs * s_sample_pad, LANES),
                    nb=bs, tq=ts, s_keys=s_sample_pad, qs=ts, q_row0=n_p, v_col=0, mk_col=0,
                    causal=False, limit_all=s_sample)

        ob_p = _band(proj, proj, proj, bias_p, l, nb=bp, tq=tp, s_keys=tp, qs=qs_p, q_row0=0,
                     q_col=COL_BQ // B_WIDTH, k_col=COL_BK // B_WIDTH, v_col=COL_BV // B_WIDTH,
                     pad=B_WIN)
        bk_new = proj[n_p:, COL_BK:COL_BK + B_WIDTH].reshape(bs, ts, B_WIDTH)
        bv_new = proj[n_p:, COL_BV:COL_BV + B_WIDTH].reshape(bs, ts, B_WIDTH)
        kb = jnp.concatenate([cache_b_k[l].reshape(bs, nbuf, B_WIDTH), bk_new], axis=1)
        vb = jnp.concatenate([cache_b_v[l].reshape(bs, nbuf, B_WIDTH), bv_new], axis=1)
        ob_s = _band(proj, kb.reshape(bs * (nbuf + ts), B_WIDTH), vb.reshape(bs * (nbuf + ts), B_WIDTH),
                     bias_s, l, nb=bs, tq=ts, s_keys=nbuf + ts, qs=ts, q_row0=n_p,
                     q_col=COL_BQ // B_WIDTH, k_col=0, v_col=0, pad=0)

        prm = jnp.zeros((8, LANES), F32)
        prm = prm.at[0].set(_lane_row(dt_bias[l], MISC_DT)).at[1].set(_lane_row(a_log[l], MISC_DT))
        prm = prm.at[2].set(_lane_row(d_skip[l], MISC_DT))
        oc_p, ht_p, conv_p = _ssd(proj, zeros_h, zeros_c, conv_w[l], conv_b[l], prm, gate_norm[l],
                                  nb=bp, nchunks=tp // CHUNK, row0=0)
        oc_s, ht_s, conv_s = _ssd(proj, jnp.swapaxes(state_ssm[l], -1, -2), state_conv[l], conv_w[l],
                                  conv_b[l], prm, gate_norm[l], nb=bs, nchunks=1, row0=n_p)

        oa = jnp.concatenate([oa_p, oa_s], axis=0)
        ob = jnp.concatenate([ob_p, ob_s], axis=0)
        oc = jnp.concatenate([oc_p, oc_s], axis=0)
        x = _out_proj(x, oa, ob, oc, w_out[l].astype(BF16))
        x = _ffn(x, norm2[l], w_up[l].astype(BF16), w_down[l].astype(BF16))

        av = proj[:, COL_AV:COL_AV + A_WIDTH]
        bk = proj[:, COL_BK:COL_BK + B_WIDTH]
        bv = proj[:, COL_BV:COL_BV + B_WIDTH]
        nkeep = min(B_WIN, tp)
        outs_p.append((
            ak_r[:n_p].reshape(bp, tp, A_HEADS, A_HEAD_DIM),
            av[:n_p].reshape(bp, tp, A_HEADS, A_HEAD_DIM),
            misc_r[:n_p, :IDX_DIM].reshape(bp, tp, IDX_DIM),
            bk[:n_p].reshape(bp, tp, B_HEADS, B_HEAD_DIM)[:, tp - nkeep:],
            bv[:n_p].reshape(bp, tp, B_HEADS, B_HEAD_DIM)[:, tp - nkeep:],
            jnp.swapaxes(ht_p, -1, -2),
            conv_p))
        outs_s.append((
            ak_r[n_p:].reshape(bs, ts, A_HEADS, A_HEAD_DIM),
            av[n_p:].reshape(bs, ts, A_HEADS, A_HEAD_DIM),
            misc_r[n_p:, :IDX_DIM].reshape(bs, ts, IDX_DIM),
            bk[n_p:].reshape(bs, ts, B_HEADS, B_HEAD_DIM),
            bv[n_p:].reshape(bs, ts, B_HEADS, B_HEAD_DIM),
            jnp.swapaxes(ht_s, -1, -2),
            conv_s))

    y = _final_norm(x, final_norm)
    y_prompt = y[:n_p].reshape(bp, tp, d)
    y_sample = y[n_p:].reshape(bs, ts, d)
    stack = lambda states: [jnp.stack([s[i] for s in states], axis=0) for i in range(7)]
    return (y_prompt, y_sample, *stack(outs_p), *stack(outs_s))


def kernel(x_prompt, x_sample, cache_a_k, cache_a_v, cache_a_kidx, cache_b_k, cache_b_v, state_ssm, state_conv, norm1, w_in, w_out, b_rel, conv_w, conv_b, dt_bias, a_log, d_skip, gate_norm, norm2, w_up, w_down, final_norm):
    return _trunk(x_prompt, x_sample, cache_a_k, cache_a_v, cache_a_kidx, cache_b_k, cache_b_v,
                  state_ssm, state_conv, norm1, w_in, w_out, b_rel, conv_w, conv_b, dt_bias,
                  a_log, d_skip, gate_norm, norm2, w_up, w_down, final_norm)
```

```python
import functools

import jax
import jax.numpy as jnp
import numpy as np
from jax import lax
from jax.experimental import pallas as pl
from jax.experimental.pallas import tpu as pltpu

F32 = jnp.float32
BF16 = jnp.bfloat16
I32 = jnp.int32

D_MODEL = 2048
CHUNK = 64
A_HEADS = 4
A_HEAD_DIM = 128
A_WIDTH = A_HEADS * A_HEAD_DIM
IDX_HEADS = 16
IDX_DIM = 64
IQ_WIDTH = IDX_HEADS * IDX_DIM
DSA_TOPK = 256
B_HEADS = 4
B_HEAD_DIM = 128
B_WIDTH = B_HEADS * B_HEAD_DIM
B_PREV_CHUNKS = 8
B_WIN = B_PREV_CHUNKS * CHUNK
REL_CLIP = 128
C_WIDTH = 1024
C_HEAD_DIM = 64
C_HEADS = C_WIDTH // C_HEAD_DIM
C_GROUPS = 4
C_STATE = 128
CONV_W = 4
CONV_DIM = C_WIDTH + 2 * C_GROUPS * C_STATE
D_FF = 4 * D_MODEL
ROPE_THETA = 500000.0
EPS = 1e-5

LANES = 128
INT_MIN = -(2 ** 31)
NEG_INF = float("-inf")

COL_AQ = 0
COL_AK = 512
COL_IQ = 1024
COL_AV = 2048
COL_BQ = 2560
COL_BK = 3072
COL_BV = 3584
COL_CXBC = 4096
COL_CZ = 6144
COL_MISC = 7168
MISC_IW = 64
MISC_DT = 80
IN_COLS_PAD = 7680
IN_TN = 512

VMEM_LIMIT = 56 * 1024 * 1024


def _cparams(*sem):
    return pltpu.CompilerParams(dimension_semantics=sem, vmem_limit_bytes=VMEM_LIMIT)


def _row_tile(t, cap):
    tm = cap
    while t % tm:
        tm //= 2
    return tm


def _nt_dot(a, b):
    return lax.dot_general(a, b, (((1,), (1,)), ((), ())), preferred_element_type=F32)


def _split3(x):
    hi = x.astype(BF16)
    r1 = x - hi.astype(F32)
    mid = r1.astype(BF16)
    lo = (r1 - mid.astype(F32)).astype(BF16)
    return hi, mid, lo


def _rope_block(x, c, sm, sp, half):
    return x * c + pltpu.roll(x, LANES - half, 1) * sm + pltpu.roll(x, half, 1) * sp


def _inproj_kernel(x_ref, nw_ref, w_ref, ta_ref, ti_ref, o_ref, h_scr):
    j = pl.program_id(1)

    @pl.when(j == 0)
    def _():
        x = x_ref[...]
        ms = jnp.mean(x * x, axis=-1, keepdims=True)
        h_scr[...] = (x * lax.rsqrt(ms + EPS) * nw_ref[...]).astype(BF16)

    o_ref[...] = jnp.dot(h_scr[...], w_ref[...], preferred_element_type=F32)

    @pl.when(j < COL_IQ // IN_TN)
    def _():
        for h in range(IN_TN // LANES):
            sl = slice(h * LANES, (h + 1) * LANES)
            o_ref[:, sl] = _rope_block(o_ref[:, sl], ta_ref[0], ta_ref[1], ta_ref[2], A_HEAD_DIM // 8)

    @pl.when((j >= COL_IQ // IN_TN) & (j < COL_AV // IN_TN))
    def _():
        for h in range(IN_TN // LANES):
            sl = slice(h * LANES, (h + 1) * LANES)
            o_ref[:, sl] = _rope_block(o_ref[:, sl], ti_ref[0], ti_ref[1], ti_ref[2], IDX_DIM // 8)

    @pl.when(j == COL_MISC // IN_TN)
    def _():
        m = o_ref[:, 0:LANES]
        lane = lax.broadcasted_iota(I32, m.shape, 1)
        rot = _rope_block(m, ti_ref[0], ti_ref[1], ti_ref[2], IDX_DIM // 8)
        o_ref[:, 0:LANES] = jnp.where(lane < IDX_DIM, rot, m)


def _in_proj(x, norm_w, w_bf16, tab_a, tab_i):
    t, d = x.shape
    n = w_bf16.shape[1]
    tm = _row_tile(t, 1024)
    tab = pl.BlockSpec((3, tm, LANES), lambda i, j: (0, i, 0))
    return pl.pallas_call(
        _inproj_kernel,
        grid=(t // tm, n // IN_TN),
        in_specs=[
            pl.BlockSpec((tm, d), lambda i, j: (i, 0)),
            pl.BlockSpec((1, d), lambda i, j: (0, 0)),
            pl.BlockSpec((d, IN_TN), lambda i, j: (0, j)),
            tab, tab,
        ],
        out_specs=pl.BlockSpec((tm, IN_TN), lambda i, j: (i, j)),
        out_shape=jax.ShapeDtypeStruct((t, n), F32),
        scratch_shapes=[pltpu.VMEM((tm, d), BF16)],
        compiler_params=_cparams("parallel", "arbitrary"),
        name="in_proj",
    )(x, norm_w.reshape(1, d), w_bf16, tab_a, tab_i)


def _rope_tables(pos, head_dim):
    rot = head_dim // 4
    half = rot // 2
    inv = ROPE_THETA ** (-jnp.arange(half, dtype=F32) * 2.0 / rot)
    ang = pos.astype(F32)[:, None] * inv[None, :]
    cos, sin = jnp.cos(ang), jnp.sin(ang)
    n = pos.shape[0]
    rest = head_dim - rot
    c = jnp.concatenate([cos, cos, jnp.ones((n, rest), F32)], axis=1)
    sm = jnp.concatenate([-sin, jnp.zeros((n, half + rest), F32)], axis=1)
    sp = jnp.concatenate([jnp.zeros((n, half), F32), sin, jnp.zeros((n, rest), F32)], axis=1)
    tab = jnp.stack([c, sm, sp], axis=0)
    return jnp.tile(tab, (1, 1, LANES // head_dim))


TIE_BLOCK = 64
SUB = 128
DSA_KB_PROMPT = 512
DSA_KB_SAMPLE = 384


def _tree_sum(parts):
    while len(parts) > 1:
        parts = [parts[i] + parts[i + 1] for i in range(0, len(parts) - 1, 2)] + (
            parts[-1:] if len(parts) % 2 else [])
    return parts[0]


def _count_keys(key_scr, nsb, kb, qs, pred):
    def blk(b, acc):
        r0 = pl.multiple_of(b * kb, kb)
        parts = []
        for u in range(kb // SUB):
            hit = jnp.where(pred(key_scr[pl.ds(r0 + u * SUB, SUB), :]), 1, 0).astype(I32)
            parts.append(jnp.sum(hit.reshape(SUB // 8, 8, qs), axis=0))
        return acc + _tree_sum(parts)

    acc = lax.fori_loop(0, nsb, blk, jnp.zeros((8, qs), I32))
    return jnp.sum(acc, axis=0, keepdims=True)


def _dsa_select_attend(q_ref, qi_ref, mq_ref, o_ref, kb_scr, vt_scr, kib_scr, w_scr, score_scr,
                       key_scr, mask_scr, *, nsb, variants, kb, qs, topk, limit):
    w_scr[...] = mq_ref[...].T * (IDX_HEADS ** -0.5 * IDX_DIM ** -0.5)

    def for_block_count(fn):
        if len(variants) == 1:
            fn(variants[0] * kb)
        else:
            for n in variants:
                pl.when(nsb == n)(functools.partial(fn, n * kb))

    def scores(n_keys):
        kib = kib_scr[0:n_keys, :]
        for h in range(IDX_HEADS):
            qh = qi_ref[:, h * IDX_DIM:(h + 1) * IDX_DIM].astype(BF16)
            term = jnp.maximum(_nt_dot(kib, qh), 0.0) * w_scr[MISC_IW + h:MISC_IW + h + 1, :]
            if h == 0:
                score_scr[0:n_keys, :] = term
            else:
                score_scr[0:n_keys, :] += term
        sc = score_scr[0:n_keys, :]
        sc = jnp.where(sc == 0.0, 0.0, sc)
        bits = lax.bitcast_convert_type(sc, I32)
        key = bits ^ ((bits >> 31) & 0x7FFFFFFF)
        s_idx = lax.broadcasted_iota(I32, (n_keys, qs), 0)
        key_scr[0:n_keys, :] = jnp.where(s_idx < limit, key, INT_MIN)

    for_block_count(scores)

    count = functools.partial(_count_keys, key_scr, nsb, kb, qs)
    ans0 = jnp.where(count(lambda x: x >= 0) >= topk, 0, INT_MIN).astype(I32)

    def bit_step(i, ans):
        cand = ans | (jnp.int32(1) << (30 - i))
        return jnp.where(count(lambda x: x >= cand) >= topk, cand, ans)

    thr = lax.fori_loop(0, 31, bit_step, ans0)
    cnt_gt = count(lambda x: x > thr)
    cnt_eq = count(lambda x: x == thr)
    room = topk - cnt_gt
    live = thr > INT_MIN
    all_ties = (cnt_eq <= room) & live

    def mask_blk(b, carry):
        rows = pl.ds(pl.multiple_of(b * kb, kb), kb)
        key = key_scr[rows, :]
        mask_scr[rows, :] = jnp.where((key > thr) | ((key == thr) & all_ties), 0.0, NEG_INF)
        return carry

    lax.fori_loop(0, nsb, mask_blk, 0)
    need_ties = jnp.max(((cnt_eq > room) & live).astype(I32)) > 0

    @pl.when(need_ties)
    def _():
        tri = (lax.broadcasted_iota(I32, (TIE_BLOCK, TIE_BLOCK), 0)
               >= lax.broadcasted_iota(I32, (TIE_BLOCK, TIE_BLOCK), 1)).astype(BF16)

        def blk(b, carry):
            rows = pl.ds(pl.multiple_of(b * TIE_BLOCK, TIE_BLOCK), TIE_BLOCK)
            kblk = key_scr[rows, :]
            eqb = (kblk == thr) & live
            eqf = eqb.astype(F32)
            prefix = jnp.dot(tri, eqf.astype(BF16), preferred_element_type=F32) + carry
            keep = (kblk > thr) | (eqb & (prefix <= room.astype(F32)))
            mask_scr[rows, :] = jnp.where(keep, 0.0, NEG_INF)
            return carry + jnp.sum(eqf, axis=0, keepdims=True)

        lax.fori_loop(0, nsb * (kb // TIE_BLOCK), blk, jnp.zeros((1, qs), F32))

    def attend(n_keys):
        for h in range(A_HEADS):
            sl = slice(h * A_HEAD_DIM, (h + 1) * A_HEAD_DIM)
            qh = q_ref[:, sl].astype(BF16)
            lg = _nt_dot(kb_scr[0:n_keys, sl], qh) * (A_HEAD_DIM ** -0.5) + mask_scr[0:n_keys, :]
            m = jnp.max(lg, axis=0, keepdims=True)
            e = jnp.exp(lg - m)
            den = jnp.sum(e, axis=0, keepdims=True)
            o_t = jnp.dot(vt_scr[sl, 0:n_keys], e.astype(BF16), preferred_element_type=F32)
            o_ref[:, sl] = (o_t / den).T

    for_block_count(attend)


def _dsa_prompt_kernel(q_ref, qi_ref, mq_ref, k_ref, v_ref, mk_ref, o_ref,
                       kb_scr, vt_scr, kib_scr, w_scr, score_scr, key_scr, mask_scr,
                       *, s_keys, qs, kb, topk):
    j = pl.program_id(1)

    @pl.when(j == 0)
    def _():
        kb_scr[...] = k_ref[...].astype(BF16)
        kib_scr[...] = mk_ref[:, 0:IDX_DIM].astype(BF16)
        for i in range(s_keys // kb):
            vt_scr[:, i * kb:(i + 1) * kb] = v_ref[i * kb:(i + 1) * kb, :].T.astype(BF16)

    q_pos = j * qs + lax.broadcasted_iota(I32, (1, qs), 1)
    limit = ((q_pos >> 6) + 1) * CHUNK
    nsb = (j * qs + qs + kb - 1) // kb
    _dsa_select_attend(q_ref, qi_ref, mq_ref, o_ref, kb_scr, vt_scr, kib_scr, w_scr, score_scr,
                       key_scr, mask_scr, nsb=nsb, variants=tuple(range(1, s_keys // kb + 1)),
                       kb=kb, qs=qs, topk=topk, limit=limit)


def _dsa_sample_kernel(q_ref, qi_ref, mq_ref, kc_ref, vc_ref, kic_ref, kn_ref, vn_ref, mkn_ref, buf_ref,
                       o_ref, kb_scr, vt_scr, kib_scr, w_scr, score_scr, key_scr, mask_scr, vrow_scr,
                       *, s_keys, qs, kb, topk, past):
    del buf_ref
    n_new = kn_ref.shape[0]
    live = past + n_new
    kb_scr[0:past, :] = kc_ref[...].astype(BF16)
    kb_scr[past:live, :] = kn_ref[...].astype(BF16)
    kib_scr[0:past, :] = kic_ref[...].astype(BF16)
    kib_scr[past:live, :] = mkn_ref[:, 0:IDX_DIM].astype(BF16)
    vrow_scr[0:past, :] = vc_ref[...]
    vrow_scr[past:live, :] = vn_ref[...]
    if s_keys > live:
        kb_scr[live:, :] = jnp.zeros((s_keys - live, A_WIDTH), BF16)
        kib_scr[live:, :] = jnp.zeros((s_keys - live, IDX_DIM), BF16)
        vrow_scr[live:, :] = jnp.zeros((s_keys - live, A_WIDTH), F32)
    for i in range(s_keys // kb):
        vt_scr[:, i * kb:(i + 1) * kb] = vrow_scr[i * kb:(i + 1) * kb, :].T.astype(BF16)

    limit = jnp.full((1, qs), live, I32)
    _dsa_select_attend(q_ref, qi_ref, mq_ref, o_ref, kb_scr, vt_scr, kib_scr, w_scr, score_scr,
                       key_scr, mask_scr, nsb=s_keys // kb, variants=(s_keys // kb,),
                       kb=kb, qs=qs, topk=topk, limit=limit)


def _dsa_scratch(s_keys, qs):
    return [
        pltpu.VMEM((s_keys, A_WIDTH), BF16),
        pltpu.VMEM((A_WIDTH, s_keys), BF16),
        pltpu.VMEM((s_keys, IDX_DIM), BF16),
        pltpu.VMEM((LANES, qs), F32),
        pltpu.VMEM((s_keys, qs), F32),
        pltpu.VMEM((s_keys, qs), I32),
        pltpu.VMEM((s_keys, qs), F32),
    ]


def _dsa_prompt(proj, *, nb, tq, qs, kb):
    nsteps = tq // qs
    qcol = lambda w, off: pl.BlockSpec((qs, w), lambda b, j: (b * nsteps + j, off // w))
    kcol = lambda w, off: pl.BlockSpec((tq, w), lambda b, j: (b, off // w))
    kern = functools.partial(_dsa_prompt_kernel, s_keys=tq, qs=qs, kb=kb, topk=min(DSA_TOPK, tq // 4))
    return pl.pallas_call(
        kern,
        grid=(nb, nsteps),
        in_specs=[qcol(A_WIDTH, COL_AQ), qcol(IQ_WIDTH, COL_IQ), qcol(LANES, COL_MISC),
                  kcol(A_WIDTH, COL_AK), kcol(A_WIDTH, COL_AV), kcol(LANES, COL_MISC)],
        out_specs=pl.BlockSpec((qs, A_WIDTH), lambda b, j: (b * nsteps + j, 0)),
        out_shape=jax.ShapeDtypeStruct((proj.shape[0], A_WIDTH), F32),
        scratch_shapes=_dsa_scratch(tq, qs),
        compiler_params=_cparams("parallel", "arbitrary"),
        name="dsa_prompt",
    )(proj, proj, proj, proj, proj, proj)


def _dsa_sample(proj, cache_k, cache_v, cache_ki, buf, layer, *, nb, ts, past, row0, kb):
    rb0 = row0 // ts
    live = past + ts
    s_keys = -(-live // kb) * kb
    qcol = lambda w, off: pl.BlockSpec((ts, w), lambda b, j: (rb0 + b, off // w))
    crow = lambda w: pl.BlockSpec((past, w), lambda b, j: (layer * nb + b, 0))
    kern = functools.partial(_dsa_sample_kernel, s_keys=s_keys, qs=ts, kb=kb,
                             topk=min(DSA_TOPK, live // 4), past=past)
    return pl.pallas_call(
        kern,
        grid=(nb, 1),
        in_specs=[qcol(A_WIDTH, COL_AQ), qcol(IQ_WIDTH, COL_IQ), qcol(LANES, COL_MISC),
                  crow(A_WIDTH), crow(A_WIDTH), crow(IDX_DIM),
                  qcol(A_WIDTH, COL_AK), qcol(A_WIDTH, COL_AV), qcol(LANES, COL_MISC),
                  pl.BlockSpec(memory_space=pl.ANY)],
        out_specs=pl.BlockSpec((ts, A_WIDTH), lambda b, j: (rb0 + b, 0)),
        out_shape=jax.ShapeDtypeStruct(buf.shape, F32),
        input_output_aliases={9: 0},
        scratch_shapes=_dsa_scratch(s_keys, ts) + [pltpu.VMEM((s_keys, A_WIDTH), F32)],
        compiler_params=_cparams("parallel", "arbitrary"),
        name="dsa_sample",
    )(proj, proj, proj, cache_k, cache_v, cache_ki, proj, proj, proj, buf)


def _band_bias_kernel(tab_ref, o_ref, *, qs, w):
    layer = pl.program_id(0)
    q = lax.broadcasted_iota(I32, (qs, w), 0)
    jx = lax.broadcasted_iota(I32, (qs, w), 1)
    idx = jnp.clip(B_WIN + q - jx, -REL_CLIP, REL_CLIP) + REL_CLIP
    back = (q >> 6) + B_PREV_CHUNKS - (jx >> 6)
    allowed = (back >= 0) & (back <= B_PREV_CHUNKS)
    for h in range(B_HEADS):
        row = layer * B_HEADS + h

        def body(v, acc):
            return jnp.where(idx == v, tab_ref[row, v], acc)

        acc = lax.fori_loop(0, 2 * REL_CLIP + 1, body, jnp.zeros((qs, w), F32))
        o_ref[0, h] = jnp.where(allowed, acc, NEG_INF)


def _band_bias(b_rel, qs):
    depth = b_rel.shape[0]
    w = qs + B_WIN
    return pl.pallas_call(
        functools.partial(_band_bias_kernel, qs=qs, w=w),
        grid=(depth,),
        in_specs=[pl.BlockSpec(memory_space=pltpu.SMEM)],
        out_specs=pl.BlockSpec((1, B_HEADS, qs, w), lambda l: (l, 0, 0, 0)),
        out_shape=jax.ShapeDtypeStruct((depth, B_HEADS, qs, w), F32),
        compiler_params=_cparams("arbitrary"),
        name=f"band_bias_{qs}",
    )(b_rel.reshape(depth * B_HEADS, 2 * REL_CLIP + 1))


def _band_attend(q_ref, bias_ref, o_ref, kp_scr, vp_scr, start, valid, qs):
    w = qs + B_WIN
    for h in range(B_HEADS):
        sl = slice(h * B_HEAD_DIM, (h + 1) * B_HEAD_DIM)
        qh = q_ref[:, sl].astype(BF16)
        lg = _nt_dot(qh, kp_scr[pl.ds(start, w), sl]) * (B_HEAD_DIM ** -0.5) + bias_ref[0, h]
        if valid is not None:
            lg = jnp.where(valid, lg, NEG_INF)
        m = jnp.max(lg, axis=-1, keepdims=True)
        e = jnp.exp(lg - m)
        den = jnp.sum(e, axis=-1, keepdims=True)
        o = jnp.dot(e.astype(BF16), vp_scr[pl.ds(start, w), sl], preferred_element_type=F32)
        o_ref[:, sl] = o / den


def _band_prompt_kernel(q_ref, k_ref, v_ref, bias_ref, o_ref, kp_scr, vp_scr, *, s_keys, qs):
    j = pl.program_id(1)

    @pl.when(j == 0)
    def _():
        kp_scr[0:B_WIN, :] = jnp.zeros((B_WIN, B_WIDTH), BF16)
        vp_scr[0:B_WIN, :] = jnp.zeros((B_WIN, B_WIDTH), BF16)
        kp_scr[B_WIN:B_WIN + s_keys, :] = k_ref[...].astype(BF16)
        vp_scr[B_WIN:B_WIN + s_keys, :] = v_ref[...].astype(BF16)

    start = pl.multiple_of(j * qs, qs)
    valid = lax.broadcasted_iota(I32, (qs, qs + B_WIN), 1) >= B_WIN - j * qs
    _band_attend(q_ref, bias_ref, o_ref, kp_scr, vp_scr, start, valid, qs)


def _band_sample_kernel(q_ref, kc_ref, vc_ref, kn_ref, vn_ref, bias_ref, buf_ref, o_ref, kp_scr, vp_scr,
                        *, qs):
    del buf_ref
    kp_scr[0:B_WIN, :] = kc_ref[...].astype(BF16)
    vp_scr[0:B_WIN, :] = vc_ref[...].astype(BF16)
    kp_scr[B_WIN:B_WIN + qs, :] = kn_ref[...].astype(BF16)
    vp_scr[B_WIN:B_WIN + qs, :] = vn_ref[...].astype(BF16)
    _band_attend(q_ref, bias_ref, o_ref, kp_scr, vp_scr, 0, None, qs)


def _band_prompt(proj, bias, layer, *, nb, tq, qs):
    nsteps = tq // qs
    w = qs + B_WIN
    kcol = lambda off: pl.BlockSpec((tq, B_WIDTH), lambda b, j: (b, off // B_WIDTH))
    return pl.pallas_call(
        functools.partial(_band_prompt_kernel, s_keys=tq, qs=qs),
        grid=(nb, nsteps),
        in_specs=[
            pl.BlockSpec((qs, B_WIDTH), lambda b, j: (b * nsteps + j, COL_BQ // B_WIDTH)),
            kcol(COL_BK), kcol(COL_BV),
            pl.BlockSpec((1, B_HEADS, qs, w), lambda b, j: (layer, 0, 0, 0)),
        ],
        out_specs=pl.BlockSpec((qs, B_WIDTH), lambda b, j: (b * nsteps + j, 0)),
        out_shape=jax.ShapeDtypeStruct((proj.shape[0], B_WIDTH), F32),
        scratch_shapes=[pltpu.VMEM((B_WIN + tq, B_WIDTH), BF16),
                        pltpu.VMEM((B_WIN + tq, B_WIDTH), BF16)],
        compiler_params=_cparams("parallel", "arbitrary"),
        name="band_prompt",
    )(proj, proj, proj, bias)


def _band_sample(proj, cache_k, cache_v, bias, buf, layer, *, nb, ts, row0):
    rb0 = row0 // ts
    w = ts + B_WIN
    qcol = lambda off: pl.BlockSpec((ts, B_WIDTH), lambda b, j: (rb0 + b, off // B_WIDTH))
    crow = pl.BlockSpec((B_WIN, B_WIDTH), lambda b, j: (layer * nb + b, 0))
    return pl.pallas_call(
        functools.partial(_band_sample_kernel, qs=ts),
        grid=(nb, 1),
        in_specs=[qcol(COL_BQ), crow, crow, qcol(COL_BK), qcol(COL_BV),
                  pl.BlockSpec((1, B_HEADS, ts, w), lambda b, j: (layer, 0, 0, 0)),
                  pl.BlockSpec(memory_space=pl.ANY)],
        out_specs=pl.BlockSpec((ts, B_WIDTH), lambda b, j: (rb0 + b, 0)),
        out_shape=jax.ShapeDtypeStruct(buf.shape, F32),
        input_output_aliases={6: 0},
        scratch_shapes=[pltpu.VMEM((w, B_WIDTH), BF16), pltpu.VMEM((w, B_WIDTH), BF16)],
        compiler_params=_cparams("parallel", "arbitrary"),
        name="band_sample",
    )(proj, cache_k, cache_v, proj, proj, bias, buf)


XP_OFF = 8


def _ssd_kernel(*refs, zero_init):
    if zero_init:
        (z_ref, xbc_ref, misc_ref, cw_ref, cbias_ref, prm_ref, gn_ref,
         o_ref, ht_ref, cnew_ref, xp_scr, y_scr) = refs
    else:
        (z_ref, xbc_ref, misc_ref, h0_ref, cb0_ref, cw_ref, cbias_ref, prm_ref, gn_ref, buf_ref,
         o_ref, ht_ref, cnew_ref, xp_scr, y_scr) = refs
        del buf_ref
    c = pl.program_id(1)
    tail = CONV_W - 1

    @pl.when(c == 0)
    def _():
        if zero_init:
            ht_ref[...] = jnp.zeros(ht_ref.shape, F32)
            xp_scr[XP_OFF - tail:XP_OFF, :] = jnp.zeros((tail, CONV_DIM), F32)
        else:
            ht_ref[...] = h0_ref[...]
            xp_scr[XP_OFF - tail:XP_OFF, :] = cb0_ref[0]

    xp_scr[XP_OFF:XP_OFF + CHUNK, :] = xbc_ref[...]
    y = cbias_ref[...] + xp_scr[XP_OFF - tail:XP_OFF - tail + CHUNK, :] * cw_ref[0:1, :]
    for t in range(1, CONV_W):
        y = y + xp_scr[XP_OFF - tail + t:XP_OFF - tail + t + CHUNK, :] * cw_ref[t:t + 1, :]
    new_tail = xp_scr[XP_OFF + CHUNK - tail:XP_OFF + CHUNK, :]
    xp_scr[XP_OFF - tail:XP_OFF, :] = new_tail
    cnew_ref[0] = new_tail
    xc = y * (1.0 / (1.0 + jnp.exp(-y)))
    xs = xc[:, 0:C_WIDTH]

    pre = misc_ref[...] + prm_ref[0:1, :]
    dt = jnp.maximum(pre, 0.0) + jnp.log(1.0 + jnp.exp(-jnp.abs(pre)))
    ad = dt * (-jnp.exp(prm_ref[1:2, :]))
    li = lax.broadcasted_iota(I32, (CHUNK, CHUNK), 0)
    si = lax.broadcasted_iota(I32, (CHUNK, CHUNK), 1)
    causal = li >= si
    tri = causal.astype(BF16)
    acs = sum(jnp.dot(tri, part, preferred_element_type=F32) for part in _split3(ad))
    acs_t = acs.T
    acs_last = acs[CHUNK - 1:CHUNK, :]
    grow = jnp.exp(acs)
    to_end = jnp.exp(acs_last - acs)
    chunk_decay = jnp.exp(acs_last)

    rep = C_HEADS // C_GROUPS
    for g in range(C_GROUPS):
        b_g = xc[:, C_WIDTH + g * C_STATE:C_WIDTH + (g + 1) * C_STATE]
        c_g = xc[:, C_WIDTH + (C_GROUPS + g) * C_STATE:C_WIDTH + (C_GROUPS + g + 1) * C_STATE]
        b_gt = b_g.T.astype(BF16)
        c_gb = c_g.astype(BF16)
        cb = jnp.dot(c_gb, b_gt, preferred_element_type=F32)
        for r in range(rep):
            h = g * rep + r
            col = MISC_DT + h
            xh = xs[:, h * C_HEAD_DIM:(h + 1) * C_HEAD_DIM]
            xd = xh * dt[:, col:col + 1]
            seg = acs[:, col:col + 1] - acs_t[col:col + 1, :]
            decay_in = jnp.where(causal, jnp.exp(jnp.where(causal, seg, 0.0)), 0.0)
            y_diag = jnp.dot((cb * decay_in).astype(BF16), xd.astype(BF16), preferred_element_type=F32)
            st = ht_ref[0, h]
            y_off = jnp.dot(c_gb, st.astype(BF16), preferred_element_type=F32) * grow[:, col:col + 1]
            y_scr[:, h * C_HEAD_DIM:(h + 1) * C_HEAD_DIM] = (
                y_diag + y_off + prm_ref[2:3, col:col + 1] * xh)
            xde = (xd * to_end[:, col:col + 1]).astype(BF16)
            ht_ref[0, h] = st * chunk_decay[:, col:col + 1] + jnp.dot(
                b_gt, xde, preferred_element_type=F32)

    z = z_ref[...]
    gate = y_scr[...] * (z * (1.0 / (1.0 + jnp.exp(-z))))
    gw = C_WIDTH // C_GROUPS
    for g in range(C_GROUPS):
        gg = gate[:, g * gw:(g + 1) * gw]
        ms = jnp.mean(gg * gg, axis=-1, keepdims=True)
        o_ref[:, g * gw:(g + 1) * gw] = gg * lax.rsqrt(ms + EPS) * gn_ref[:, g * gw:(g + 1) * gw]


def _ssd(proj, conv_w, conv_b, prm, gate_norm, *, nb, nchunks, row0, state=None):
    rb0 = row0 // CHUNK
    row = lambda b, c: rb0 + b * nchunks + c
    const = lambda shape: pl.BlockSpec(shape, lambda b, c: tuple(0 for _ in shape))
    in_specs = [
        pl.BlockSpec((CHUNK, C_WIDTH), lambda b, c: (row(b, c), COL_CZ // C_WIDTH)),
        pl.BlockSpec((CHUNK, CONV_DIM), lambda b, c: (row(b, c), COL_CXBC // CONV_DIM)),
        pl.BlockSpec((CHUNK, LANES), lambda b, c: (row(b, c), COL_MISC // LANES)),
    ]
    args = [proj, proj, proj]
    aliases = {}
    if state is not None:
        h0_t, cbuf0, buf, layer = state
        in_specs += [
            pl.BlockSpec((1, C_HEADS, C_STATE, C_HEAD_DIM), lambda b, c: (layer * nb + b, 0, 0, 0)),
            pl.BlockSpec((1, CONV_W - 1, CONV_DIM), lambda b, c: (layer * nb + b, 0, 0)),
        ]
        args += [h0_t, cbuf0]
    in_specs += [const((CONV_W, CONV_DIM)), const((1, CONV_DIM)), const((8, LANES)), const((1, C_WIDTH))]
    args += [conv_w, conv_b.reshape(1, CONV_DIM), prm, gate_norm.reshape(1, C_WIDTH)]
    if state is not None:
        in_specs.append(pl.BlockSpec(memory_space=pl.ANY))
        args.append(buf)
        aliases = {len(args) - 1: 0}
    return pl.pallas_call(
        functools.partial(_ssd_kernel, zero_init=state is None),
        grid=(nb, nchunks),
        in_specs=in_specs,
        out_specs=[
            pl.BlockSpec((CHUNK, C_WIDTH), lambda b, c: (row(b, c), 0)),
            pl.BlockSpec((1, C_HEADS, C_STATE, C_HEAD_DIM), lambda b, c: (b, 0, 0, 0)),
            pl.BlockSpec((1, CONV_W - 1, CONV_DIM), lambda b, c: (b, 0, 0)),
        ],
        out_shape=[
            jax.ShapeDtypeStruct((proj.shape[0], C_WIDTH), F32),
            jax.ShapeDtypeStruct((nb, C_HEADS, C_STATE, C_HEAD_DIM), F32),
            jax.ShapeDtypeStruct((nb, CONV_W - 1, CONV_DIM), F32),
        ],
        input_output_aliases=aliases,
        scratch_shapes=[pltpu.VMEM((XP_OFF + CHUNK, CONV_DIM), F32),
                        pltpu.VMEM((CHUNK, C_WIDTH), F32)],
        compiler_params=_cparams("parallel", "arbitrary"),
        name="ssd_prompt" if state is None else "ssd_sample",
    )(*args)


def _outproj_kernel(x_ref, oa_ref, ob_ref, oc_ref, w_ref, o_ref):
    acc = x_ref[...]
    acc = acc + jnp.dot(oa_ref[...].astype(BF16), w_ref[0:A_WIDTH, :], preferred_element_type=F32)
    acc = acc + jnp.dot(ob_ref[...].astype(BF16), w_ref[A_WIDTH:A_WIDTH + B_WIDTH, :],
                        preferred_element_type=F32)
    acc = acc + jnp.dot(oc_ref[...].astype(BF16), w_ref[A_WIDTH + B_WIDTH:, :],
                        preferred_element_type=F32)
    o_ref[...] = acc


def _out_proj(x, oa, ob, oc, w_bf16):
    t, d = x.shape
    tm = _row_tile(t, 512)
    rows = lambda w: pl.BlockSpec((tm, w), lambda i: (i, 0))
    return pl.pallas_call(
        _outproj_kernel,
        grid=(t // tm,),
        in_specs=[rows(d), rows(A_WIDTH), rows(B_WIDTH), rows(C_WIDTH),
                  pl.BlockSpec((d, d), lambda i: (0, 0))],
        out_specs=rows(d),
        out_shape=jax.ShapeDtypeStruct((t, d), F32),
        compiler_params=_cparams("parallel"),
        name="out_proj",
    )(x, oa, ob, oc, w_bf16)


def _ffn_kernel(x_ref, nw_ref, wu_ref, wd_ref, o_ref, h_scr):
    @pl.when(pl.program_id(1) == 0)
    def _():
        x = x_ref[...]
        ms = jnp.mean(x * x, axis=-1, keepdims=True)
        h_scr[...] = (x * lax.rsqrt(ms + EPS) * nw_ref[...]).astype(BF16)
        o_ref[...] = x

    u = jnp.maximum(jnp.dot(h_scr[...], wu_ref[...], preferred_element_type=F32), 0.0)
    o_ref[...] += jnp.dot((u * u).astype(BF16), wd_ref[...], preferred_element_type=F32)


def _ffn(x, norm_w, wu_bf16, wd_bf16):
    t, d = x.shape
    ff = wu_bf16.shape[1]
    tm = _row_tile(t, 512)
    tf = 512
    return pl.pallas_call(
        _ffn_kernel,
        grid=(t // tm, ff // tf),
        in_specs=[
            pl.BlockSpec((tm, d), lambda i, f: (i, 0)),
            pl.BlockSpec((1, d), lambda i, f: (0, 0)),
            pl.BlockSpec((d, tf), lambda i, f: (0, f)),
            pl.BlockSpec((tf, d), lambda i, f: (f, 0)),
        ],
        out_specs=pl.BlockSpec((tm, d), lambda i, f: (i, 0)),
        out_shape=jax.ShapeDtypeStruct((t, d), F32),
        scratch_shapes=[pltpu.VMEM((tm, d), BF16)],
        compiler_params=_cparams("parallel", "arbitrary"),
        name="ffn",
    )(x, norm_w.reshape(1, d), wu_bf16, wd_bf16)


def _norm_kernel(x_ref, nw_ref, o_ref):
    x = x_ref[...]
    ms = jnp.mean(x * x, axis=-1, keepdims=True)
    o_ref[...] = x * lax.rsqrt(ms + EPS) * nw_ref[...]


def _final_norm(x, norm_w):
    t, d = x.shape
    tm = _row_tile(t, 512)
    return pl.pallas_call(
        _norm_kernel,
        grid=(t // tm,),
        in_specs=[pl.BlockSpec((tm, d), lambda i: (i, 0)), pl.BlockSpec((1, d), lambda i: (0, 0))],
        out_specs=pl.BlockSpec((tm, d), lambda i: (i, 0)),
        out_shape=jax.ShapeDtypeStruct((t, d), F32),
        compiler_params=_cparams("parallel"),
        name="final_norm",
    )(x, norm_w.reshape(1, d))


def _regroup_w_in(w):
    d = w.shape[0]
    o = np.cumsum([0, A_WIDTH, A_WIDTH, A_WIDTH, IQ_WIDTH, IDX_DIM, IDX_HEADS,
                   B_WIDTH, B_WIDTH, B_WIDTH, C_WIDTH, CONV_DIM, C_HEADS])
    aq, ak, av, iq, ik, iw, bq, bk, bv, cz, cxbc, cdt = [w[:, o[i]:o[i + 1]] for i in range(12)]
    pad = jnp.zeros((d, IN_COLS_PAD - COL_MISC - (IDX_DIM + IDX_HEADS + C_HEADS)), w.dtype)
    return jnp.concatenate([aq, ak, iq, av, bq, bk, bv, cxbc, cz, ik, iw, cdt, pad],
                           axis=1).astype(BF16)


def _lane_row(v, off):
    return jnp.zeros((LANES,), F32).at[off:off + v.shape[0]].set(v)


def _trunk(x_prompt, x_sample, cache_a_k, cache_a_v, cache_a_kidx, cache_b_k, cache_b_v,
           state_ssm, state_conv, norm1, w_in, w_out, b_rel, conv_w, conv_b, dt_bias,
           a_log, d_skip, gate_norm, norm2, w_up, w_down, final_norm):
    bp, tp, d = x_prompt.shape
    bs, ts, _ = x_sample.shape
    depth = w_in.shape[0]
    past = cache_a_k.shape[2]
    nbuf = cache_b_k.shape[2]
    n_p = bp * tp
    n_s = bs * ts
    assert ts == CHUNK and tp % DSA_KB_PROMPT == 0 and nbuf == B_WIN
    qs_p = LANES

    x = jnp.concatenate([x_prompt.reshape(n_p, d), x_sample.reshape(n_s, d)], axis=0)

    pos = jnp.concatenate([jnp.tile(jnp.arange(tp), bp), jnp.tile(past + jnp.arange(ts), bs)])
    tab_a = _rope_tables(pos, A_HEAD_DIM)
    tab_i = _rope_tables(pos, IDX_DIM)
    bias_p = _band_bias(b_rel, qs_p)
    bias_s = _band_bias(b_rel, ts)
    cak = cache_a_k.reshape(depth * bs * past, A_WIDTH)
    cav = cache_a_v.reshape(depth * bs * past, A_WIDTH)
    caki = cache_a_kidx.reshape(depth * bs * past, IDX_DIM)
    cbk = cache_b_k.reshape(depth * bs * nbuf, B_WIDTH)
    cbv = cache_b_v.reshape(depth * bs * nbuf, B_WIDTH)
    h0_t = jnp.swapaxes(state_ssm, -1, -2).reshape(depth * bs, C_HEADS, C_STATE, C_HEAD_DIM)
    cbuf0 = state_conv.reshape(depth * bs, CONV_W - 1, CONV_DIM)

    outs_p, outs_s = [], []
    for l in range(depth):
        proj = _in_proj(x, norm1[l], _regroup_w_in(w_in[l]), tab_a, tab_i)

        oa = _dsa_prompt(proj, nb=bp, tq=tp, qs=qs_p, kb=DSA_KB_PROMPT)
        oa = _dsa_sample(proj, cak, cav, caki, oa, l, nb=bs, ts=ts, past=past, row0=n_p,
                         kb=DSA_KB_SAMPLE)
        ob = _band_prompt(proj, bias_p, l, nb=bp, tq=tp, qs=qs_p)
        ob = _band_sample(proj, cbk, cbv, bias_s, ob, l, nb=bs, ts=ts, row0=n_p)
        prm = jnp.zeros((8, LANES), F32)
        prm = prm.at[0].set(_lane_row(dt_bias[l], MISC_DT)).at[1].set(_lane_row(a_log[l], MISC_DT))
        prm = prm.at[2].set(_lane_row(d_skip[l], MISC_DT))
        oc, ht_p, conv_p = _ssd(proj, conv_w[l], conv_b[l], prm, gate_norm[l],
                                nb=bp, nchunks=tp // CHUNK, row0=0)
        oc, ht_s, conv_s = _ssd(proj, conv_w[l], conv_b[l], prm, gate_norm[l],
                                nb=bs, nchunks=1, row0=n_p, state=(h0_t, cbuf0, oc, l))

        x = _out_proj(x, oa, ob, oc, w_out[l].astype(BF16))
        x = _ffn(x, norm2[l], w_up[l].astype(BF16), w_down[l].astype(BF16))

        ak = proj[:, COL_AK:COL_AK + A_WIDTH]
        av = proj[:, COL_AV:COL_AV + A_WIDTH]
        ik = proj[:, COL_MISC:COL_MISC + IDX_DIM]
        bk = proj[:, COL_BK:COL_BK + B_WIDTH]
        bv = proj[:, COL_BV:COL_BV + B_WIDTH]
        nkeep = min(B_WIN, tp)
        outs_p.append((
            ak[:n_p].reshape(bp, tp, A_HEADS, A_HEAD_DIM),
            av[:n_p].reshape(bp, tp, A_HEADS, A_HEAD_DIM),
            ik[:n_p].reshape(bp, tp, IDX_DIM),
            bk[:n_p].reshape(bp, tp, B_HEADS, B_HEAD_DIM)[:, tp - nkeep:],
            bv[:n_p].reshape(bp, tp, B_HEADS, B_HEAD_DIM)[:, tp - nkeep:],
            jnp.swapaxes(ht_p, -1, -2),
            conv_p))
        outs_s.append((
            ak[n_p:].reshape(bs, ts, A_HEADS, A_HEAD_DIM),
            av[n_p:].reshape(bs, ts, A_HEADS, A_HEAD_DIM),
            ik[n_p:].reshape(bs, ts, IDX_DIM),
            bk[n_p:].reshape(bs, ts, B_HEADS, B_HEAD_DIM),
            bv[n_p:].reshape(bs, ts, B_HEADS, B_HEAD_DIM),
            jnp.swapaxes(ht_s, -1, -2),
            conv_s))

    y = _final_norm(x, final_norm)
    y_prompt = y[:n_p].reshape(bp, tp, d)
    y_sample = y[n_p:].reshape(bs, ts, d)
    stack = lambda states: [jnp.stack([s[i] for s in states], axis=0) for i in range(7)]
    return (y_prompt, y_sample, *stack(outs_p), *stack(outs_s))


def kernel(x_prompt, x_sample, cache_a_k, cache_a_v, cache_a_kidx, cache_b_k, cache_b_v, state_ssm, state_conv, norm1, w_in, w_out, b_rel, conv_w, conv_b, dt_bias, a_log, d_skip, gate_norm, norm2, w_up, w_down, final_norm):
    return _trunk(x_prompt, x_sample, cache_a_k, cache_a_v, cache_a_kidx, cache_b_k, cache_b_v,
                  state_ssm, state_conv, norm1, w_in, w_out, b_rel, conv_w, conv_b, dt_bias,
                  a_log, d_skip, gate_norm, norm2, w_up, w_down, final_norm)
```

```python
import functools

import jax
import jax.numpy as jnp
import numpy as np
from jax import lax
from jax.experimental import pallas as pl
from jax.experimental.pallas import tpu as pltpu

F32 = jnp.float32
BF16 = jnp.bfloat16
I32 = jnp.int32

D_MODEL = 2048
CHUNK = 64
A_HEADS = 4
A_HEAD_DIM = 128
A_WIDTH = A_HEADS * A_HEAD_DIM
IDX_HEADS = 16
IDX_DIM = 64
IQ_WIDTH = IDX_HEADS * IDX_DIM
DSA_TOPK = 256
B_HEADS = 4
B_HEAD_DIM = 128
B_WIDTH = B_HEADS * B_HEAD_DIM
B_PREV_CHUNKS = 8
B_WIN = B_PREV_CHUNKS * CHUNK
REL_CLIP = 128
C_WIDTH = 1024
C_HEAD_DIM = 64
C_HEADS = C_WIDTH // C_HEAD_DIM
C_GROUPS = 4
C_STATE = 128
CONV_W = 4
CONV_DIM = C_WIDTH + 2 * C_GROUPS * C_STATE
D_FF = 4 * D_MODEL
ROPE_THETA = 500000.0
EPS = 1e-5

LANES = 128
INT_MIN = -(2 ** 31)
NEG_INF = float("-inf")

COL_AQ = 0
COL_AK = 512
COL_IQ = 1024
COL_AV = 2048
COL_BQ = 2560
COL_BK = 3072
COL_BV = 3584
COL_CXBC = 4096
COL_CZ = 6144
COL_MISC = 7168
MISC_IW = 64
MISC_DT = 80
IN_COLS_PAD = 7680
IN_TN = 512
SRC_AQ, SRC_AV, SRC_IQ, SRC_IK, SRC_BQ, SRC_CZ, SRC_CXBC, SRC_DT, SRC_END = (
    0, 1024, 1536, 2560, 2640, 4176, 5200, 7248, 7264)

VMEM_LIMIT = 56 * 1024 * 1024
ANY = pl.BlockSpec(memory_space=pl.ANY)


def _cparams(*sem):
    return pltpu.CompilerParams(dimension_semantics=sem, vmem_limit_bytes=VMEM_LIMIT)


def _row_tile(t, cap):
    tm = cap
    while t % tm:
        tm //= 2
    return tm


def _nt_dot(a, b):
    return lax.dot_general(a, b, (((1,), (1,)), ((), ())), preferred_element_type=F32)


def _split3(x):
    hi = x.astype(BF16)
    r1 = x - hi.astype(F32)
    mid = r1.astype(BF16)
    lo = (r1 - mid.astype(F32)).astype(BF16)
    return hi, mid, lo


def _head_rows(ref, h, n, heads):
    return ref.at[pl.ds(h, n, stride=heads), :]


def _uninit_kernel(o_ref):
    del o_ref


def _uninit(shape, dtype=F32):
    return pl.pallas_call(_uninit_kernel, out_specs=ANY, out_shape=jax.ShapeDtypeStruct(shape, dtype),
                          name="alloc")()


def _regroup_kernel(w_ref, o_ref):
    def put(dst, src, width):
        o_ref[0, :, dst:dst + width] = w_ref[0, :, src:src + width].astype(BF16)

    rows = w_ref.shape[1]
    put(COL_AQ, SRC_AQ, 2 * A_WIDTH)
    put(COL_IQ, SRC_IQ, IQ_WIDTH)
    put(COL_AV, SRC_AV, A_WIDTH)
    put(COL_BQ, SRC_BQ, 3 * B_WIDTH)
    put(COL_CXBC, SRC_CXBC, CONV_DIM)
    put(COL_CZ, SRC_CZ, C_WIDTH)
    o_ref[0, :, COL_MISC:] = jnp.zeros((rows, IN_COLS_PAD - COL_MISC), BF16)
    put(COL_MISC, SRC_IK, IDX_DIM + IDX_HEADS)
    put(COL_MISC + MISC_DT, SRC_DT, C_HEADS)


def _regroup_w_in(w_in):
    depth, d, n = w_in.shape
    assert n == SRC_END
    tk = 256
    return pl.pallas_call(
        _regroup_kernel,
        grid=(depth, d // tk),
        in_specs=[pl.BlockSpec((1, tk, n), lambda l, i: (l, i, 0))],
        out_specs=pl.BlockSpec((1, tk, IN_COLS_PAD), lambda l, i: (l, i, 0)),
        out_shape=jax.ShapeDtypeStruct((depth, d, IN_COLS_PAD), BF16),
        compiler_params=_cparams("parallel", "parallel"),
        name="regroup_w_in",
    )(w_in)


def _rope_block(x, c, sm, sp, half):
    return x * c + pltpu.roll(x, LANES - half, 1) * sm + pltpu.roll(x, half, 1) * sp


def _inproj_kernel(x_ref, nw_ref, w_ref, ta_ref, ti_ref, o_ref, h_scr):
    j = pl.program_id(1)

    @pl.when(j == 0)
    def _():
        x = x_ref[...]
        ms = jnp.mean(x * x, axis=-1, keepdims=True)
        h_scr[...] = (x * lax.rsqrt(ms + EPS) * nw_ref[...]).astype(BF16)

    o_ref[...] = jnp.dot(h_scr[...], w_ref[0], preferred_element_type=F32)

    @pl.when(j < COL_IQ // IN_TN)
    def _():
        for h in range(IN_TN // LANES):
            sl = slice(h * LANES, (h + 1) * LANES)
            o_ref[:, sl] = _rope_block(o_ref[:, sl], ta_ref[0], ta_ref[1], ta_ref[2], A_HEAD_DIM // 8)

    @pl.when((j >= COL_IQ // IN_TN) & (j < COL_AV // IN_TN))
    def _():
        for h in range(IN_TN // LANES):
            sl = slice(h * LANES, (h + 1) * LANES)
            o_ref[:, sl] = _rope_block(o_ref[:, sl], ti_ref[0], ti_ref[1], ti_ref[2], IDX_DIM // 8)

    @pl.when(j == COL_MISC // IN_TN)
    def _():
        m = o_ref[:, 0:LANES]
        lane = lax.broadcasted_iota(I32, m.shape, 1)
        rot = _rope_block(m, ti_ref[0], ti_ref[1], ti_ref[2], IDX_DIM // 8)
        o_ref[:, 0:LANES] = jnp.where(lane < IDX_DIM, rot, m)


def _in_proj(x, norm_w, w_all, layer, tab_a, tab_i):
    t, d = x.shape
    n = w_all.shape[2]
    tm = _row_tile(t, 1024)
    tab = pl.BlockSpec((3, tm, LANES), lambda i, j: (0, i, 0))
    return pl.pallas_call(
        _inproj_kernel,
        grid=(t // tm, n // IN_TN),
        in_specs=[
            pl.BlockSpec((tm, d), lambda i, j: (i, 0)),
            pl.BlockSpec((1, d), lambda i, j: (0, 0)),
            pl.BlockSpec((1, d, IN_TN), lambda i, j: (layer, 0, j)),
            tab, tab,
        ],
        out_specs=pl.BlockSpec((tm, IN_TN), lambda i, j: (i, j)),
        out_shape=jax.ShapeDtypeStruct((t, n), F32),
        scratch_shapes=[pltpu.VMEM((tm, d), BF16)],
        compiler_params=_cparams("parallel", "arbitrary"),
        name="in_proj",
    )(x, norm_w.reshape(1, d), w_all, tab_a, tab_i)


def _rope_tables(pos, head_dim):
    rot = head_dim // 4
    half = rot // 2
    inv = ROPE_THETA ** (-jnp.arange(half, dtype=F32) * 2.0 / rot)
    ang = pos.astype(F32)[:, None] * inv[None, :]
    cos, sin = jnp.cos(ang), jnp.sin(ang)
    n = pos.shape[0]
    rest = head_dim - rot
    c = jnp.concatenate([cos, cos, jnp.ones((n, rest), F32)], axis=1)
    sm = jnp.concatenate([-sin, jnp.zeros((n, half + rest), F32)], axis=1)
    sp = jnp.concatenate([jnp.zeros((n, half), F32), sin, jnp.zeros((n, rest), F32)], axis=1)
    tab = jnp.stack([c, sm, sp], axis=0)
    return jnp.tile(tab, (1, 1, LANES // head_dim))


TIE_BLOCK = 64
SUB = 128
DSA_KB_PROMPT = 512
DSA_KB_SAMPLE = 384


def _tree_sum(parts):
    while len(parts) > 1:
        parts = [parts[i] + parts[i + 1] for i in range(0, len(parts) - 1, 2)] + (
            parts[-1:] if len(parts) % 2 else [])
    return parts[0]


def _count_keys(key_scr, nsb, kb, qs, pred):
    def blk(b, acc):
        r0 = pl.multiple_of(b * kb, kb)
        parts = []
        for u in range(kb // SUB):
            hit = jnp.where(pred(key_scr[pl.ds(r0 + u * SUB, SUB), :]), 1, 0).astype(I32)
            parts.append(jnp.sum(hit.reshape(SUB // 8, 8, qs), axis=0))
        return acc + _tree_sum(parts)

    acc = lax.fori_loop(0, nsb, blk, jnp.zeros((8, qs), I32))
    return jnp.sum(acc, axis=0, keepdims=True)


def _dsa_select_attend(q_ref, qi_ref, mq_ref, o_ref, kb_scr, vt_scr, kib_scr, w_scr, score_scr,
                       key_scr, mask_scr, *, nsb, variants, kb, qs, topk, limit):
    w_scr[...] = mq_ref[...].T * (IDX_HEADS ** -0.5 * IDX_DIM ** -0.5)

    def for_block_count(fn):
        if len(variants) == 1:
            fn(variants[0] * kb)
        else:
            for n in variants:
                pl.when(nsb == n)(functools.partial(fn, n * kb))

    def scores(n_keys):
        kib = kib_scr[0:n_keys, :]
        for h in range(IDX_HEADS):
            qh = qi_ref[:, h * IDX_DIM:(h + 1) * IDX_DIM].astype(BF16)
            term = jnp.maximum(_nt_dot(kib, qh), 0.0) * w_scr[MISC_IW + h:MISC_IW + h + 1, :]
            if h == 0:
                score_scr[0:n_keys, :] = term
            else:
                score_scr[0:n_keys, :] += term
        sc = score_scr[0:n_keys, :]
        sc = jnp.where(sc == 0.0, 0.0, sc)
        bits = lax.bitcast_convert_type(sc, I32)
        key = bits ^ ((bits >> 31) & 0x7FFFFFFF)
        s_idx = lax.broadcasted_iota(I32, (n_keys, qs), 0)
        key_scr[0:n_keys, :] = jnp.where(s_idx < limit, key, INT_MIN)

    for_block_count(scores)

    count = functools.partial(_count_keys, key_scr, nsb, kb, qs)
    ans0 = jnp.where(count(lambda x: x >= 0) >= topk, 0, INT_MIN).astype(I32)

    def bit_step(i, ans):
        cand = ans | (jnp.int32(1) << (30 - i))
        return jnp.where(count(lambda x: x >= cand) >= topk, cand, ans)

    thr = lax.fori_loop(0, 31, bit_step, ans0)
    cnt_gt = count(lambda x: x > thr)
    cnt_eq = count(lambda x: x == thr)
    room = topk - cnt_gt
    live = thr > INT_MIN
    all_ties = (cnt_eq <= room) & live

    def mask_blk(b, carry):
        rows = pl.ds(pl.multiple_of(b * kb, kb), kb)
        key = key_scr[rows, :]
        mask_scr[rows, :] = jnp.where((key > thr) | ((key == thr) & all_ties), 0.0, NEG_INF)
        return carry

    lax.fori_loop(0, nsb, mask_blk, 0)
    need_ties = jnp.max(((cnt_eq > room) & live).astype(I32)) > 0

    @pl.when(need_ties)
    def _():
        tri = (lax.broadcasted_iota(I32, (TIE_BLOCK, TIE_BLOCK), 0)
               >= lax.broadcasted_iota(I32, (TIE_BLOCK, TIE_BLOCK), 1)).astype(BF16)

        def blk(b, carry):
            rows = pl.ds(pl.multiple_of(b * TIE_BLOCK, TIE_BLOCK), TIE_BLOCK)
            kblk = key_scr[rows, :]
            eqb = (kblk == thr) & live
            eqf = eqb.astype(F32)
            prefix = jnp.dot(tri, eqf.astype(BF16), preferred_element_type=F32) + carry
            keep = (kblk > thr) | (eqb & (prefix <= room.astype(F32)))
            mask_scr[rows, :] = jnp.where(keep, 0.0, NEG_INF)
            return carry + jnp.sum(eqf, axis=0, keepdims=True)

        lax.fori_loop(0, nsb * (kb // TIE_BLOCK), blk, jnp.zeros((1, qs), F32))

    def attend(n_keys):
        for h in range(A_HEADS):
            sl = slice(h * A_HEAD_DIM, (h + 1) * A_HEAD_DIM)
            qh = q_ref[:, sl].astype(BF16)
            lg = _nt_dot(kb_scr[0:n_keys, sl], qh) * (A_HEAD_DIM ** -0.5) + mask_scr[0:n_keys, :]
            m = jnp.max(lg, axis=0, keepdims=True)
            e = jnp.exp(lg - m)
            den = jnp.sum(e, axis=0, keepdims=True)
            o_t = jnp.dot(vt_scr[sl, 0:n_keys], e.astype(BF16), preferred_element_type=F32)
            o_ref[:, sl] = (o_t / den).T

    for_block_count(attend)


def _dsa_prompt_kernel(q_ref, qi_ref, mq_ref, k_ref, v_ref, mk_ref, bufk_ref, bufv_ref, bufi_ref,
                       o_ref, ck_ref, cv_ref, ci_ref,
                       kb_scr, vt_scr, kib_scr, w_scr, score_scr, key_scr, mask_scr,
                       *, s_keys, qs, kb, topk):
    del bufk_ref, bufv_ref, bufi_ref
    j = pl.program_id(1)

    @pl.when(j == 0)
    def _():
        kb_scr[...] = k_ref[...].astype(BF16)
        kib_scr[...] = mk_ref[:, 0:IDX_DIM].astype(BF16)
        for i in range(s_keys // kb):
            vt_scr[:, i * kb:(i + 1) * kb] = v_ref[i * kb:(i + 1) * kb, :].T.astype(BF16)
        ci_ref[...] = mk_ref[:, 0:IDX_DIM]
        for h in range(A_HEADS):
            sl = slice(h * A_HEAD_DIM, (h + 1) * A_HEAD_DIM)
            _head_rows(ck_ref, h, s_keys, A_HEADS)[...] = k_ref[:, sl]
            _head_rows(cv_ref, h, s_keys, A_HEADS)[...] = v_ref[:, sl]

    q_pos = j * qs + lax.broadcasted_iota(I32, (1, qs), 1)
    limit = ((q_pos >> 6) + 1) * CHUNK
    nsb = (j * qs + qs + kb - 1) // kb
    _dsa_select_attend(q_ref, qi_ref, mq_ref, o_ref, kb_scr, vt_scr, kib_scr, w_scr, score_scr,
                       key_scr, mask_scr, nsb=nsb, variants=tuple(range(1, s_keys // kb + 1)),
                       kb=kb, qs=qs, topk=topk, limit=limit)


def _dsa_sample_kernel(q_ref, qi_ref, mq_ref, kc_ref, vc_ref, kic_ref, kn_ref, vn_ref, mkn_ref,
                       bufo_ref, bufk_ref, bufv_ref, bufi_ref,
                       o_ref, ck_ref, cv_ref, ci_ref,
                       kb_scr, vt_scr, kib_scr, w_scr, score_scr, key_scr, mask_scr, vrow_scr,
                       *, s_keys, qs, kb, topk, past):
    del bufo_ref, bufk_ref, bufv_ref, bufi_ref
    n_new = kn_ref.shape[0]
    live = past + n_new
    for h in range(A_HEADS):
        sl = slice(h * A_HEAD_DIM, (h + 1) * A_HEAD_DIM)
        kb_scr[0:past, sl] = _head_rows(kc_ref, h, past, A_HEADS)[...].astype(BF16)
        vrow_scr[0:past, sl] = _head_rows(vc_ref, h, past, A_HEADS)[...]
        _head_rows(ck_ref, h, n_new, A_HEADS)[...] = kn_ref[:, sl]
        _head_rows(cv_ref, h, n_new, A_HEADS)[...] = vn_ref[:, sl]
    ci_ref[...] = mkn_ref[:, 0:IDX_DIM]
    kb_scr[past:live, :] = kn_ref[...].astype(BF16)
    kib_scr[0:past, :] = kic_ref[...].astype(BF16)
    kib_scr[past:live, :] = mkn_ref[:, 0:IDX_DIM].astype(BF16)
    vrow_scr[past:live, :] = vn_ref[...]
    if s_keys > live:
        kb_scr[live:, :] = jnp.zeros((s_keys - live, A_WIDTH), BF16)
        kib_scr[live:, :] = jnp.zeros((s_keys - live, IDX_DIM), BF16)
        vrow_scr[live:, :] = jnp.zeros((s_keys - live, A_WIDTH), F32)
    for i in range(s_keys // kb):
        vt_scr[:, i * kb:(i + 1) * kb] = vrow_scr[i * kb:(i + 1) * kb, :].T.astype(BF16)

    limit = jnp.full((1, qs), live, I32)
    _dsa_select_attend(q_ref, qi_ref, mq_ref, o_ref, kb_scr, vt_scr, kib_scr, w_scr, score_scr,
                       key_scr, mask_scr, nsb=s_keys // kb, variants=(s_keys // kb,),
                       kb=kb, qs=qs, topk=topk, limit=limit)


def _dsa_scratch(s_keys, qs):
    return [
        pltpu.VMEM((s_keys, A_WIDTH), BF16),
        pltpu.VMEM((A_WIDTH, s_keys), BF16),
        pltpu.VMEM((s_keys, IDX_DIM), BF16),
        pltpu.VMEM((LANES, qs), F32),
        pltpu.VMEM((s_keys, qs), F32),
        pltpu.VMEM((s_keys, qs), I32),
        pltpu.VMEM((s_keys, qs), F32),
    ]


def _dsa_prompt(proj, out_k, out_v, out_ki, layer, *, nb, tq, qs, kb):
    nsteps = tq // qs
    qcol = lambda w, off: pl.BlockSpec((qs, w), lambda b, j: (b * nsteps + j, off // w))
    kcol = lambda w, off: pl.BlockSpec((tq, w), lambda b, j: (b, off // w))
    cache = lambda rows, w: pl.BlockSpec((rows, w), lambda b, j: (layer * nb + b, 0))
    kern = functools.partial(_dsa_prompt_kernel, s_keys=tq, qs=qs, kb=kb, topk=min(DSA_TOPK, tq // 4))
    return pl.pallas_call(
        kern,
        grid=(nb, nsteps),
        in_specs=[qcol(A_WIDTH, COL_AQ), qcol(IQ_WIDTH, COL_IQ), qcol(LANES, COL_MISC),
                  kcol(A_WIDTH, COL_AK), kcol(A_WIDTH, COL_AV), kcol(LANES, COL_MISC), ANY, ANY, ANY],
        out_specs=[pl.BlockSpec((qs, A_WIDTH), lambda b, j: (b * nsteps + j, 0)),
                   cache(tq * A_HEADS, A_HEAD_DIM), cache(tq * A_HEADS, A_HEAD_DIM), cache(tq, IDX_DIM)],
        out_shape=[jax.ShapeDtypeStruct((proj.shape[0], A_WIDTH), F32),
                   jax.ShapeDtypeStruct(out_k.shape, F32), jax.ShapeDtypeStruct(out_v.shape, F32),
                   jax.ShapeDtypeStruct(out_ki.shape, F32)],
        input_output_aliases={6: 1, 7: 2, 8: 3},
        scratch_shapes=_dsa_scratch(tq, qs),
        compiler_params=_cparams("parallel", "arbitrary"),
        name="dsa_prompt",
    )(proj, proj, proj, proj, proj, proj, out_k, out_v, out_ki)


def _dsa_sample(proj, cache_k, cache_v, cache_ki, buf, out_k, out_v, out_ki, layer,
                *, nb, ts, past, row0, kb):
    rb0 = row0 // ts
    live = past + ts
    s_keys = -(-live // kb) * kb
    qcol = lambda w, off: pl.BlockSpec((ts, w), lambda b, j: (rb0 + b, off // w))
    per_seq = lambda rows, w: pl.BlockSpec((rows, w), lambda b, j: (layer * nb + b, 0))
    kern = functools.partial(_dsa_sample_kernel, s_keys=s_keys, qs=ts, kb=kb,
                             topk=min(DSA_TOPK, live // 4), past=past)
    return pl.pallas_call(
        kern,
        grid=(nb, 1),
        in_specs=[qcol(A_WIDTH, COL_AQ), qcol(IQ_WIDTH, COL_IQ), qcol(LANES, COL_MISC),
                  per_seq(past * A_HEADS, A_HEAD_DIM), per_seq(past * A_HEADS, A_HEAD_DIM),
                  per_seq(past, IDX_DIM),
                  qcol(A_WIDTH, COL_AK), qcol(A_WIDTH, COL_AV), qcol(LANES, COL_MISC),
                  ANY, ANY, ANY, ANY],
        out_specs=[pl.BlockSpec((ts, A_WIDTH), lambda b, j: (rb0 + b, 0)),
                   per_seq(ts * A_HEADS, A_HEAD_DIM), per_seq(ts * A_HEADS, A_HEAD_DIM),
                   per_seq(ts, IDX_DIM)],
        out_shape=[jax.ShapeDtypeStruct(buf.shape, F32), jax.ShapeDtypeStruct(out_k.shape, F32),
                   jax.ShapeDtypeStruct(out_v.shape, F32), jax.ShapeDtypeStruct(out_ki.shape, F32)],
        input_output_aliases={9: 0, 10: 1, 11: 2, 12: 3},
        scratch_shapes=_dsa_scratch(s_keys, ts) + [pltpu.VMEM((s_keys, A_WIDTH), F32)],
        compiler_params=_cparams("parallel", "arbitrary"),
        name="dsa_sample",
    )(proj, proj, proj, cache_k, cache_v, cache_ki, proj, proj, proj, buf, out_k, out_v, out_ki)


def _band_bias_kernel(tab_ref, o_ref, *, qs, w):
    layer = pl.program_id(0)
    q = lax.broadcasted_iota(I32, (qs, w), 0)
    jx = lax.broadcasted_iota(I32, (qs, w), 1)
    idx = jnp.clip(B_WIN + q - jx, -REL_CLIP, REL_CLIP) + REL_CLIP
    back = (q >> 6) + B_PREV_CHUNKS - (jx >> 6)
    allowed = (back >= 0) & (back <= B_PREV_CHUNKS)
    for h in range(B_HEADS):
        row = layer * B_HEADS + h

        def body(v, acc):
            return jnp.where(idx == v, tab_ref[row, v], acc)

        acc = lax.fori_loop(0, 2 * REL_CLIP + 1, body, jnp.zeros((qs, w), F32))
        o_ref[0, h] = jnp.where(allowed, acc, NEG_INF)


def _band_bias(b_rel, qs):
    depth = b_rel.shape[0]
    w = qs + B_WIN
    return pl.pallas_call(
        functools.partial(_band_bias_kernel, qs=qs, w=w),
        grid=(depth,),
        in_specs=[pl.BlockSpec(memory_space=pltpu.SMEM)],
        out_specs=pl.BlockSpec((1, B_HEADS, qs, w), lambda l: (l, 0, 0, 0)),
        out_shape=jax.ShapeDtypeStruct((depth, B_HEADS, qs, w), F32),
        compiler_params=_cparams("arbitrary"),
        name=f"band_bias_{qs}",
    )(b_rel.reshape(depth * B_HEADS, 2 * REL_CLIP + 1))


def _band_attend(q_ref, bias_ref, o_ref, kp_scr, vp_scr, start, valid, qs):
    w = qs + B_WIN
    for h in range(B_HEADS):
        sl = slice(h * B_HEAD_DIM, (h + 1) * B_HEAD_DIM)
        qh = q_ref[:, sl].astype(BF16)
        lg = _nt_dot(qh, kp_scr[pl.ds(start, w), sl]) * (B_HEAD_DIM ** -0.5) + bias_ref[0, h]
        if valid is not None:
            lg = jnp.where(valid, lg, NEG_INF)
        m = jnp.max(lg, axis=-1, keepdims=True)
        e = jnp.exp(lg - m)
        den = jnp.sum(e, axis=-1, keepdims=True)
        o = jnp.dot(e.astype(BF16), vp_scr[pl.ds(start, w), sl], preferred_element_type=F32)
        o_ref[:, sl] = o / den


def _band_prompt_kernel(q_ref, k_ref, v_ref, bias_ref, bufk_ref, bufv_ref, o_ref, ck_ref, cv_ref,
                        kp_scr, vp_scr, *, s_keys, qs, nkeep):
    del bufk_ref, bufv_ref
    j = pl.program_id(1)

    @pl.when(j == 0)
    def _():
        kp_scr[0:B_WIN, :] = jnp.zeros((B_WIN, B_WIDTH), BF16)
        vp_scr[0:B_WIN, :] = jnp.zeros((B_WIN, B_WIDTH), BF16)
        kp_scr[B_WIN:B_WIN + s_keys, :] = k_ref[...].astype(BF16)
        vp_scr[B_WIN:B_WIN + s_keys, :] = v_ref[...].astype(BF16)
        for h in range(B_HEADS):
            sl = slice(h * B_HEAD_DIM, (h + 1) * B_HEAD_DIM)
            _head_rows(ck_ref, h, nkeep, B_HEADS)[...] = k_ref[s_keys - nkeep:, sl]
            _head_rows(cv_ref, h, nkeep, B_HEADS)[...] = v_ref[s_keys - nkeep:, sl]

    start = pl.multiple_of(j * qs, qs)
    valid = lax.broadcasted_iota(I32, (qs, qs + B_WIN), 1) >= B_WIN - j * qs
    _band_attend(q_ref, bias_ref, o_ref, kp_scr, vp_scr, start, valid, qs)


def _band_sample_kernel(q_ref, kc_ref, vc_ref, kn_ref, vn_ref, bias_ref, bufo_ref, bufk_ref, bufv_ref,
                        o_ref, ck_ref, cv_ref, kp_scr, vp_scr, *, qs):
    del bufo_ref, bufk_ref, bufv_ref
    for h in range(B_HEADS):
        sl = slice(h * B_HEAD_DIM, (h + 1) * B_HEAD_DIM)
        kp_scr[0:B_WIN, sl] = _head_rows(kc_ref, h, B_WIN, B_HEADS)[...].astype(BF16)
        vp_scr[0:B_WIN, sl] = _head_rows(vc_ref, h, B_WIN, B_HEADS)[...].astype(BF16)
        _head_rows(ck_ref, h, qs, B_HEADS)[...] = kn_ref[:, sl]
        _head_rows(cv_ref, h, qs, B_HEADS)[...] = vn_ref[:, sl]
    kp_scr[B_WIN:B_WIN + qs, :] = kn_ref[...].astype(BF16)
    vp_scr[B_WIN:B_WIN + qs, :] = vn_ref[...].astype(BF16)
    _band_attend(q_ref, bias_ref, o_ref, kp_scr, vp_scr, 0, None, qs)


def _band_prompt(proj, bias, out_k, out_v, layer, *, nb, tq, qs):
    nsteps = tq // qs
    w = qs + B_WIN
    nkeep = min(B_WIN, tq)
    kcol = lambda off: pl.BlockSpec((tq, B_WIDTH), lambda b, j: (b, off // B_WIDTH))
    cache = pl.BlockSpec((nkeep * B_HEADS, B_HEAD_DIM), lambda b, j: (layer * nb + b, 0))
    return pl.pallas_call(
        functools.partial(_band_prompt_kernel, s_keys=tq, qs=qs, nkeep=nkeep),
        grid=(nb, nsteps),
        in_specs=[
            pl.BlockSpec((qs, B_WIDTH), lambda b, j: (b * nsteps + j, COL_BQ // B_WIDTH)),
            kcol(COL_BK), kcol(COL_BV),
            pl.BlockSpec((1, B_HEADS, qs, w), lambda b, j: (layer, 0, 0, 0)),
            ANY, ANY,
        ],
        out_specs=[pl.BlockSpec((qs, B_WIDTH), lambda b, j: (b * nsteps + j, 0)), cache, cache],
        out_shape=[jax.ShapeDtypeStruct((proj.shape[0], B_WIDTH), F32),
                   jax.ShapeDtypeStruct(out_k.shape, F32), jax.ShapeDtypeStruct(out_v.shape, F32)],
        input_output_aliases={4: 1, 5: 2},
        scratch_shapes=[pltpu.VMEM((B_WIN + tq, B_WIDTH), BF16),
                        pltpu.VMEM((B_WIN + tq, B_WIDTH), BF16)],
        compiler_params=_cparams("parallel", "arbitrary"),
        name="band_prompt",
    )(proj, proj, proj, bias, out_k, out_v)


def _band_sample(proj, cache_k, cache_v, bias, buf, out_k, out_v, layer, *, nb, ts, row0):
    rb0 = row0 // ts
    w = ts + B_WIN
    qcol = lambda off: pl.BlockSpec((ts, B_WIDTH), lambda b, j: (rb0 + b, off // B_WIDTH))
    per_seq = lambda rows: pl.BlockSpec((rows * B_HEADS, B_HEAD_DIM), lambda b, j: (layer * nb + b, 0))
    return pl.pallas_call(
        functools.partial(_band_sample_kernel, qs=ts),
        grid=(nb, 1),
        in_specs=[qcol(COL_BQ), per_seq(B_WIN), per_seq(B_WIN), qcol(COL_BK), qcol(COL_BV),
                  pl.BlockSpec((1, B_HEADS, ts, w), lambda b, j: (layer, 0, 0, 0)),
                  ANY, ANY, ANY],
        out_specs=[pl.BlockSpec((ts, B_WIDTH), lambda b, j: (rb0 + b, 0)), per_seq(ts), per_seq(ts)],
        out_shape=[jax.ShapeDtypeStruct(buf.shape, F32), jax.ShapeDtypeStruct(out_k.shape, F32),
                   jax.ShapeDtypeStruct(out_v.shape, F32)],
        input_output_aliases={6: 0, 7: 1, 8: 2},
        scratch_shapes=[pltpu.VMEM((w, B_WIDTH), BF16), pltpu.VMEM((w, B_WIDTH), BF16)],
        compiler_params=_cparams("parallel", "arbitrary"),
        name="band_sample",
    )(proj, cache_k, cache_v, proj, proj, bias, buf, out_k, out_v)


XP_OFF = 8
_HEAD_SPREAD = np.zeros((LANES, C_WIDTH), np.float32)
for _h in range(C_HEADS):
    _HEAD_SPREAD[MISC_DT + _h, _h * C_HEAD_DIM:(_h + 1) * C_HEAD_DIM] = 1.0


def _ssd_kernel(*refs, zero_init, nchunks):
    if zero_init:
        (z_ref, xbc_ref, misc_ref, cw_ref, cbias_ref, prm_ref, gn_ref, ex_ref, dskip_ref,
         bufh_ref, bufc_ref, o_ref, hout_ref, cnew_ref, st_scr, xp_scr, y_scr) = refs
    else:
        (z_ref, xbc_ref, misc_ref, h0_ref, cb0_ref, cw_ref, cbias_ref, prm_ref, gn_ref, ex_ref, dskip_ref,
         bufo_ref, bufh_ref, bufc_ref, o_ref, hout_ref, cnew_ref, st_scr, xp_scr, y_scr) = refs
        del bufo_ref
    del bufh_ref, bufc_ref
    c = pl.program_id(1)
    tail = CONV_W - 1
    rep = C_HEADS // C_GROUPS
    gw = C_WIDTH // C_GROUPS

    @pl.when(c == 0)
    def _():
        if zero_init:
            st_scr[...] = jnp.zeros(st_scr.shape, F32)
            xp_scr[XP_OFF - tail:XP_OFF, :] = jnp.zeros((tail, CONV_DIM), F32)
        else:
            for h in range(C_HEADS):
                st_scr[h // rep, :, (h % rep) * C_HEAD_DIM:(h % rep + 1) * C_HEAD_DIM] = h0_ref[0, h].T
            xp_scr[XP_OFF - tail:XP_OFF, :] = cb0_ref[0]

    xp_scr[XP_OFF:XP_OFF + CHUNK, :] = xbc_ref[...]
    y = cbias_ref[...] + xp_scr[XP_OFF - tail:XP_OFF - tail + CHUNK, :] * cw_ref[0:1, :]
    for t in range(1, CONV_W):
        y = y + xp_scr[XP_OFF - tail + t:XP_OFF - tail + t + CHUNK, :] * cw_ref[t:t + 1, :]
    new_tail = xp_scr[XP_OFF + CHUNK - tail:XP_OFF + CHUNK, :]
    xp_scr[XP_OFF - tail:XP_OFF, :] = new_tail
    cnew_ref[0] = new_tail
    xc = y * (1.0 / (1.0 + jnp.exp(-y)))
    xs = xc[:, 0:C_WIDTH]

    pre = misc_ref[...] + prm_ref[0:1, :]
    dt = jnp.maximum(pre, 0.0) + jnp.log(1.0 + jnp.exp(-jnp.abs(pre)))
    ad = dt * (-jnp.exp(prm_ref[1:2, :]))
    li = lax.broadcasted_iota(I32, (CHUNK, CHUNK), 0)
    si = lax.broadcasted_iota(I32, (CHUNK, CHUNK), 1)
    causal = li >= si
    tri = causal.astype(BF16)
    acs = sum(jnp.dot(tri, part, preferred_element_type=F32) for part in _split3(ad))
    acs_t = acs.T
    ex = ex_ref[...]
    spread = lambda v: sum(jnp.dot(part, ex, preferred_element_type=F32) for part in _split3(v))
    dt_x = spread(dt)
    acs_x = spread(acs)
    acs_last_x = acs_x[CHUNK - 1:CHUNK, :]
    xd_all = xs * dt_x
    xd_b = xd_all.astype(BF16)
    xde_b = (xd_all * jnp.exp(acs_last_x - acs_x)).astype(BF16)
    grow_x = jnp.exp(acs_x)
    chunk_decay_x = jnp.exp(acs_last_x)
    head_of_lane = lax.broadcasted_iota(I32, (CHUNK, gw), 1) >> 6

    for g in range(C_GROUPS):
        gl = slice(g * gw, (g + 1) * gw)
        b_g = xc[:, C_WIDTH + g * C_STATE:C_WIDTH + (g + 1) * C_STATE]
        c_g = xc[:, C_WIDTH + (C_GROUPS + g) * C_STATE:C_WIDTH + (C_GROUPS + g + 1) * C_STATE]
        b_gt = b_g.T.astype(BF16)
        c_gb = c_g.astype(BF16)
        cb = jnp.dot(c_gb, b_gt, preferred_element_type=F32)
        st = st_scr[g]
        y_g = jnp.dot(c_gb, st.astype(BF16), preferred_element_type=F32) * grow_x[:, gl]
        xd_g = xd_b[:, gl]
        for r in range(rep):
            col = MISC_DT + g * rep + r
            seg = acs[:, col:col + 1] - acs_t[col:col + 1, :]
            decay_in = jnp.where(causal, jnp.exp(jnp.where(causal, seg, 0.0)), 0.0)
            t_r = jnp.dot((cb * decay_in).astype(BF16), xd_g, preferred_element_type=F32)
            y_g = y_g + jnp.where(head_of_lane == r, t_r, 0.0)
        y_scr[:, gl] = y_g + dskip_ref[:, gl] * xs[:, gl]
        st_scr[g] = st * chunk_decay_x[:, gl] + jnp.dot(b_gt, xde_b[:, gl], preferred_element_type=F32)

    @pl.when(c == nchunks - 1)
    def _():
        for h in range(C_HEADS):
            hout_ref[0, h] = st_scr[h // rep, :, (h % rep) * C_HEAD_DIM:(h % rep + 1) * C_HEAD_DIM].T

    z = z_ref[...]
    gate = y_scr[...] * (z * (1.0 / (1.0 + jnp.exp(-z))))
    for g in range(C_GROUPS):
        gg = gate[:, g * gw:(g + 1) * gw]
        ms = jnp.mean(gg * gg, axis=-1, keepdims=True)
        o_ref[:, g * gw:(g + 1) * gw] = gg * lax.rsqrt(ms + EPS) * gn_ref[:, g * gw:(g + 1) * gw]


def _ssd(proj, conv_w, conv_b, prm, gate_norm, out_h, out_c, layer, *, nb, nchunks, row0, state=None):
    rb0 = row0 // CHUNK
    row = lambda b, c: rb0 + b * nchunks + c
    const = lambda shape: pl.BlockSpec(shape, lambda b, c: tuple(0 for _ in shape))
    per_seq_h = pl.BlockSpec((1, C_HEADS, C_HEAD_DIM, C_STATE), lambda b, c: (layer * nb + b, 0, 0, 0))
    per_seq_c = pl.BlockSpec((1, CONV_W - 1, CONV_DIM), lambda b, c: (layer * nb + b, 0, 0))
    in_specs = [
        pl.BlockSpec((CHUNK, C_WIDTH), lambda b, c: (row(b, c), COL_CZ // C_WIDTH)),
        pl.BlockSpec((CHUNK, CONV_DIM), lambda b, c: (row(b, c), COL_CXBC // CONV_DIM)),
        pl.BlockSpec((CHUNK, LANES), lambda b, c: (row(b, c), COL_MISC // LANES)),
    ]
    args = [proj, proj, proj]
    if state is not None:
        h0, cbuf0, buf = state
        in_specs += [per_seq_h, per_seq_c]
        args += [h0, cbuf0]
    in_specs += [const((CONV_W, CONV_DIM)), const((1, CONV_DIM)), const((8, LANES)), const((1, C_WIDTH)),
                 const((LANES, C_WIDTH)), const((1, C_WIDTH))]
    args += [conv_w, conv_b.reshape(1, CONV_DIM), prm[0], gate_norm.reshape(1, C_WIDTH),
             jnp.asarray(_HEAD_SPREAD, BF16), prm[1]]
    aliases = {}
    if state is not None:
        in_specs.append(ANY)
        args.append(buf)
        aliases[len(args) - 1] = 0
    in_specs += [ANY, ANY]
    args += [out_h, out_c]
    aliases[len(args) - 2] = 1
    aliases[len(args) - 1] = 2
    return pl.pallas_call(
        functools.partial(_ssd_kernel, zero_init=state is None, nchunks=nchunks),
        grid=(nb, nchunks),
        in_specs=in_specs,
        out_specs=[pl.BlockSpec((CHUNK, C_WIDTH), lambda b, c: (row(b, c), 0)), per_seq_h, per_seq_c],
        out_shape=[
            jax.ShapeDtypeStruct((proj.shape[0], C_WIDTH), F32),
            jax.ShapeDtypeStruct(out_h.shape, F32),
            jax.ShapeDtypeStruct(out_c.shape, F32),
        ],
        input_output_aliases=aliases,
        scratch_shapes=[pltpu.VMEM((C_GROUPS, C_STATE, C_WIDTH // C_GROUPS), F32),
                        pltpu.VMEM((XP_OFF + CHUNK, CONV_DIM), F32),
                        pltpu.VMEM((CHUNK, C_WIDTH), F32)],
        compiler_params=_cparams("parallel", "arbitrary"),
        name="ssd_prompt" if state is None else "ssd_sample",
    )(*args)


def _outproj_kernel(x_ref, oa_ref, ob_ref, oc_ref, w_ref, o_ref):
    acc = x_ref[...]
    acc = acc + jnp.dot(oa_ref[...].astype(BF16), w_ref[0:A_WIDTH, :], preferred_element_type=F32)
    acc = acc + jnp.dot(ob_ref[...].astype(BF16), w_ref[A_WIDTH:A_WIDTH + B_WIDTH, :],
                        preferred_element_type=F32)
    acc = acc + jnp.dot(oc_ref[...].astype(BF16), w_ref[A_WIDTH + B_WIDTH:, :],
                        preferred_element_type=F32)
    o_ref[...] = acc


def _out_proj(x, oa, ob, oc, w_bf16):
    t, d = x.shape
    tm = _row_tile(t, 512)
    rows = lambda w: pl.BlockSpec((tm, w), lambda i: (i, 0))
    return pl.pallas_call(
        _outproj_kernel,
        grid=(t // tm,),
        in_specs=[rows(d), rows(A_WIDTH), rows(B_WIDTH), rows(C_WIDTH),
                  pl.BlockSpec((d, d), lambda i: (0, 0))],
        out_specs=rows(d),
        out_shape=jax.ShapeDtypeStruct((t, d), F32),
        compiler_params=_cparams("parallel"),
        name="out_proj",
    )(x, oa, ob, oc, w_bf16)


def _ffn_kernel(x_ref, nw_ref, wu_ref, wd_ref, o_ref, h_scr):
    @pl.when(pl.program_id(1) == 0)
    def _():
        x = x_ref[...]
        ms = jnp.mean(x * x, axis=-1, keepdims=True)
        h_scr[...] = (x * lax.rsqrt(ms + EPS) * nw_ref[...]).astype(BF16)
        o_ref[...] = x

    u = jnp.maximum(jnp.dot(h_scr[...], wu_ref[...], preferred_element_type=F32), 0.0)
    o_ref[...] += jnp.dot((u * u).astype(BF16), wd_ref[...], preferred_element_type=F32)


def _ffn(x, norm_w, wu_bf16, wd_bf16):
    t, d = x.shape
    ff = wu_bf16.shape[1]
    tm = _row_tile(t, 512)
    tf = 512
    return pl.pallas_call(
        _ffn_kernel,
        grid=(t // tm, ff // tf),
        in_specs=[
            pl.BlockSpec((tm, d), lambda i, f: (i, 0)),
            pl.BlockSpec((1, d), lambda i, f: (0, 0)),
            pl.BlockSpec((d, tf), lambda i, f: (0, f)),
            pl.BlockSpec((tf, d), lambda i, f: (f, 0)),
        ],
        out_specs=pl.BlockSpec((tm, d), lambda i, f: (i, 0)),
        out_shape=jax.ShapeDtypeStruct((t, d), F32),
        scratch_shapes=[pltpu.VMEM((tm, d), BF16)],
        compiler_params=_cparams("parallel", "arbitrary"),
        name="ffn",
    )(x, norm_w.reshape(1, d), wu_bf16, wd_bf16)


def _norm_kernel(x_ref, nw_ref, o_ref):
    x = x_ref[...]
    ms = jnp.mean(x * x, axis=-1, keepdims=True)
    o_ref[...] = x * lax.rsqrt(ms + EPS) * nw_ref[...]


def _final_norm(x, norm_w):
    t, d = x.shape
    tm = _row_tile(t, 512)
    return pl.pallas_call(
        _norm_kernel,
        grid=(t // tm,),
        in_specs=[pl.BlockSpec((tm, d), lambda i: (i, 0)), pl.BlockSpec((1, d), lambda i: (0, 0))],
        out_specs=pl.BlockSpec((tm, d), lambda i: (i, 0)),
        out_shape=jax.ShapeDtypeStruct((t, d), F32),
        compiler_params=_cparams("parallel"),
        name="final_norm",
    )(x, norm_w.reshape(1, d))


def _lane_row(v, off):
    return jnp.zeros((LANES,), F32).at[off:off + v.shape[0]].set(v)


def _trunk(x_prompt, x_sample, cache_a_k, cache_a_v, cache_a_kidx, cache_b_k, cache_b_v,
           state_ssm, state_conv, norm1, w_in, w_out, b_rel, conv_w, conv_b, dt_bias,
           a_log, d_skip, gate_norm, norm2, w_up, w_down, final_norm):
    bp, tp, d = x_prompt.shape
    bs, ts, _ = x_sample.shape
    depth = w_in.shape[0]
    past = cache_a_k.shape[2]
    nbuf = cache_b_k.shape[2]
    n_p = bp * tp
    n_s = bs * ts
    assert ts == CHUNK and tp % DSA_KB_PROMPT == 0 and nbuf == B_WIN
    qs_p = LANES
    nkeep = min(B_WIN, tp)

    x = jnp.concatenate([x_prompt.reshape(n_p, d), x_sample.reshape(n_s, d)], axis=0)

    pos = jnp.concatenate([jnp.tile(jnp.arange(tp), bp), jnp.tile(past + jnp.arange(ts), bs)])
    tab_a = _rope_tables(pos, A_HEAD_DIM)
    tab_i = _rope_tables(pos, IDX_DIM)
    bias_p = _band_bias(b_rel, qs_p)
    bias_s = _band_bias(b_rel, ts)
    w_in_r = _regroup_w_in(w_in)
    cak = cache_a_k.reshape(depth * bs * past * A_HEADS, A_HEAD_DIM)
    cav = cache_a_v.reshape(depth * bs * past * A_HEADS, A_HEAD_DIM)
    caki = cache_a_kidx.reshape(depth * bs * past, IDX_DIM)
    cbk = cache_b_k.reshape(depth * bs * nbuf * B_HEADS, B_HEAD_DIM)
    cbv = cache_b_v.reshape(depth * bs * nbuf * B_HEADS, B_HEAD_DIM)
    h0 = state_ssm.reshape(depth * bs, C_HEADS, C_HEAD_DIM, C_STATE)
    cbuf0 = state_conv.reshape(depth * bs, CONV_W - 1, CONV_DIM)

    p_ak = _uninit((depth * bp * tp * A_HEADS, A_HEAD_DIM))
    p_av = _uninit((depth * bp * tp * A_HEADS, A_HEAD_DIM))
    p_aki = _uninit((depth * bp * tp, IDX_DIM))
    p_bk = _uninit((depth * bp * nkeep * B_HEADS, B_HEAD_DIM))
    p_bv = _uninit((depth * bp * nkeep * B_HEADS, B_HEAD_DIM))
    p_h = _uninit((depth * bp, C_HEADS, C_HEAD_DIM, C_STATE))
    p_c = _uninit((depth * bp, CONV_W - 1, CONV_DIM))
    s_ak = _uninit((depth * bs * ts * A_HEADS, A_HEAD_DIM))
    s_av = _uninit((depth * bs * ts * A_HEADS, A_HEAD_DIM))
    s_aki = _uninit((depth * bs * ts, IDX_DIM))
    s_bk = _uninit((depth * bs * ts * B_HEADS, B_HEAD_DIM))
    s_bv = _uninit((depth * bs * ts * B_HEADS, B_HEAD_DIM))
    s_h = _uninit((depth * bs, C_HEADS, C_HEAD_DIM, C_STATE))
    s_c = _uninit((depth * bs, CONV_W - 1, CONV_DIM))

    for l in range(depth):
        proj = _in_proj(x, norm1[l], w_in_r, l, tab_a, tab_i)

        oa, p_ak, p_av, p_aki = _dsa_prompt(proj, p_ak, p_av, p_aki, l, nb=bp, tq=tp, qs=qs_p,
                                            kb=DSA_KB_PROMPT)
        oa, s_ak, s_av, s_aki = _dsa_sample(proj, cak, cav, caki, oa, s_ak, s_av, s_aki, l,
                                            nb=bs, ts=ts, past=past, row0=n_p, kb=DSA_KB_SAMPLE)
        ob, p_bk, p_bv = _band_prompt(proj, bias_p, p_bk, p_bv, l, nb=bp, tq=tp, qs=qs_p)
        ob, s_bk, s_bv = _band_sample(proj, cbk, cbv, bias_s, ob, s_bk, s_bv, l, nb=bs, ts=ts, row0=n_p)
        rows = jnp.zeros((8, LANES), F32)
        rows = rows.at[0].set(_lane_row(dt_bias[l], MISC_DT)).at[1].set(_lane_row(a_log[l], MISC_DT))
        prm = (rows, jnp.repeat(d_skip[l], C_HEAD_DIM).reshape(1, C_WIDTH))
        oc, p_h, p_c = _ssd(proj, conv_w[l], conv_b[l], prm, gate_norm[l], p_h, p_c, l,
                            nb=bp, nchunks=tp // CHUNK, row0=0)
        oc, s_h, s_c = _ssd(proj, conv_w[l], conv_b[l], prm, gate_norm[l], s_h, s_c, l,
                            nb=bs, nchunks=1, row0=n_p, state=(h0, cbuf0, oc))

        x = _out_proj(x, oa, ob, oc, w_out[l].astype(BF16))
        x = _ffn(x, norm2[l], w_up[l].astype(BF16), w_down[l].astype(BF16))

    y = _final_norm(x, final_norm)
    y_prompt = y[:n_p].reshape(bp, tp, d)
    y_sample = y[n_p:].reshape(bs, ts, d)
    return (y_prompt, y_sample,
            p_ak.reshape(depth, bp, tp, A_HEADS, A_HEAD_DIM),
            p_av.reshape(depth, bp, tp, A_HEADS, A_HEAD_DIM),
            p_aki.reshape(depth, bp, tp, IDX_DIM),
            p_bk.reshape(depth, bp, nkeep, B_HEADS, B_HEAD_DIM),
            p_bv.reshape(depth, bp, nkeep, B_HEADS, B_HEAD_DIM),
            p_h.reshape(depth, bp, C_HEADS, C_HEAD_DIM, C_STATE),
            p_c.reshape(depth, bp, CONV_W - 1, CONV_DIM),
            s_ak.reshape(depth, bs, ts, A_HEADS, A_HEAD_DIM),
            s_av.reshape(depth, bs, ts, A_HEADS, A_HEAD_DIM),
            s_aki.reshape(depth, bs, ts, IDX_DIM),
            s_bk.reshape(depth, bs, ts, B_HEADS, B_HEAD_DIM),
            s_bv.reshape(depth, bs, ts, B_HEADS, B_HEAD_DIM),
            s_h.reshape(depth, bs, C_HEADS, C_HEAD_DIM, C_STATE),
            s_c.reshape(depth, bs, CONV_W - 1, CONV_DIM))


def kernel(x_prompt, x_sample, cache_a_k, cache_a_v, cache_a_kidx, cache_b_k, cache_b_v, state_ssm, state_conv, norm1, w_in, w_out, b_rel, conv_w, conv_b, dt_bias, a_log, d_skip, gate_norm, norm2, w_up, w_down, final_norm):
    return _trunk(x_prompt, x_sample, cache_a_k, cache_a_v, cache_a_kidx, cache_b_k, cache_b_v,
                  state_ssm, state_conv, norm1, w_in, w_out, b_rel, conv_w, conv_b, dt_bias,
                  a_log, d_skip, gate_norm, norm2, w_up, w_down, final_norm)
```

```python
import functools

import jax
import jax.numpy as jnp
import numpy as np
from jax import lax
from jax.experimental import pallas as pl
from jax.experimental.pallas import tpu as pltpu

F32 = jnp.float32
BF16 = jnp.bfloat16
I32 = jnp.int32
I16 = jnp.int16

D_MODEL = 2048
CHUNK = 64
A_HEADS = 4
A_HEAD_DIM = 128
A_WIDTH = A_HEADS * A_HEAD_DIM
IDX_HEADS = 16
IDX_DIM = 64
IQ_WIDTH = IDX_HEADS * IDX_DIM
DSA_TOPK = 256
B_HEADS = 4
B_HEAD_DIM = 128
B_WIDTH = B_HEADS * B_HEAD_DIM
B_PREV_CHUNKS = 8
B_WIN = B_PREV_CHUNKS * CHUNK
REL_CLIP = 128
C_WIDTH = 1024
C_HEAD_DIM = 64
C_HEADS = C_WIDTH // C_HEAD_DIM
C_GROUPS = 4
C_STATE = 128
CONV_W = 4
CONV_DIM = C_WIDTH + 2 * C_GROUPS * C_STATE
D_FF = 4 * D_MODEL
ROPE_THETA = 500000.0
EPS = 1e-5

LANES = 128
INT_MIN = -(2 ** 31)
NEG_INF = float("-inf")

COL_AQ = 0
COL_AK = 512
COL_IQ = 1024
COL_AV = 2048
COL_BQ = 2560
COL_BK = 3072
COL_BV = 3584
COL_CXBC = 4096
COL_CZ = 6144
COL_MISC = 7168
MISC_IW = 64
MISC_DT = 80
IN_COLS_PAD = 7680
IN_TN = 512
SRC_AQ, SRC_AV, SRC_IQ, SRC_IK, SRC_BQ, SRC_CZ, SRC_CXBC, SRC_DT, SRC_END = (
    0, 1024, 1536, 2560, 2640, 4176, 5200, 7248, 7264)

VMEM_LIMIT = 56 * 1024 * 1024
ANY = pl.BlockSpec(memory_space=pl.ANY)


def _cparams(*sem):
    return pltpu.CompilerParams(dimension_semantics=sem, vmem_limit_bytes=VMEM_LIMIT)


def _row_tile(t, cap):
    tm = cap
    while t % tm:
        tm //= 2
    return tm


def _nt_dot(a, b):
    return lax.dot_general(a, b, (((1,), (1,)), ((), ())), preferred_element_type=F32)


def _split3(x):
    hi = x.astype(BF16)
    r1 = x - hi.astype(F32)
    mid = r1.astype(BF16)
    lo = (r1 - mid.astype(F32)).astype(BF16)
    return hi, mid, lo


def _head_rows(ref, h, n, heads):
    return ref.at[pl.ds(h, n, stride=heads), :]


def _uninit_kernel(o_ref):
    del o_ref


def _uninit(shape, dtype=F32):
    return pl.pallas_call(_uninit_kernel, out_specs=ANY, out_shape=jax.ShapeDtypeStruct(shape, dtype),
                          name="alloc")()


def _regroup_kernel(w_ref, o_ref):
    def put(dst, src, width):
        o_ref[0, :, dst:dst + width] = w_ref[0, :, src:src + width].astype(BF16)

    rows = w_ref.shape[1]
    put(COL_AQ, SRC_AQ, 2 * A_WIDTH)
    put(COL_IQ, SRC_IQ, IQ_WIDTH)
    put(COL_AV, SRC_AV, A_WIDTH)
    put(COL_BQ, SRC_BQ, 3 * B_WIDTH)
    put(COL_CXBC, SRC_CXBC, CONV_DIM)
    put(COL_CZ, SRC_CZ, C_WIDTH)
    o_ref[0, :, COL_MISC:] = jnp.zeros((rows, IN_COLS_PAD - COL_MISC), BF16)
    put(COL_MISC, SRC_IK, IDX_DIM + IDX_HEADS)
    put(COL_MISC + MISC_DT, SRC_DT, C_HEADS)


def _regroup_w_in(w_in):
    depth, d, n = w_in.shape
    assert n == SRC_END
    tk = 256
    return pl.pallas_call(
        _regroup_kernel,
        grid=(depth, d // tk),
        in_specs=[pl.BlockSpec((1, tk, n), lambda l, i: (l, i, 0))],
        out_specs=pl.BlockSpec((1, tk, IN_COLS_PAD), lambda l, i: (l, i, 0)),
        out_shape=jax.ShapeDtypeStruct((depth, d, IN_COLS_PAD), BF16),
        compiler_params=_cparams("parallel", "parallel"),
        name="regroup_w_in",
    )(w_in)


def _rope_block(x, c, sm, sp, half):
    return x * c + pltpu.roll(x, LANES - half, 1) * sm + pltpu.roll(x, half, 1) * sp


def _inproj_kernel(x_ref, nw_ref, w_ref, ta_ref, ti_ref, o_ref, h_scr):
    j = pl.program_id(1)

    @pl.when(j == 0)
    def _():
        x = x_ref[...]
        ms = jnp.mean(x * x, axis=-1, keepdims=True)
        h_scr[...] = (x * lax.rsqrt(ms + EPS) * nw_ref[...]).astype(BF16)

    o_ref[...] = jnp.dot(h_scr[...], w_ref[0], preferred_element_type=F32)

    @pl.when(j < COL_IQ // IN_TN)
    def _():
        for h in range(IN_TN // LANES):
            sl = slice(h * LANES, (h + 1) * LANES)
            o_ref[:, sl] = _rope_block(o_ref[:, sl], ta_ref[0], ta_ref[1], ta_ref[2], A_HEAD_DIM // 8)

    @pl.when((j >= COL_IQ // IN_TN) & (j < COL_AV // IN_TN))
    def _():
        for h in range(IN_TN // LANES):
            sl = slice(h * LANES, (h + 1) * LANES)
            o_ref[:, sl] = _rope_block(o_ref[:, sl], ti_ref[0], ti_ref[1], ti_ref[2], IDX_DIM // 8)

    @pl.when(j == COL_MISC // IN_TN)
    def _():
        m = o_ref[:, 0:LANES]
        lane = lax.broadcasted_iota(I32, m.shape, 1)
        rot = _rope_block(m, ti_ref[0], ti_ref[1], ti_ref[2], IDX_DIM // 8)
        o_ref[:, 0:LANES] = jnp.where(lane < IDX_DIM, rot, m)


def _in_proj(x, norm_w, w_all, layer, tab_a, tab_i):
    t, d = x.shape
    n = w_all.shape[2]
    tm = _row_tile(t, 1024)
    tab = pl.BlockSpec((3, tm, LANES), lambda i, j: (0, i, 0))
    return pl.pallas_call(
        _inproj_kernel,
        grid=(t // tm, n // IN_TN),
        in_specs=[
            pl.BlockSpec((tm, d), lambda i, j: (i, 0)),
            pl.BlockSpec((1, d), lambda i, j: (0, 0)),
            pl.BlockSpec((1, d, IN_TN), lambda i, j: (layer, 0, j)),
            tab, tab,
        ],
        out_specs=pl.BlockSpec((tm, IN_TN), lambda i, j: (i, j)),
        out_shape=jax.ShapeDtypeStruct((t, n), F32),
        scratch_shapes=[pltpu.VMEM((tm, d), BF16)],
        compiler_params=_cparams("parallel", "arbitrary"),
        name="in_proj",
    )(x, norm_w.reshape(1, d), w_all, tab_a, tab_i)


def _rope_tables(pos, head_dim):
    rot = head_dim // 4
    half = rot // 2
    inv = ROPE_THETA ** (-jnp.arange(half, dtype=F32) * 2.0 / rot)
    ang = pos.astype(F32)[:, None] * inv[None, :]
    cos, sin = jnp.cos(ang), jnp.sin(ang)
    n = pos.shape[0]
    rest = head_dim - rot
    c = jnp.concatenate([cos, cos, jnp.ones((n, rest), F32)], axis=1)
    sm = jnp.concatenate([-sin, jnp.zeros((n, half + rest), F32)], axis=1)
    sp = jnp.concatenate([jnp.zeros((n, half), F32), sin, jnp.zeros((n, rest), F32)], axis=1)
    tab = jnp.stack([c, sm, sp], axis=0)
    return jnp.tile(tab, (1, 1, LANES // head_dim))


TIE_BLOCK = 64
SUB = 128
DSA_KB_PROMPT = 512
DSA_KB_SAMPLE = 384


def _tree_sum(parts):
    while len(parts) > 1:
        parts = [parts[i] + parts[i + 1] for i in range(0, len(parts) - 1, 2)] + (
            parts[-1:] if len(parts) % 2 else [])
    return parts[0]


def _count_keys(key_scr, n_keys, qs, pred):
    dt = key_scr.dtype
    pack = 8 * (4 // dt.itemsize)
    blocks = []
    for u in range(n_keys // SUB):
        hit = jnp.where(pred(key_scr[u * SUB:(u + 1) * SUB, :]), jnp.ones((), dt), jnp.zeros((), dt))
        blocks.append(_tree_sum([hit[i * pack:(i + 1) * pack, :] for i in range(SUB // pack)]))
    return jnp.sum(_tree_sum(blocks).astype(I32), axis=0, keepdims=True)


def _kth_largest(scr, n_keys, qs, k, nbits):
    dt = scr.dtype
    count = functools.partial(_count_keys, scr, n_keys, qs)
    lowest = -(2 ** (nbits - 1))
    ans0 = jnp.where(count(lambda x: x >= jnp.zeros((1, qs), dt)) >= k, 0, lowest).astype(I32)

    def bit_step(i, ans):
        cand = ans | (jnp.int32(1) << (nbits - 2 - i))
        return jnp.where(count(lambda x: x >= cand.astype(dt)) >= k, cand, ans)

    return lax.fori_loop(0, nbits - 1, bit_step, ans0)


def _dsa_select_attend(q_ref, qi_ref, mq_ref, o_ref, kb_scr, vt_scr, kib_scr, w_scr, score_scr,
                       key_scr, mask_scr, hi_scr, lo_scr, *, n_keys, qs, topk, limit):
    w_scr[...] = mq_ref[...].T * (IDX_HEADS ** -0.5 * IDX_DIM ** -0.5)

    kib = kib_scr[0:n_keys, :]
    for h in range(IDX_HEADS):
        qh = qi_ref[:, h * IDX_DIM:(h + 1) * IDX_DIM].astype(BF16)
        term = jnp.maximum(_nt_dot(kib, qh), 0.0) * w_scr[MISC_IW + h:MISC_IW + h + 1, :]
        if h == 0:
            score_scr[0:n_keys, :] = term
        else:
            score_scr[0:n_keys, :] += term
    sc = score_scr[0:n_keys, :]
    sc = jnp.where(sc == 0.0, 0.0, sc)
    bits = lax.bitcast_convert_type(sc, I32)
    key = bits ^ ((bits >> 31) & 0x7FFFFFFF)
    s_idx = lax.broadcasted_iota(I32, (n_keys, qs), 0)
    key = jnp.where(s_idx < limit, key, INT_MIN)
    key_scr[0:n_keys, :] = key
    hi_scr[0:n_keys, :] = (key >> 16).astype(I16)

    thr_hi = _kth_largest(hi_scr, n_keys, qs, topk, 16)
    above = _count_keys(hi_scr, n_keys, qs, lambda x: x > thr_hi.astype(I16))
    key = key_scr[0:n_keys, :]
    lo = ((key & 0xFFFF) - 2 ** 15).astype(I16)
    lo_scr[0:n_keys, :] = jnp.where((key >> 16) == thr_hi, lo, jnp.int16(-(2 ** 15)))
    thr_lo = _kth_largest(lo_scr, n_keys, qs, topk - above, 16)
    thr = (thr_hi << 16) | (thr_lo + 2 ** 15)

    cnt_gt = _count_keys(key_scr, n_keys, qs, lambda x: x > thr)
    cnt_eq = _count_keys(key_scr, n_keys, qs, lambda x: x == thr)
    room = topk - cnt_gt
    live = thr > INT_MIN
    all_ties = (cnt_eq <= room) & live
    key = key_scr[0:n_keys, :]
    mask_scr[0:n_keys, :] = jnp.where((key > thr) | ((key == thr) & all_ties), 0.0, NEG_INF)
    need_ties = jnp.max(((cnt_eq > room) & live).astype(I32)) > 0

    @pl.when(need_ties)
    def _():
        tri = (lax.broadcasted_iota(I32, (TIE_BLOCK, TIE_BLOCK), 0)
               >= lax.broadcasted_iota(I32, (TIE_BLOCK, TIE_BLOCK), 1)).astype(BF16)

        def blk(b, carry):
            rows = pl.ds(pl.multiple_of(b * TIE_BLOCK, TIE_BLOCK), TIE_BLOCK)
            kblk = key_scr[rows, :]
            eqb = (kblk == thr) & live
            eqf = eqb.astype(F32)
            prefix = jnp.dot(tri, eqf.astype(BF16), preferred_element_type=F32) + carry
            keep = (kblk > thr) | (eqb & (prefix <= room.astype(F32)))
            mask_scr[rows, :] = jnp.where(keep, 0.0, NEG_INF)
            return carry + jnp.sum(eqf, axis=0, keepdims=True)

        lax.fori_loop(0, n_keys // TIE_BLOCK, blk, jnp.zeros((1, qs), F32))

    for h in range(A_HEADS):
        sl = slice(h * A_HEAD_DIM, (h + 1) * A_HEAD_DIM)
        qh = q_ref[:, sl].astype(BF16)
        lg = _nt_dot(kb_scr[0:n_keys, sl], qh) * (A_HEAD_DIM ** -0.5) + mask_scr[0:n_keys, :]
        m = jnp.max(lg, axis=0, keepdims=True)
        e = jnp.exp(lg - m)
        den = jnp.sum(e, axis=0, keepdims=True)
        o_t = jnp.dot(vt_scr[sl, 0:n_keys], e.astype(BF16), preferred_element_type=F32)
        o_ref[:, sl] = (o_t / den).T


def _dsa_prompt_kernel(q_ref, qi_ref, mq_ref, k_ref, v_ref, mk_ref, bufk_ref, bufv_ref, bufi_ref,
                       o_ref, ck_ref, cv_ref, ci_ref,
                       kb_scr, vt_scr, kib_scr, w_scr, score_scr, key_scr, mask_scr, hi_scr, lo_scr,
                       *, s_keys, qs, kb, topk):
    del bufk_ref, bufv_ref, bufi_ref
    j = pl.program_id(1)

    @pl.when(j == 0)
    def _():
        kb_scr[...] = k_ref[...].astype(BF16)
        kib_scr[...] = mk_ref[:, 0:IDX_DIM].astype(BF16)
        for i in range(s_keys // kb):
            vt_scr[:, i * kb:(i + 1) * kb] = v_ref[i * kb:(i + 1) * kb, :].T.astype(BF16)
        ci_ref[...] = mk_ref[:, 0:IDX_DIM]
        for h in range(A_HEADS):
            sl = slice(h * A_HEAD_DIM, (h + 1) * A_HEAD_DIM)
            _head_rows(ck_ref, h, s_keys, A_HEADS)[...] = k_ref[:, sl]
            _head_rows(cv_ref, h, s_keys, A_HEADS)[...] = v_ref[:, sl]

    q_pos = j * qs + lax.broadcasted_iota(I32, (1, qs), 1)
    limit = ((q_pos >> 6) + 1) * CHUNK
    nsb = (j * qs + qs + kb - 1) // kb
    for n in range(1, s_keys // kb + 1):
        pl.when(nsb == n)(functools.partial(
            _dsa_select_attend, q_ref, qi_ref, mq_ref, o_ref, kb_scr, vt_scr, kib_scr, w_scr, score_scr,
            key_scr, mask_scr, hi_scr, lo_scr, n_keys=n * kb, qs=qs, topk=topk, limit=limit))


def _dsa_sample_kernel(q_ref, qi_ref, mq_ref, kc_ref, vc_ref, kic_ref, kn_ref, vn_ref, mkn_ref,
                       bufo_ref, bufk_ref, bufv_ref, bufi_ref,
                       o_ref, ck_ref, cv_ref, ci_ref,
                       kb_scr, vt_scr, kib_scr, w_scr, score_scr, key_scr, mask_scr, hi_scr, lo_scr,
                       vrow_scr, *, s_keys, qs, kb, topk, past):
    del bufo_ref, bufk_ref, bufv_ref, bufi_ref
    n_new = kn_ref.shape[0]
    live = past + n_new
    for h in range(A_HEADS):
        sl = slice(h * A_HEAD_DIM, (h + 1) * A_HEAD_DIM)
        kb_scr[0:past, sl] = _head_rows(kc_ref, h, past, A_HEADS)[...].astype(BF16)
        vrow_scr[0:past, sl] = _head_rows(vc_ref, h, past, A_HEADS)[...]
        _head_rows(ck_ref, h, n_new, A_HEADS)[...] = kn_ref[:, sl]
        _head_rows(cv_ref, h, n_new, A_HEADS)[...] = vn_ref[:, sl]
    ci_ref[...] = mkn_ref[:, 0:IDX_DIM]
    kb_scr[past:live, :] = kn_ref[...].astype(BF16)
    kib_scr[0:past, :] = kic_ref[...].astype(BF16)
    kib_scr[past:live, :] = mkn_ref[:, 0:IDX_DIM].astype(BF16)
    vrow_scr[past:live, :] = vn_ref[...]
    if s_keys > live:
        kb_scr[live:, :] = jnp.zeros((s_keys - live, A_WIDTH), BF16)
        kib_scr[live:, :] = jnp.zeros((s_keys - live, IDX_DIM), BF16)
        vrow_scr[live:, :] = jnp.zeros((s_keys - live, A_WIDTH), F32)
    for i in range(s_keys // kb):
        vt_scr[:, i * kb:(i + 1) * kb] = vrow_scr[i * kb:(i + 1) * kb, :].T.astype(BF16)

    limit = jnp.full((1, qs), live, I32)
    _dsa_select_attend(q_ref, qi_ref, mq_ref, o_ref, kb_scr, vt_scr, kib_scr, w_scr, score_scr,
                       key_scr, mask_scr, hi_scr, lo_scr, n_keys=s_keys, qs=qs, topk=topk, limit=limit)


def _dsa_scratch(s_keys, qs):
    return [
        pltpu.VMEM((s_keys, A_WIDTH), BF16),
        pltpu.VMEM((A_WIDTH, s_keys), BF16),
        pltpu.VMEM((s_keys, IDX_DIM), BF16),
        pltpu.VMEM((LANES, qs), F32),
        pltpu.VMEM((s_keys, qs), F32),
        pltpu.VMEM((s_keys, qs), I32),
        pltpu.VMEM((s_keys, qs), F32),
        pltpu.VMEM((s_keys, qs), I16),
        pltpu.VMEM((s_keys, qs), I16),
    ]


def _dsa_prompt(proj, out_k, out_v, out_ki, layer, *, nb, tq, qs, kb):
    nsteps = tq // qs
    qcol = lambda w, off: pl.BlockSpec((qs, w), lambda b, j: (b * nsteps + j, off // w))
    kcol = lambda w, off: pl.BlockSpec((tq, w), lambda b, j: (b, off // w))
    cache = lambda rows, w: pl.BlockSpec((rows, w), lambda b, j: (layer * nb + b, 0))
    kern = functools.partial(_dsa_prompt_kernel, s_keys=tq, qs=qs, kb=kb, topk=min(DSA_TOPK, tq // 4))
    return pl.pallas_call(
        kern,
        grid=(nb, nsteps),
        in_specs=[qcol(A_WIDTH, COL_AQ), qcol(IQ_WIDTH, COL_IQ), qcol(LANES, COL_MISC),
                  kcol(A_WIDTH, COL_AK), kcol(A_WIDTH, COL_AV), kcol(LANES, COL_MISC), ANY, ANY, ANY],
        out_specs=[pl.BlockSpec((qs, A_WIDTH), lambda b, j: (b * nsteps + j, 0)),
                   cache(tq * A_HEADS, A_HEAD_DIM), cache(tq * A_HEADS, A_HEAD_DIM), cache(tq, IDX_DIM)],
        out_shape=[jax.ShapeDtypeStruct((proj.shape[0], A_WIDTH), F32),
                   jax.ShapeDtypeStruct(out_k.shape, F32), jax.ShapeDtypeStruct(out_v.shape, F32),
                   jax.ShapeDtypeStruct(out_ki.shape, F32)],
        input_output_aliases={6: 1, 7: 2, 8: 3},
        scratch_shapes=_dsa_scratch(tq, qs),
        compiler_params=_cparams("parallel", "arbitrary"),
        name="dsa_prompt",
    )(proj, proj, proj, proj, proj, proj, out_k, out_v, out_ki)


def _dsa_sample(proj, cache_k, cache_v, cache_ki, buf, out_k, out_v, out_ki, layer,
                *, nb, ts, past, row0, kb):
    rb0 = row0 // ts
    live = past + ts
    s_keys = -(-live // kb) * kb
    qcol = lambda w, off: pl.BlockSpec((ts, w), lambda b, j: (rb0 + b, off // w))
    per_seq = lambda rows, w: pl.BlockSpec((rows, w), lambda b, j: (layer * nb + b, 0))
    kern = functools.partial(_dsa_sample_kernel, s_keys=s_keys, qs=ts, kb=kb,
                             topk=min(DSA_TOPK, live // 4), past=past)
    return pl.pallas_call(
        kern,
        grid=(nb, 1),
        in_specs=[qcol(A_WIDTH, COL_AQ), qcol(IQ_WIDTH, COL_IQ), qcol(LANES, COL_MISC),
                  per_seq(past * A_HEADS, A_HEAD_DIM), per_seq(past * A_HEADS, A_HEAD_DIM),
                  per_seq(past, IDX_DIM),
                  qcol(A_WIDTH, COL_AK), qcol(A_WIDTH, COL_AV), qcol(LANES, COL_MISC),
                  ANY, ANY, ANY, ANY],
        out_specs=[pl.BlockSpec((ts, A_WIDTH), lambda b, j: (rb0 + b, 0)),
                   per_seq(ts * A_HEADS, A_HEAD_DIM), per_seq(ts * A_HEADS, A_HEAD_DIM),
                   per_seq(ts, IDX_DIM)],
        out_shape=[jax.ShapeDtypeStruct(buf.shape, F32), jax.ShapeDtypeStruct(out_k.shape, F32),
                   jax.ShapeDtypeStruct(out_v.shape, F32), jax.ShapeDtypeStruct(out_ki.shape, F32)],
        input_output_aliases={9: 0, 10: 1, 11: 2, 12: 3},
        scratch_shapes=_dsa_scratch(s_keys, ts) + [pltpu.VMEM((s_keys, A_WIDTH), F32)],
        compiler_params=_cparams("parallel", "arbitrary"),
        name="dsa_sample",
    )(proj, proj, proj, cache_k, cache_v, cache_ki, proj, proj, proj, buf, out_k, out_v, out_ki)


def _band_bias_kernel(tab_ref, o_ref, *, qs, w):
    layer = pl.program_id(0)
    q = lax.broadcasted_iota(I32, (qs, w), 0)
    jx = lax.broadcasted_iota(I32, (qs, w), 1)
    idx = jnp.clip(B_WIN + q - jx, -REL_CLIP, REL_CLIP) + REL_CLIP
    back = (q >> 6) + B_PREV_CHUNKS - (jx >> 6)
    allowed = (back >= 0) & (back <= B_PREV_CHUNKS)
    for h in range(B_HEADS):
        row = layer * B_HEADS + h

        def body(v, acc):
            return jnp.where(idx == v, tab_ref[row, v], acc)

        acc = lax.fori_loop(0, 2 * REL_CLIP + 1, body, jnp.zeros((qs, w), F32))
        o_ref[0, h] = jnp.where(allowed, acc, NEG_INF)


def _band_bias(b_rel, qs):
    depth = b_rel.shape[0]
    w = qs + B_WIN
    return pl.pallas_call(
        functools.partial(_band_bias_kernel, qs=qs, w=w),
        grid=(depth,),
        in_specs=[pl.BlockSpec(memory_space=pltpu.SMEM)],
        out_specs=pl.BlockSpec((1, B_HEADS, qs, w), lambda l: (l, 0, 0, 0)),
        out_shape=jax.ShapeDtypeStruct((depth, B_HEADS, qs, w), F32),
        compiler_params=_cparams("arbitrary"),
        name=f"band_bias_{qs}",
    )(b_rel.reshape(depth * B_HEADS, 2 * REL_CLIP + 1))


def _band_attend(q_ref, bias_ref, o_ref, kp_scr, vp_scr, start, valid, qs):
    w = qs + B_WIN
    for h in range(B_HEADS):
        sl = slice(h * B_HEAD_DIM, (h + 1) * B_HEAD_DIM)
        qh = q_ref[:, sl].astype(BF16)
        lg = _nt_dot(qh, kp_scr[pl.ds(start, w), sl]) * (B_HEAD_DIM ** -0.5) + bias_ref[0, h]
        if valid is not None:
            lg = jnp.where(valid, lg, NEG_INF)
        m = jnp.max(lg, axis=-1, keepdims=True)
        e = jnp.exp(lg - m)
        den = jnp.sum(e, axis=-1, keepdims=True)
        o = jnp.dot(e.astype(BF16), vp_scr[pl.ds(start, w), sl], preferred_element_type=F32)
        o_ref[:, sl] = o / den


def _band_prompt_kernel(q_ref, k_ref, v_ref, bias_ref, bufk_ref, bufv_ref, o_ref, ck_ref, cv_ref,
                        kp_scr, vp_scr, *, s_keys, qs, nkeep):
    del bufk_ref, bufv_ref
    j = pl.program_id(1)

    @pl.when(j == 0)
    def _():
        kp_scr[0:B_WIN, :] = jnp.zeros((B_WIN, B_WIDTH), BF16)
        vp_scr[0:B_WIN, :] = jnp.zeros((B_WIN, B_WIDTH), BF16)
        kp_scr[B_WIN:B_WIN + s_keys, :] = k_ref[...].astype(BF16)
        vp_scr[B_WIN:B_WIN + s_keys, :] = v_ref[...].astype(BF16)
        for h in range(B_HEADS):
            sl = slice(h * B_HEAD_DIM, (h + 1) * B_HEAD_DIM)
            _head_rows(ck_ref, h, nkeep, B_HEADS)[...] = k_ref[s_keys - nkeep:, sl]
            _head_rows(cv_ref, h, nkeep, B_HEADS)[...] = v_ref[s_keys - nkeep:, sl]

    start = pl.multiple_of(j * qs, qs)
    valid = lax.broadcasted_iota(I32, (qs, qs + B_WIN), 1) >= B_WIN - j * qs
    _band_attend(q_ref, bias_ref, o_ref, kp_scr, vp_scr, start, valid, qs)


def _band_sample_kernel(q_ref, kc_ref, vc_ref, kn_ref, vn_ref, bias_ref, bufo_ref, bufk_ref, bufv_ref,
                        o_ref, ck_ref, cv_ref, kp_scr, vp_scr, *, qs):
    del bufo_ref, bufk_ref, bufv_ref
    for h in range(B_HEADS):
        sl = slice(h * B_HEAD_DIM, (h + 1) * B_HEAD_DIM)
        kp_scr[0:B_WIN, sl] = _head_rows(kc_ref, h, B_WIN, B_HEADS)[...].astype(BF16)
        vp_scr[0:B_WIN, sl] = _head_rows(vc_ref, h, B_WIN, B_HEADS)[...].astype(BF16)
        _head_rows(ck_ref, h, qs, B_HEADS)[...] = kn_ref[:, sl]
        _head_rows(cv_ref, h, qs, B_HEADS)[...] = vn_ref[:, sl]
    kp_scr[B_WIN:B_WIN + qs, :] = kn_ref[...].astype(BF16)
    vp_scr[B_WIN:B_WIN + qs, :] = vn_ref[...].astype(BF16)
    _band_attend(q_ref, bias_ref, o_ref, kp_scr, vp_scr, 0, None, qs)


def _band_prompt(proj, bias, out_k, out_v, layer, *, nb, tq, qs):
    nsteps = tq // qs
    w = qs + B_WIN
    nkeep = min(B_WIN, tq)
    kcol = lambda off: pl.BlockSpec((tq, B_WIDTH), lambda b, j: (b, off // B_WIDTH))
    cache = pl.BlockSpec((nkeep * B_HEADS, B_HEAD_DIM), lambda b, j: (layer * nb + b, 0))
    return pl.pallas_call(
        functools.partial(_band_prompt_kernel, s_keys=tq, qs=qs, nkeep=nkeep),
        grid=(nb, nsteps),
        in_specs=[
            pl.BlockSpec((qs, B_WIDTH), lambda b, j: (b * nsteps + j, COL_BQ // B_WIDTH)),
            kcol(COL_BK), kcol(COL_BV),
            pl.BlockSpec((1, B_HEADS, qs, w), lambda b, j: (layer, 0, 0, 0)),
            ANY, ANY,
        ],
        out_specs=[pl.BlockSpec((qs, B_WIDTH), lambda b, j: (b * nsteps + j, 0)), cache, cache],
        out_shape=[jax.ShapeDtypeStruct((proj.shape[0], B_WIDTH), F32),
                   jax.ShapeDtypeStruct(out_k.shape, F32), jax.ShapeDtypeStruct(out_v.shape, F32)],
        input_output_aliases={4: 1, 5: 2},
        scratch_shapes=[pltpu.VMEM((B_WIN + tq, B_WIDTH), BF16),
                        pltpu.VMEM((B_WIN + tq, B_WIDTH), BF16)],
        compiler_params=_cparams("parallel", "arbitrary"),
        name="band_prompt",
    )(proj, proj, proj, bias, out_k, out_v)


def _band_sample(proj, cache_k, cache_v, bias, buf, out_k, out_v, layer, *, nb, ts, row0):
    rb0 = row0 // ts
    w = ts + B_WIN
    qcol = lambda off: pl.BlockSpec((ts, B_WIDTH), lambda b, j: (rb0 + b, off // B_WIDTH))
    per_seq = lambda rows: pl.BlockSpec((rows * B_HEADS, B_HEAD_DIM), lambda b, j: (layer * nb + b, 0))
    return pl.pallas_call(
        functools.partial(_band_sample_kernel, qs=ts),
        grid=(nb, 1),
        in_specs=[qcol(COL_BQ), per_seq(B_WIN), per_seq(B_WIN), qcol(COL_BK), qcol(COL_BV),
                  pl.BlockSpec((1, B_HEADS, ts, w), lambda b, j: (layer, 0, 0, 0)),
                  ANY, ANY, ANY],
        out_specs=[pl.BlockSpec((ts, B_WIDTH), lambda b, j: (rb0 + b, 0)), per_seq(ts), per_seq(ts)],
        out_shape=[jax.ShapeDtypeStruct(buf.shape, F32), jax.ShapeDtypeStruct(out_k.shape, F32),
                   jax.ShapeDtypeStruct(out_v.shape, F32)],
        input_output_aliases={6: 0, 7: 1, 8: 2},
        scratch_shapes=[pltpu.VMEM((w, B_WIDTH), BF16), pltpu.VMEM((w, B_WIDTH), BF16)],
        compiler_params=_cparams("parallel", "arbitrary"),
        name="band_sample",
    )(proj, cache_k, cache_v, proj, proj, bias, buf, out_k, out_v)


XP_OFF = 8
_HEAD_SPREAD = np.zeros((LANES, C_WIDTH), np.float32)
for _h in range(C_HEADS):
    _HEAD_SPREAD[MISC_DT + _h, _h * C_HEAD_DIM:(_h + 1) * C_HEAD_DIM] = 1.0


def _ssd_kernel(*refs, zero_init, nchunks):
    if zero_init:
        (z_ref, xbc_ref, misc_ref, cw_ref, cbias_ref, prm_ref, gn_ref, ex_ref, dskip_ref,
         bufh_ref, bufc_ref, o_ref, hout_ref, cnew_ref, st_scr, xp_scr, y_scr) = refs
    else:
        (z_ref, xbc_ref, misc_ref, h0_ref, cb0_ref, cw_ref, cbias_ref, prm_ref, gn_ref, ex_ref, dskip_ref,
         bufo_ref, bufh_ref, bufc_ref, o_ref, hout_ref, cnew_ref, st_scr, xp_scr, y_scr) = refs
        del bufo_ref
    del bufh_ref, bufc_ref
    c = pl.program_id(1)
    tail = CONV_W - 1
    rep = C_HEADS // C_GROUPS
    gw = C_WIDTH // C_GROUPS

    @pl.when(c == 0)
    def _():
        if zero_init:
            st_scr[...] = jnp.zeros(st_scr.shape, F32)
            xp_scr[XP_OFF - tail:XP_OFF, :] = jnp.zeros((tail, CONV_DIM), F32)
        else:
            for h in range(C_HEADS):
                st_scr[h // rep, :, (h % rep) * C_HEAD_DIM:(h % rep + 1) * C_HEAD_DIM] = h0_ref[0, h].T
            xp_scr[XP_OFF - tail:XP_OFF, :] = cb0_ref[0]

    xp_scr[XP_OFF:XP_OFF + CHUNK, :] = xbc_ref[...]
    y = cbias_ref[...] + xp_scr[XP_OFF - tail:XP_OFF - tail + CHUNK, :] * cw_ref[0:1, :]
    for t in range(1, CONV_W):
        y = y + xp_scr[XP_OFF - tail + t:XP_OFF - tail + t + CHUNK, :] * cw_ref[t:t + 1, :]
    new_tail = xp_scr[XP_OFF + CHUNK - tail:XP_OFF + CHUNK, :]
    xp_scr[XP_OFF - tail:XP_OFF, :] = new_tail
    cnew_ref[0] = new_tail
    xc = y * (1.0 / (1.0 + jnp.exp(-y)))
    xs = xc[:, 0:C_WIDTH]

    pre = misc_ref[...] + prm_ref[0:1, :]
    dt = jnp.maximum(pre, 0.0) + jnp.log(1.0 + jnp.exp(-jnp.abs(pre)))
    ad = dt * (-jnp.exp(prm_ref[1:2, :]))
    li = lax.broadcasted_iota(I32, (CHUNK, CHUNK), 0)
    si = lax.broadcasted_iota(I32, (CHUNK, CHUNK), 1)
    causal = li >= si
    tri = causal.astype(BF16)
    acs = sum(jnp.dot(tri, part, preferred_element_type=F32) for part in _split3(ad))
    acs_t = acs.T
    ex = ex_ref[...]
    spread = lambda v: sum(jnp.dot(part, ex, preferred_element_type=F32) for part in _split3(v))
    dt_x = spread(dt)
    acs_x = spread(acs)
    acs_last_x = acs_x[CHUNK - 1:CHUNK, :]
    xd_all = xs * dt_x
    xd_b = xd_all.astype(BF16)
    xde_b = (xd_all * jnp.exp(acs_last_x - acs_x)).astype(BF16)
    grow_x = jnp.exp(acs_x)
    chunk_decay_x = jnp.exp(acs_last_x)
    head_of_lane = lax.broadcasted_iota(I32, (CHUNK, gw), 1) >> 6

    for g in range(C_GROUPS):
        gl = slice(g * gw, (g + 1) * gw)
        b_g = xc[:, C_WIDTH + g * C_STATE:C_WIDTH + (g + 1) * C_STATE]
        c_g = xc[:, C_WIDTH + (C_GROUPS + g) * C_STATE:C_WIDTH + (C_GROUPS + g + 1) * C_STATE]
        b_gt = b_g.T.astype(BF16)
        c_gb = c_g.astype(BF16)
        cb = jnp.dot(c_gb, b_gt, preferred_element_type=F32)
        st = st_scr[g]
        y_g = jnp.dot(c_gb, st.astype(BF16), preferred_element_type=F32) * grow_x[:, gl]
        xd_g = xd_b[:, gl]
        for r in range(rep):
            col = MISC_DT + g * rep + r
            seg = acs[:, col:col + 1] - acs_t[col:col + 1, :]
            decay_in = jnp.where(causal, jnp.exp(jnp.where(causal, seg, 0.0)), 0.0)
            t_r = jnp.dot((cb * decay_in).astype(BF16), xd_g, preferred_element_type=F32)
            y_g = y_g + jnp.where(head_of_lane == r, t_r, 0.0)
        y_scr[:, gl] = y_g + dskip_ref[:, gl] * xs[:, gl]
        st_scr[g] = st * chunk_decay_x[:, gl] + jnp.dot(b_gt, xde_b[:, gl], preferred_element_type=F32)

    @pl.when(c == nchunks - 1)
    def _():
        for h in range(C_HEADS):
            hout_ref[0, h] = st_scr[h // rep, :, (h % rep) * C_HEAD_DIM:(h % rep + 1) * C_HEAD_DIM].T

    z = z_ref[...]
    gate = y_scr[...] * (z * (1.0 / (1.0 + jnp.exp(-z))))
    for g in range(C_GROUPS):
        gg = gate[:, g * gw:(g + 1) * gw]
        ms = jnp.mean(gg * gg, axis=-1, keepdims=True)
        o_ref[:, g * gw:(g + 1) * gw] = gg * lax.rsqrt(ms + EPS) * gn_ref[:, g * gw:(g + 1) * gw]


def _ssd(proj, conv_w, conv_b, prm, gate_norm, out_h, out_c, layer, *, nb, nchunks, row0, state=None):
    rb0 = row0 // CHUNK
    row = lambda b, c: rb0 + b * nchunks + c
    const = lambda shape: pl.BlockSpec(shape, lambda b, c: tuple(0 for _ in shape))
    per_seq_h = pl.BlockSpec((1, C_HEADS, C_HEAD_DIM, C_STATE), lambda b, c: (layer * nb + b, 0, 0, 0))
    per_seq_c = pl.BlockSpec((1, CONV_W - 1, CONV_DIM), lambda b, c: (layer * nb + b, 0, 0))
    in_specs = [
        pl.BlockSpec((CHUNK, C_WIDTH), lambda b, c: (row(b, c), COL_CZ // C_WIDTH)),
        pl.BlockSpec((CHUNK, CONV_DIM), lambda b, c: (row(b, c), COL_CXBC // CONV_DIM)),
        pl.BlockSpec((CHUNK, LANES), lambda b, c: (row(b, c), COL_MISC // LANES)),
    ]
    args = [proj, proj, proj]
    if state is not None:
        h0, cbuf0, buf = state
        in_specs += [per_seq_h, per_seq_c]
        args += [h0, cbuf0]
    in_specs += [const((CONV_W, CONV_DIM)), const((1, CONV_DIM)), const((8, LANES)), const((1, C_WIDTH)),
                 const((LANES, C_WIDTH)), const((1, C_WIDTH))]
    args += [conv_w, conv_b.reshape(1, CONV_DIM), prm[0], gate_norm.reshape(1, C_WIDTH),
             jnp.asarray(_HEAD_SPREAD, BF16), prm[1]]
    aliases = {}
    if state is not None:
        in_specs.append(ANY)
        args.append(buf)
        aliases[len(args) - 1] = 0
    in_specs += [ANY, ANY]
    args += [out_h, out_c]
    aliases[len(args) - 2] = 1
    aliases[len(args) - 1] = 2
    return pl.pallas_call(
        functools.partial(_ssd_kernel, zero_init=state is None, nchunks=nchunks),
        grid=(nb, nchunks),
        in_specs=in_specs,
        out_specs=[pl.BlockSpec((CHUNK, C_WIDTH), lambda b, c: (row(b, c), 0)), per_seq_h, per_seq_c],
        out_shape=[
            jax.ShapeDtypeStruct((proj.shape[0], C_WIDTH), F32),
            jax.ShapeDtypeStruct(out_h.shape, F32),
            jax.ShapeDtypeStruct(out_c.shape, F32),
        ],
        input_output_aliases=aliases,
        scratch_shapes=[pltpu.VMEM((C_GROUPS, C_STATE, C_WIDTH // C_GROUPS), F32),
                        pltpu.VMEM((XP_OFF + CHUNK, CONV_DIM), F32),
                        pltpu.VMEM((CHUNK, C_WIDTH), F32)],
        compiler_params=_cparams("parallel", "arbitrary"),
        name="ssd_prompt" if state is None else "ssd_sample",
    )(*args)


def _outproj_kernel(x_ref, oa_ref, ob_ref, oc_ref, w_ref, o_ref):
    acc = x_ref[...]
    acc = acc + jnp.dot(oa_ref[...].astype(BF16), w_ref[0:A_WIDTH, :], preferred_element_type=F32)
    acc = acc + jnp.dot(ob_ref[...].astype(BF16), w_ref[A_WIDTH:A_WIDTH + B_WIDTH, :],
                        preferred_element_type=F32)
    acc = acc + jnp.dot(oc_ref[...].astype(BF16), w_ref[A_WIDTH + B_WIDTH:, :],
                        preferred_element_type=F32)
    o_ref[...] = acc


def _out_proj(x, oa, ob, oc, w_bf16):
    t, d = x.shape
    tm = _row_tile(t, 512)
    rows = lambda w: pl.BlockSpec((tm, w), lambda i: (i, 0))
    return pl.pallas_call(
        _outproj_kernel,
        grid=(t // tm,),
        in_specs=[rows(d), rows(A_WIDTH), rows(B_WIDTH), rows(C_WIDTH),
                  pl.BlockSpec((d, d), lambda i: (0, 0))],
        out_specs=rows(d),
        out_shape=jax.ShapeDtypeStruct((t, d), F32),
        compiler_params=_cparams("parallel"),
        name="out_proj",
    )(x, oa, ob, oc, w_bf16)


def _ffn_kernel(x_ref, nw_ref, wu_ref, wd_ref, o_ref, h_scr):
    @pl.when(pl.program_id(1) == 0)
    def _():
        x = x_ref[...]
        ms = jnp.mean(x * x, axis=-1, keepdims=True)
        h_scr[...] = (x * lax.rsqrt(ms + EPS) * nw_ref[...]).astype(BF16)
        o_ref[...] = x

    u = jnp.maximum(jnp.dot(h_scr[...], wu_ref[...], preferred_element_type=F32), 0.0)
    o_ref[...] += jnp.dot((u * u).astype(BF16), wd_ref[...], preferred_element_type=F32)


def _ffn(x, norm_w, wu_bf16, wd_bf16):
    t, d = x.shape
    ff = wu_bf16.shape[1]
    tm = _row_tile(t, 1024)
    tf = 512
    return pl.pallas_call(
        _ffn_kernel,
        grid=(t // tm, ff // tf),
        in_specs=[
            pl.BlockSpec((tm, d), lambda i, f: (i, 0)),
            pl.BlockSpec((1, d), lambda i, f: (0, 0)),
            pl.BlockSpec((d, tf), lambda i, f: (0, f)),
            pl.BlockSpec((tf, d), lambda i, f: (f, 0)),
        ],
        out_specs=pl.BlockSpec((tm, d), lambda i, f: (i, 0)),
        out_shape=jax.ShapeDtypeStruct((t, d), F32),
        scratch_shapes=[pltpu.VMEM((tm, d), BF16)],
        compiler_params=_cparams("parallel", "arbitrary"),
        name="ffn",
    )(x, norm_w.reshape(1, d), wu_bf16, wd_bf16)


def _norm_kernel(x_ref, nw_ref, o_ref):
    x = x_ref[...]
    ms = jnp.mean(x * x, axis=-1, keepdims=True)
    o_ref[...] = x * lax.rsqrt(ms + EPS) * nw_ref[...]


def _final_norm(x, norm_w, row0, nrows):
    d = x.shape[1]
    tm = _row_tile(nrows, 512)
    assert row0 % tm == 0
    return pl.pallas_call(
        _norm_kernel,
        grid=(nrows // tm,),
        in_specs=[pl.BlockSpec((tm, d), lambda i: (row0 // tm + i, 0)),
                  pl.BlockSpec((1, d), lambda i: (0, 0))],
        out_specs=pl.BlockSpec((tm, d), lambda i: (i, 0)),
        out_shape=jax.ShapeDtypeStruct((nrows, d), F32),
        compiler_params=_cparams("parallel"),
        name="final_norm",
    )(x, norm_w.reshape(1, d))


def _lane_row(v, off):
    return jnp.zeros((LANES,), F32).at[off:off + v.shape[0]].set(v)


def _trunk(x_prompt, x_sample, cache_a_k, cache_a_v, cache_a_kidx, cache_b_k, cache_b_v,
           state_ssm, state_conv, norm1, w_in, w_out, b_rel, conv_w, conv_b, dt_bias,
           a_log, d_skip, gate_norm, norm2, w_up, w_down, final_norm):
    bp, tp, d = x_prompt.shape
    bs, ts, _ = x_sample.shape
    depth = w_in.shape[0]
    past = cache_a_k.shape[2]
    nbuf = cache_b_k.shape[2]
    n_p = bp * tp
    n_s = bs * ts
    assert ts == CHUNK and tp % DSA_KB_PROMPT == 0 and nbuf == B_WIN
    qs_p = LANES
    nkeep = min(B_WIN, tp)

    x = jnp.concatenate([x_prompt.reshape(n_p, d), x_sample.reshape(n_s, d)], axis=0)

    pos = jnp.concatenate([jnp.tile(jnp.arange(tp), bp), jnp.tile(past + jnp.arange(ts), bs)])
    tab_a = _rope_tables(pos, A_HEAD_DIM)
    tab_i = _rope_tables(pos, IDX_DIM)
    bias_p = _band_bias(b_rel, qs_p)
    bias_s = _band_bias(b_rel, ts)
    w_in_r = _regroup_w_in(w_in)
    cak = cache_a_k.reshape(depth * bs * past * A_HEADS, A_HEAD_DIM)
    cav = cache_a_v.reshape(depth * bs * past * A_HEADS, A_HEAD_DIM)
    caki = cache_a_kidx.reshape(depth * bs * past, IDX_DIM)
    cbk = cache_b_k.reshape(depth * bs * nbuf * B_HEADS, B_HEAD_DIM)
    cbv = cache_b_v.reshape(depth * bs * nbuf * B_HEADS, B_HEAD_DIM)
    h0 = state_ssm.reshape(depth * bs, C_HEADS, C_HEAD_DIM, C_STATE)
    cbuf0 = state_conv.reshape(depth * bs, CONV_W - 1, CONV_DIM)

    p_ak = _uninit((depth * bp * tp * A_HEADS, A_HEAD_DIM))
    p_av = _uninit((depth * bp * tp * A_HEADS, A_HEAD_DIM))
    p_aki = _uninit((depth * bp * tp, IDX_DIM))
    p_bk = _uninit((depth * bp * nkeep * B_HEADS, B_HEAD_DIM))
    p_bv = _uninit((depth * bp * nkeep * B_HEADS, B_HEAD_DIM))
    p_h = _uninit((depth * bp, C_HEADS, C_HEAD_DIM, C_STATE))
    p_c = _uninit((depth * bp, CONV_W - 1, CONV_DIM))
    s_ak = _uninit((depth * bs * ts * A_HEADS, A_HEAD_DIM))
    s_av = _uninit((depth * bs * ts * A_HEADS, A_HEAD_DIM))
    s_aki = _uninit((depth * bs * ts, IDX_DIM))
    s_bk = _uninit((depth * bs * ts * B_HEADS, B_HEAD_DIM))
    s_bv = _uninit((depth * bs * ts * B_HEADS, B_HEAD_DIM))
    s_h = _uninit((depth * bs, C_HEADS, C_HEAD_DIM, C_STATE))
    s_c = _uninit((depth * bs, CONV_W - 1, CONV_DIM))

    for l in range(depth):
        proj = _in_proj(x, norm1[l], w_in_r, l, tab_a, tab_i)

        oa, p_ak, p_av, p_aki = _dsa_prompt(proj, p_ak, p_av, p_aki, l, nb=bp, tq=tp, qs=qs_p,
                                            kb=DSA_KB_PROMPT)
        oa, s_ak, s_av, s_aki = _dsa_sample(proj, cak, cav, caki, oa, s_ak, s_av, s_aki, l,
                                            nb=bs, ts=ts, past=past, row0=n_p, kb=DSA_KB_SAMPLE)
        ob, p_bk, p_bv = _band_prompt(proj, bias_p, p_bk, p_bv, l, nb=bp, tq=tp, qs=qs_p)
        ob, s_bk, s_bv = _band_sample(proj, cbk, cbv, bias_s, ob, s_bk, s_bv, l, nb=bs, ts=ts, row0=n_p)
        rows = jnp.zeros((8, LANES), F32)
        rows = rows.at[0].set(_lane_row(dt_bias[l], MISC_DT)).at[1].set(_lane_row(a_log[l], MISC_DT))
        prm = (rows, jnp.repeat(d_skip[l], C_HEAD_DIM).reshape(1, C_WIDTH))
        oc, p_h, p_c = _ssd(proj, conv_w[l], conv_b[l], prm, gate_norm[l], p_h, p_c, l,
                            nb=bp, nchunks=tp // CHUNK, row0=0)
        oc, s_h, s_c = _ssd(proj, conv_w[l], conv_b[l], prm, gate_norm[l], s_h, s_c, l,
                            nb=bs, nchunks=1, row0=n_p, state=(h0, cbuf0, oc))

        x = _out_proj(x, oa, ob, oc, w_out[l].astype(BF16))
        x = _ffn(x, norm2[l], w_up[l].astype(BF16), w_down[l].astype(BF16))

    y_prompt = _final_norm(x, final_norm, 0, n_p).reshape(bp, tp, d)
    y_sample = _final_norm(x, final_norm, n_p, n_s).reshape(bs, ts, d)
    return (y_prompt, y_sample,
            p_ak.reshape(depth, bp, tp, A_HEADS, A_HEAD_DIM),
            p_av.reshape(depth, bp, tp, A_HEADS, A_HEAD_DIM),
            p_aki.reshape(depth, bp, tp, IDX_DIM),
            p_bk.reshape(depth, bp, nkeep, B_HEADS, B_HEAD_DIM),
            p_bv.reshape(depth, bp, nkeep, B_HEADS, B_HEAD_DIM),
            p_h.reshape(depth, bp, C_HEADS, C_HEAD_DIM, C_STATE),
            p_c.reshape(depth, bp, CONV_W - 1, CONV_DIM),
            s_ak.reshape(depth, bs, ts, A_HEADS, A_HEAD_DIM),
            s_av.reshape(depth, bs, ts, A_HEADS, A_HEAD_DIM),
            s_aki.reshape(depth, bs, ts, IDX_DIM),
            s_bk.reshape(depth, bs, ts, B_HEADS, B_HEAD_DIM),
            s_bv.reshape(depth, bs, ts, B_HEADS, B_HEAD_DIM),
            s_h.reshape(depth, bs, C_HEADS, C_HEAD_DIM, C_STATE),
            s_c.reshape(depth, bs, CONV_W - 1, CONV_DIM))


def kernel(x_prompt, x_sample, cache_a_k, cache_a_v, cache_a_kidx, cache_b_k, cache_b_v, state_ssm, state_conv, norm1, w_in, w_out, b_rel, conv_w, conv_b, dt_bias, a_log, d_skip, gate_norm, norm2, w_up, w_down, final_norm):
    return _trunk(x_prompt, x_sample, cache_a_k, cache_a_v, cache_a_kidx, cache_b_k, cache_b_v,
                  state_ssm, state_conv, norm1, w_in, w_out, b_rel, conv_w, conv_b, dt_bias,
                  a_log, d_skip, gate_norm, norm2, w_up, w_down, final_norm)
```

```python
import functools

import jax
import jax.numpy as jnp
import numpy as np
from jax import lax
from jax.experimental import pallas as pl
from jax.experimental.pallas import tpu as pltpu

F32 = jnp.float32
BF16 = jnp.bfloat16
I32 = jnp.int32
I16 = jnp.int16

D_MODEL = 2048
CHUNK = 64
A_HEADS = 4
A_HEAD_DIM = 128
A_WIDTH = A_HEADS * A_HEAD_DIM
IDX_HEADS = 16
IDX_DIM = 64
IQ_WIDTH = IDX_HEADS * IDX_DIM
DSA_TOPK = 256
B_HEADS = 4
B_HEAD_DIM = 128
B_WIDTH = B_HEADS * B_HEAD_DIM
B_PREV_CHUNKS = 8
B_WIN = B_PREV_CHUNKS * CHUNK
REL_CLIP = 128
C_WIDTH = 1024
C_HEAD_DIM = 64
C_HEADS = C_WIDTH // C_HEAD_DIM
C_GROUPS = 4
C_STATE = 128
CONV_W = 4
CONV_DIM = C_WIDTH + 2 * C_GROUPS * C_STATE
D_FF = 4 * D_MODEL
ROPE_THETA = 500000.0
EPS = 1e-5

LANES = 128
INT_MIN = -(2 ** 31)
NEG_INF = float("-inf")

COL_AQ = 0
COL_AK = 512
COL_IQ = 1024
COL_AV = 2048
COL_BQ = 2560
COL_BK = 3072
COL_BV = 3584
COL_CXBC = 4096
COL_CZ = 6144
COL_MISC = 7168
MISC_IW = 64
MISC_DT = 80
IN_COLS_PAD = 7680
IN_TN = 1280
SRC_AQ, SRC_AV, SRC_IQ, SRC_IK, SRC_BQ, SRC_CZ, SRC_CXBC, SRC_DT, SRC_END = (
    0, 1024, 1536, 2560, 2640, 4176, 5200, 7248, 7264)

VMEM_LIMIT = 56 * 1024 * 1024
ANY = pl.BlockSpec(memory_space=pl.ANY)


def _cparams(*sem):
    return pltpu.CompilerParams(dimension_semantics=sem, vmem_limit_bytes=VMEM_LIMIT)


def _row_tile(t, cap):
    tm = cap
    while t % tm:
        tm //= 2
    return tm


def _nt_dot(a, b):
    return lax.dot_general(a, b, (((1,), (1,)), ((), ())), preferred_element_type=F32)


def _split3(x):
    hi = x.astype(BF16)
    r1 = x - hi.astype(F32)
    mid = r1.astype(BF16)
    lo = (r1 - mid.astype(F32)).astype(BF16)
    return hi, mid, lo


def _head_rows(ref, h, n, heads):
    return ref.at[pl.ds(h, n, stride=heads), :]


def _uninit_kernel(o_ref):
    del o_ref


def _uninit(shape, dtype=F32):
    return pl.pallas_call(_uninit_kernel, out_specs=ANY, out_shape=jax.ShapeDtypeStruct(shape, dtype),
                          name="alloc")()


def _regroup_kernel(w_ref, o_ref):
    def put(dst, src, width):
        o_ref[0, :, dst:dst + width] = w_ref[0, :, src:src + width].astype(BF16)

    rows = w_ref.shape[1]
    put(COL_AQ, SRC_AQ, 2 * A_WIDTH)
    put(COL_IQ, SRC_IQ, IQ_WIDTH)
    put(COL_AV, SRC_AV, A_WIDTH)
    put(COL_BQ, SRC_BQ, 3 * B_WIDTH)
    put(COL_CXBC, SRC_CXBC, CONV_DIM)
    put(COL_CZ, SRC_CZ, C_WIDTH)
    o_ref[0, :, COL_MISC:] = jnp.zeros((rows, IN_COLS_PAD - COL_MISC), BF16)
    put(COL_MISC, SRC_IK, IDX_DIM + IDX_HEADS)
    put(COL_MISC + MISC_DT, SRC_DT, C_HEADS)


def _regroup_w_in(w_in):
    depth, d, n = w_in.shape
    assert n == SRC_END
    tk = 256
    return pl.pallas_call(
        _regroup_kernel,
        grid=(depth, d // tk),
        in_specs=[pl.BlockSpec((1, tk, n), lambda l, i: (l, i, 0))],
        out_specs=pl.BlockSpec((1, tk, IN_COLS_PAD), lambda l, i: (l, i, 0)),
        out_shape=jax.ShapeDtypeStruct((depth, d, IN_COLS_PAD), BF16),
        compiler_params=_cparams("parallel", "parallel"),
        name="regroup_w_in",
    )(w_in)


def _rope_block(x, c, sm, sp, half):
    return x * c + pltpu.roll(x, LANES - half, 1) * sm + pltpu.roll(x, half, 1) * sp


def _inproj_kernel(x_ref, nw_ref, w_ref, ta_ref, ti_ref, o_ref, h_scr):
    j = pl.program_id(1)

    @pl.when(j == 0)
    def _():
        x = x_ref[...]
        ms = jnp.mean(x * x, axis=-1, keepdims=True)
        h_scr[...] = (x * lax.rsqrt(ms + EPS) * nw_ref[...]).astype(BF16)

    o_ref[...] = jnp.dot(h_scr[...], w_ref[0], preferred_element_type=F32)

    def rotate(tile):
        for blk in range(IN_TN // LANES):
            col = tile * IN_TN + blk * LANES
            sl = slice(blk * LANES, (blk + 1) * LANES)
            if col < COL_IQ:
                o_ref[:, sl] = _rope_block(o_ref[:, sl], ta_ref[0], ta_ref[1], ta_ref[2],
                                           A_HEAD_DIM // 8)
            elif col < COL_AV:
                o_ref[:, sl] = _rope_block(o_ref[:, sl], ti_ref[0], ti_ref[1], ti_ref[2], IDX_DIM // 8)
            elif col == COL_MISC:
                m = o_ref[:, sl]
                lane = lax.broadcasted_iota(I32, m.shape, 1)
                rot = _rope_block(m, ti_ref[0], ti_ref[1], ti_ref[2], IDX_DIM // 8)
                o_ref[:, sl] = jnp.where(lane < IDX_DIM, rot, m)

    for tile in range(IN_COLS_PAD // IN_TN):
        lo, hi = tile * IN_TN, (tile + 1) * IN_TN
        if lo < COL_AV or lo <= COL_MISC < hi:
            pl.when(j == tile)(functools.partial(rotate, tile))


def _in_proj(x, norm_w, w_all, layer, tab_a, tab_i):
    t, d = x.shape
    n = w_all.shape[2]
    tm = _row_tile(t, 1024)
    tab = pl.BlockSpec((3, tm, LANES), lambda i, j: (0, i, 0))
    return pl.pallas_call(
        _inproj_kernel,
        grid=(t // tm, n // IN_TN),
        in_specs=[
            pl.BlockSpec((tm, d), lambda i, j: (i, 0)),
            pl.BlockSpec((1, d), lambda i, j: (0, 0)),
            pl.BlockSpec((1, d, IN_TN), lambda i, j: (layer, 0, j)),
            tab, tab,
        ],
        out_specs=pl.BlockSpec((tm, IN_TN), lambda i, j: (i, j)),
        out_shape=jax.ShapeDtypeStruct((t, n), F32),
        scratch_shapes=[pltpu.VMEM((tm, d), BF16)],
        compiler_params=_cparams("parallel", "arbitrary"),
        name="in_proj",
    )(x, norm_w.reshape(1, d), w_all, tab_a, tab_i)


def _rope_tables(pos, head_dim):
    rot = head_dim // 4
    half = rot // 2
    inv = ROPE_THETA ** (-jnp.arange(half, dtype=F32) * 2.0 / rot)
    ang = pos.astype(F32)[:, None] * inv[None, :]
    cos, sin = jnp.cos(ang), jnp.sin(ang)
    n = pos.shape[0]
    rest = head_dim - rot
    c = jnp.concatenate([cos, cos, jnp.ones((n, rest), F32)], axis=1)
    sm = jnp.concatenate([-sin, jnp.zeros((n, half + rest), F32)], axis=1)
    sp = jnp.concatenate([jnp.zeros((n, half), F32), sin, jnp.zeros((n, rest), F32)], axis=1)
    tab = jnp.stack([c, sm, sp], axis=0)
    return jnp.tile(tab, (1, 1, LANES // head_dim))


TIE_BLOCK = 64
SUB = 128
DSA_KB_PROMPT = 512
DSA_KB_SAMPLE = 384


def _tree_sum(parts):
    while len(parts) > 1:
        parts = [parts[i] + parts[i + 1] for i in range(0, len(parts) - 1, 2)] + (
            parts[-1:] if len(parts) % 2 else [])
    return parts[0]


def _count_keys(key_scr, n_keys, qs, pred):
    dt = key_scr.dtype
    pack = 8 * (4 // dt.itemsize)
    blocks = []
    for u in range(n_keys // SUB):
        hit = jnp.where(pred(key_scr[u * SUB:(u + 1) * SUB, :]), jnp.ones((), dt), jnp.zeros((), dt))
        blocks.append(_tree_sum([hit[i * pack:(i + 1) * pack, :] for i in range(SUB // pack)]))
    return jnp.sum(_tree_sum(blocks).astype(I32), axis=0, keepdims=True)


def _kth_largest(scr, n_keys, qs, k, nbits):
    dt = scr.dtype
    count = functools.partial(_count_keys, scr, n_keys, qs)
    lowest = -(2 ** (nbits - 1))
    ans0 = jnp.where(count(lambda x: x >= jnp.zeros((1, qs), dt)) >= k, 0, lowest).astype(I32)

    def bit_step(i, ans):
        cand = ans | (jnp.int32(1) << (nbits - 2 - i))
        return jnp.where(count(lambda x: x >= cand.astype(dt)) >= k, cand, ans)

    return lax.fori_loop(0, nbits - 1, bit_step, ans0)


def _dsa_select_attend(q_ref, qi_ref, mq_ref, o_ref, kb_scr, vt_scr, kib_scr, w_scr, score_scr,
                       key_scr, mask_scr, hi_scr, lo_scr, *, n_keys, qs, topk, limit):
    w_scr[...] = mq_ref[...].T * (IDX_HEADS ** -0.5 * IDX_DIM ** -0.5)

    kib = kib_scr[0:n_keys, :]
    for h in range(IDX_HEADS):
        qh = qi_ref[:, h * IDX_DIM:(h + 1) * IDX_DIM].astype(BF16)
        term = jnp.maximum(_nt_dot(kib, qh), 0.0) * w_scr[MISC_IW + h:MISC_IW + h + 1, :]
        if h == 0:
            score_scr[0:n_keys, :] = term
        else:
            score_scr[0:n_keys, :] += term
    sc = score_scr[0:n_keys, :]
    sc = jnp.where(sc == 0.0, 0.0, sc)
    bits = lax.bitcast_convert_type(sc, I32)
    key = bits ^ ((bits >> 31) & 0x7FFFFFFF)
    s_idx = lax.broadcasted_iota(I32, (n_keys, qs), 0)
    key = jnp.where(s_idx < limit, key, INT_MIN)
    key_scr[0:n_keys, :] = key
    hi_scr[0:n_keys, :] = (key >> 16).astype(I16)

    thr_hi = _kth_largest(hi_scr, n_keys, qs, topk, 16)
    above = _count_keys(hi_scr, n_keys, qs, lambda x: x > thr_hi.astype(I16))
    key = key_scr[0:n_keys, :]
    lo = ((key & 0xFFFF) - 2 ** 15).astype(I16)
    lo_scr[0:n_keys, :] = jnp.where((key >> 16) == thr_hi, lo, jnp.int16(-(2 ** 15)))
    thr_lo = _kth_largest(lo_scr, n_keys, qs, topk - above, 16)
    thr = (thr_hi << 16) | (thr_lo + 2 ** 15)

    cnt_gt = _count_keys(key_scr, n_keys, qs, lambda x: x > thr)
    cnt_eq = _count_keys(key_scr, n_keys, qs, lambda x: x == thr)
    room = topk - cnt_gt
    live = thr > INT_MIN
    all_ties = (cnt_eq <= room) & live
    key = key_scr[0:n_keys, :]
    mask_scr[0:n_keys, :] = jnp.where((key > thr) | ((key == thr) & all_ties), 0.0, NEG_INF)
    need_ties = jnp.max(((cnt_eq > room) & live).astype(I32)) > 0

    @pl.when(need_ties)
    def _():
        tri = (lax.broadcasted_iota(I32, (TIE_BLOCK, TIE_BLOCK), 0)
               >= lax.broadcasted_iota(I32, (TIE_BLOCK, TIE_BLOCK), 1)).astype(BF16)

        def blk(b, carry):
            rows = pl.ds(pl.multiple_of(b * TIE_BLOCK, TIE_BLOCK), TIE_BLOCK)
            kblk = key_scr[rows, :]
            eqb = (kblk == thr) & live
            eqf = eqb.astype(F32)
            prefix = jnp.dot(tri, eqf.astype(BF16), preferred_element_type=F32) + carry
            keep = (kblk > thr) | (eqb & (prefix <= room.astype(F32)))
            mask_scr[rows, :] = jnp.where(keep, 0.0, NEG_INF)
            return carry + jnp.sum(eqf, axis=0, keepdims=True)

        lax.fori_loop(0, n_keys // TIE_BLOCK, blk, jnp.zeros((1, qs), F32))

    for h in range(A_HEADS):
        sl = slice(h * A_HEAD_DIM, (h + 1) * A_HEAD_DIM)
        qh = q_ref[:, sl].astype(BF16)
        lg = _nt_dot(kb_scr[0:n_keys, sl], qh) * (A_HEAD_DIM ** -0.5) + mask_scr[0:n_keys, :]
        m = jnp.max(lg, axis=0, keepdims=True)
        e = jnp.exp(lg - m)
        den = jnp.sum(e, axis=0, keepdims=True)
        o_t = jnp.dot(vt_scr[sl, 0:n_keys], e.astype(BF16), preferred_element_type=F32)
        o_ref[:, sl] = (o_t / den).T


def _dsa_prompt_kernel(q_ref, qi_ref, mq_ref, k_ref, v_ref, mk_ref, bufk_ref, bufv_ref, bufi_ref,
                       o_ref, ck_ref, cv_ref, ci_ref,
                       kb_scr, vt_scr, kib_scr, w_scr, score_scr, key_scr, mask_scr, hi_scr, lo_scr,
                       *, s_keys, qs, kb, topk):
    del bufk_ref, bufv_ref, bufi_ref
    j = pl.program_id(1)

    @pl.when(j == 0)
    def _():
        kb_scr[...] = k_ref[...].astype(BF16)
        kib_scr[...] = mk_ref[:, 0:IDX_DIM].astype(BF16)
        for i in range(s_keys // kb):
            vt_scr[:, i * kb:(i + 1) * kb] = v_ref[i * kb:(i + 1) * kb, :].T.astype(BF16)
        ci_ref[...] = mk_ref[:, 0:IDX_DIM]
        for h in range(A_HEADS):
            sl = slice(h * A_HEAD_DIM, (h + 1) * A_HEAD_DIM)
            _head_rows(ck_ref, h, s_keys, A_HEADS)[...] = k_ref[:, sl]
            _head_rows(cv_ref, h, s_keys, A_HEADS)[...] = v_ref[:, sl]

    q_pos = j * qs + lax.broadcasted_iota(I32, (1, qs), 1)
    limit = ((q_pos >> 6) + 1) * CHUNK
    nsb = (j * qs + qs + kb - 1) // kb
    for n in range(1, s_keys // kb + 1):
        pl.when(nsb == n)(functools.partial(
            _dsa_select_attend, q_ref, qi_ref, mq_ref, o_ref, kb_scr, vt_scr, kib_scr, w_scr, score_scr,
            key_scr, mask_scr, hi_scr, lo_scr, n_keys=n * kb, qs=qs, topk=topk, limit=limit))


def _dsa_sample_kernel(q_ref, qi_ref, mq_ref, kc_ref, vc_ref, kic_ref, kn_ref, vn_ref, mkn_ref,
                       bufo_ref, bufk_ref, bufv_ref, bufi_ref,
                       o_ref, ck_ref, cv_ref, ci_ref,
                       kb_scr, vt_scr, kib_scr, w_scr, score_scr, key_scr, mask_scr, hi_scr, lo_scr,
                       vrow_scr, *, s_keys, qs, kb, topk, past):
    del bufo_ref, bufk_ref, bufv_ref, bufi_ref
    n_new = kn_ref.shape[0]
    live = past + n_new
    for h in range(A_HEADS):
        sl = slice(h * A_HEAD_DIM, (h + 1) * A_HEAD_DIM)
        kb_scr[0:past, sl] = _head_rows(kc_ref, h, past, A_HEADS)[...].astype(BF16)
        vrow_scr[0:past, sl] = _head_rows(vc_ref, h, past, A_HEADS)[...]
        _head_rows(ck_ref, h, n_new, A_HEADS)[...] = kn_ref[:, sl]
        _head_rows(cv_ref, h, n_new, A_HEADS)[...] = vn_ref[:, sl]
    ci_ref[...] = mkn_ref[:, 0:IDX_DIM]
    kb_scr[past:live, :] = kn_ref[...].astype(BF16)
    kib_scr[0:past, :] = kic_ref[...].astype(BF16)
    kib_scr[past:live, :] = mkn_ref[:, 0:IDX_DIM].astype(BF16)
    vrow_scr[past:live, :] = vn_ref[...]
    if s_keys > live:
        kb_scr[live:, :] = jnp.zeros((s_keys - live, A_WIDTH), BF16)
        kib_scr[live:, :] = jnp.zeros((s_keys - live, IDX_DIM), BF16)
        vrow_scr[live:, :] = jnp.zeros((s_keys - live, A_WIDTH), F32)
    for i in range(s_keys // kb):
        vt_scr[:, i * kb:(i + 1) * kb] = vrow_scr[i * kb:(i + 1) * kb, :].T.astype(BF16)

    limit = jnp.full((1, qs), live, I32)
    _dsa_select_attend(q_ref, qi_ref, mq_ref, o_ref, kb_scr, vt_scr, kib_scr, w_scr, score_scr,
                       key_scr, mask_scr, hi_scr, lo_scr, n_keys=s_keys, qs=qs, topk=topk, limit=limit)


def _dsa_scratch(s_keys, qs):
    return [
        pltpu.VMEM((s_keys, A_WIDTH), BF16),
        pltpu.VMEM((A_WIDTH, s_keys), BF16),
        pltpu.VMEM((s_keys, IDX_DIM), BF16),
        pltpu.VMEM((LANES, qs), F32),
        pltpu.VMEM((s_keys, qs), F32),
        pltpu.VMEM((s_keys, qs), I32),
        pltpu.VMEM((s_keys, qs), F32),
        pltpu.VMEM((s_keys, qs), I16),
        pltpu.VMEM((s_keys, qs), I16),
    ]


def _dsa_prompt(proj, out_k, out_v, out_ki, layer, *, nb, tq, qs, kb):
    nsteps = tq // qs
    qcol = lambda w, off: pl.BlockSpec((qs, w), lambda b, j: (b * nsteps + j, off // w))
    kcol = lambda w, off: pl.BlockSpec((tq, w), lambda b, j: (b, off // w))
    cache = lambda rows, w: pl.BlockSpec((rows, w), lambda b, j: (layer * nb + b, 0))
    kern = functools.partial(_dsa_prompt_kernel, s_keys=tq, qs=qs, kb=kb, topk=min(DSA_TOPK, tq // 4))
    return pl.pallas_call(
        kern,
        grid=(nb, nsteps),
        in_specs=[qcol(A_WIDTH, COL_AQ), qcol(IQ_WIDTH, COL_IQ), qcol(LANES, COL_MISC),
                  kcol(A_WIDTH, COL_AK), kcol(A_WIDTH, COL_AV), kcol(LANES, COL_MISC), ANY, ANY, ANY],
        out_specs=[pl.BlockSpec((qs, A_WIDTH), lambda b, j: (b * nsteps + j, 0)),
                   cache(tq * A_HEADS, A_HEAD_DIM), cache(tq * A_HEADS, A_HEAD_DIM), cache(tq, IDX_DIM)],
        out_shape=[jax.ShapeDtypeStruct((proj.shape[0], A_WIDTH), F32),
                   jax.ShapeDtypeStruct(out_k.shape, F32), jax.ShapeDtypeStruct(out_v.shape, F32),
                   jax.ShapeDtypeStruct(out_ki.shape, F32)],
        input_output_aliases={6: 1, 7: 2, 8: 3},
        scratch_shapes=_dsa_scratch(tq, qs),
        compiler_params=_cparams("parallel", "arbitrary"),
        name="dsa_prompt",
    )(proj, proj, proj, proj, proj, proj, out_k, out_v, out_ki)


def _dsa_sample(proj, cache_k, cache_v, cache_ki, buf, out_k, out_v, out_ki, layer,
                *, nb, ts, past, row0, kb):
    rb0 = row0 // ts
    live = past + ts
    s_keys = -(-live // kb) * kb
    qcol = lambda w, off: pl.BlockSpec((ts, w), lambda b, j: (rb0 + b, off // w))
    per_seq = lambda rows, w: pl.BlockSpec((rows, w), lambda b, j: (layer * nb + b, 0))
    kern = functools.partial(_dsa_sample_kernel, s_keys=s_keys, qs=ts, kb=kb,
                             topk=min(DSA_TOPK, live // 4), past=past)
    return pl.pallas_call(
        kern,
        grid=(nb, 1),
        in_specs=[qcol(A_WIDTH, COL_AQ), qcol(IQ_WIDTH, COL_IQ), qcol(LANES, COL_MISC),
                  per_seq(past * A_HEADS, A_HEAD_DIM), per_seq(past * A_HEADS, A_HEAD_DIM),
                  per_seq(past, IDX_DIM),
                  qcol(A_WIDTH, COL_AK), qcol(A_WIDTH, COL_AV), qcol(LANES, COL_MISC),
                  ANY, ANY, ANY, ANY],
        out_specs=[pl.BlockSpec((ts, A_WIDTH), lambda b, j: (rb0 + b, 0)),
                   per_seq(ts * A_HEADS, A_HEAD_DIM), per_seq(ts * A_HEADS, A_HEAD_DIM),
                   per_seq(ts, IDX_DIM)],
        out_shape=[jax.ShapeDtypeStruct(buf.shape, F32), jax.ShapeDtypeStruct(out_k.shape, F32),
                   jax.ShapeDtypeStruct(out_v.shape, F32), jax.ShapeDtypeStruct(out_ki.shape, F32)],
        input_output_aliases={9: 0, 10: 1, 11: 2, 12: 3},
        scratch_shapes=_dsa_scratch(s_keys, ts) + [pltpu.VMEM((s_keys, A_WIDTH), F32)],
        compiler_params=_cparams("parallel", "arbitrary"),
        name="dsa_sample",
    )(proj, proj, proj, cache_k, cache_v, cache_ki, proj, proj, proj, buf, out_k, out_v, out_ki)


BAND_QS_PROMPT = 256


def _band_bias_kernel(tab_ref, o_ref, *, qs, w):
    layer = pl.program_id(0)
    wp = -(-(w + qs) // LANES) * LANES
    jj = lax.broadcasted_iota(I32, (8, wp), 1)
    d = jnp.where(jj < w, jj, jj - wp)
    idx = jnp.clip(B_WIN - d, -REL_CLIP, REL_CLIP) + REL_CLIP
    q = lax.broadcasted_iota(I32, (qs, w), 0)
    jx = lax.broadcasted_iota(I32, (qs, w), 1)
    back = (q >> 6) + B_PREV_CHUNKS - (jx >> 6)
    allowed = (back >= 0) & (back <= B_PREV_CHUNKS)
    for h in range(B_HEADS):
        row = layer * B_HEADS + h

        def body(v, acc):
            return jnp.where(idx == v, tab_ref[row, v], acc)

        base = lax.fori_loop(0, 2 * REL_CLIP + 1, body, jnp.zeros((8, wp), F32))
        rows = jnp.broadcast_to(base[0:1, :], (qs, wp))
        toeplitz = pltpu.roll(rows, 0, 1, stride=1, stride_axis=0)
        o_ref[0, h] = jnp.where(allowed, toeplitz[:, 0:w], NEG_INF)


def _band_bias(b_rel, qs):
    depth = b_rel.shape[0]
    w = qs + B_WIN
    return pl.pallas_call(
        functools.partial(_band_bias_kernel, qs=qs, w=w),
        grid=(depth,),
        in_specs=[pl.BlockSpec(memory_space=pltpu.SMEM)],
        out_specs=pl.BlockSpec((1, B_HEADS, qs, w), lambda l: (l, 0, 0, 0)),
        out_shape=jax.ShapeDtypeStruct((depth, B_HEADS, qs, w), F32),
        compiler_params=_cparams("arbitrary"),
        name=f"band_bias_{qs}",
    )(b_rel.reshape(depth * B_HEADS, 2 * REL_CLIP + 1))


def _band_attend(q_ref, bias_ref, o_ref, kp_scr, vp_scr, start, valid, qs):
    w = qs + B_WIN
    for h in range(B_HEADS):
        sl = slice(h * B_HEAD_DIM, (h + 1) * B_HEAD_DIM)
        qh = q_ref[:, sl].astype(BF16)
        lg = _nt_dot(qh, kp_scr[pl.ds(start, w), sl]) * (B_HEAD_DIM ** -0.5) + bias_ref[0, h]
        if valid is not None:
            lg = jnp.where(valid, lg, NEG_INF)
        m = jnp.max(lg, axis=-1, keepdims=True)
        e = jnp.exp(lg - m)
        den = jnp.sum(e, axis=-1, keepdims=True)
        o = jnp.dot(e.astype(BF16), vp_scr[pl.ds(start, w), sl], preferred_element_type=F32)
        o_ref[:, sl] = o / den


def _band_prompt_kernel(q_ref, k_ref, v_ref, bias_ref, bufk_ref, bufv_ref, o_ref, ck_ref, cv_ref,
                        kp_scr, vp_scr, *, s_keys, qs, nkeep):
    del bufk_ref, bufv_ref
    j = pl.program_id(1)

    @pl.when(j == 0)
    def _():
        kp_scr[0:B_WIN, :] = jnp.zeros((B_WIN, B_WIDTH), BF16)
        vp_scr[0:B_WIN, :] = jnp.zeros((B_WIN, B_WIDTH), BF16)
        kp_scr[B_WIN:B_WIN + s_keys, :] = k_ref[...].astype(BF16)
        vp_scr[B_WIN:B_WIN + s_keys, :] = v_ref[...].astype(BF16)
        for h in range(B_HEADS):
            sl = slice(h * B_HEAD_DIM, (h + 1) * B_HEAD_DIM)
            _head_rows(ck_ref, h, nkeep, B_HEADS)[...] = k_ref[s_keys - nkeep:, sl]
            _head_rows(cv_ref, h, nkeep, B_HEADS)[...] = v_ref[s_keys - nkeep:, sl]

    start = pl.multiple_of(j * qs, qs)
    valid = lax.broadcasted_iota(I32, (qs, qs + B_WIN), 1) >= B_WIN - j * qs
    _band_attend(q_ref, bias_ref, o_ref, kp_scr, vp_scr, start, valid, qs)


def _band_sample_kernel(q_ref, kc_ref, vc_ref, kn_ref, vn_ref, bias_ref, bufo_ref, bufk_ref, bufv_ref,
                        o_ref, ck_ref, cv_ref, kp_scr, vp_scr, *, qs):
    del bufo_ref, bufk_ref, bufv_ref
    for h in range(B_HEADS):
        sl = slice(h * B_HEAD_DIM, (h + 1) * B_HEAD_DIM)
        kp_scr[0:B_WIN, sl] = _head_rows(kc_ref, h, B_WIN, B_HEADS)[...].astype(BF16)
        vp_scr[0:B_WIN, sl] = _head_rows(vc_ref, h, B_WIN, B_HEADS)[...].astype(BF16)
        _head_rows(ck_ref, h, qs, B_HEADS)[...] = kn_ref[:, sl]
        _head_rows(cv_ref, h, qs, B_HEADS)[...] = vn_ref[:, sl]
    kp_scr[B_WIN:B_WIN + qs, :] = kn_ref[...].astype(BF16)
    vp_scr[B_WIN:B_WIN + qs, :] = vn_ref[...].astype(BF16)
    _band_attend(q_ref, bias_ref, o_ref, kp_scr, vp_scr, 0, None, qs)


def _band_prompt(proj, bias, out_k, out_v, layer, *, nb, tq, qs):
    nsteps = tq // qs
    w = qs + B_WIN
    nkeep = min(B_WIN, tq)
    kcol = lambda off: pl.BlockSpec((tq, B_WIDTH), lambda b, j: (b, off // B_WIDTH))
    cache = pl.BlockSpec((nkeep * B_HEADS, B_HEAD_DIM), lambda b, j: (layer * nb + b, 0))
    return pl.pallas_call(
        functools.partial(_band_prompt_kernel, s_keys=tq, qs=qs, nkeep=nkeep),
        grid=(nb, nsteps),
        in_specs=[
            pl.BlockSpec((qs, B_WIDTH), lambda b, j: (b * nsteps + j, COL_BQ // B_WIDTH)),
            kcol(COL_BK), kcol(COL_BV),
            pl.BlockSpec((1, B_HEADS, qs, w), lambda b, j: (layer, 0, 0, 0)),
            ANY, ANY,
        ],
        out_specs=[pl.BlockSpec((qs, B_WIDTH), lambda b, j: (b * nsteps + j, 0)), cache, cache],
        out_shape=[jax.ShapeDtypeStruct((proj.shape[0], B_WIDTH), F32),
                   jax.ShapeDtypeStruct(out_k.shape, F32), jax.ShapeDtypeStruct(out_v.shape, F32)],
        input_output_aliases={4: 1, 5: 2},
        scratch_shapes=[pltpu.VMEM((B_WIN + tq, B_WIDTH), BF16),
                        pltpu.VMEM((B_WIN + tq, B_WIDTH), BF16)],
        compiler_params=_cparams("parallel", "arbitrary"),
        name="band_prompt",
    )(proj, proj, proj, bias, out_k, out_v)


def _band_sample(proj, cache_k, cache_v, bias, buf, out_k, out_v, layer, *, nb, ts, row0):
    rb0 = row0 // ts
    w = ts + B_WIN
    qcol = lambda off: pl.BlockSpec((ts, B_WIDTH), lambda b, j: (rb0 + b, off // B_WIDTH))
    per_seq = lambda rows: pl.BlockSpec((rows * B_HEADS, B_HEAD_DIM), lambda b, j: (layer * nb + b, 0))
    return pl.pallas_call(
        functools.partial(_band_sample_kernel, qs=ts),
        grid=(nb, 1),
        in_specs=[qcol(COL_BQ), per_seq(B_WIN), per_seq(B_WIN), qcol(COL_BK), qcol(COL_BV),
                  pl.BlockSpec((1, B_HEADS, ts, w), lambda b, j: (layer, 0, 0, 0)),
                  ANY, ANY, ANY],
        out_specs=[pl.BlockSpec((ts, B_WIDTH), lambda b, j: (rb0 + b, 0)), per_seq(ts), per_seq(ts)],
        out_shape=[jax.ShapeDtypeStruct(buf.shape, F32), jax.ShapeDtypeStruct(out_k.shape, F32),
                   jax.ShapeDtypeStruct(out_v.shape, F32)],
        input_output_aliases={6: 0, 7: 1, 8: 2},
        scratch_shapes=[pltpu.VMEM((w, B_WIDTH), BF16), pltpu.VMEM((w, B_WIDTH), BF16)],
        compiler_params=_cparams("parallel", "arbitrary"),
        name="band_sample",
    )(proj, cache_k, cache_v, proj, proj, bias, buf, out_k, out_v)


XP_OFF = 8
SSD_CHUNKS_PER_STEP = 2
_HEAD_SPREAD = np.zeros((LANES, C_WIDTH), np.float32)
for _h in range(C_HEADS):
    _HEAD_SPREAD[MISC_DT + _h, _h * C_HEAD_DIM:(_h + 1) * C_HEAD_DIM] = 1.0


def _ssd_kernel(*refs, zero_init, nsteps, cps):
    if zero_init:
        (z_ref, xbc_ref, misc_ref, cw_ref, cbias_ref, prm_ref, gn_ref, ex_ref, dskip_ref,
         bufh_ref, bufc_ref, o_ref, hout_ref, cnew_ref, st_scr, xp_scr, y_scr) = refs
    else:
        (z_ref, xbc_ref, misc_ref, h0_ref, cb0_ref, cw_ref, cbias_ref, prm_ref, gn_ref, ex_ref, dskip_ref,
         bufo_ref, bufh_ref, bufc_ref, o_ref, hout_ref, cnew_ref, st_scr, xp_scr, y_scr) = refs
        del bufo_ref
    del bufh_ref, bufc_ref
    c = pl.program_id(1)
    tail = CONV_W - 1
    rep = C_HEADS // C_GROUPS
    gw = C_WIDTH // C_GROUPS

    @pl.when(c == 0)
    def _():
        if zero_init:
            st_scr[...] = jnp.zeros(st_scr.shape, F32)
            xp_scr[XP_OFF - tail:XP_OFF, :] = jnp.zeros((tail, CONV_DIM), F32)
        else:
            for h in range(C_HEADS):
                st_scr[h // rep, :, (h % rep) * C_HEAD_DIM:(h % rep + 1) * C_HEAD_DIM] = h0_ref[0, h].T
            xp_scr[XP_OFF - tail:XP_OFF, :] = cb0_ref[0]

    def chunk(rows):
        xp_scr[XP_OFF:XP_OFF + CHUNK, :] = xbc_ref[rows, :]
        y = cbias_ref[...] + xp_scr[XP_OFF - tail:XP_OFF - tail + CHUNK, :] * cw_ref[0:1, :]
        for t in range(1, CONV_W):
            y = y + xp_scr[XP_OFF - tail + t:XP_OFF - tail + t + CHUNK, :] * cw_ref[t:t + 1, :]
        new_tail = xp_scr[XP_OFF + CHUNK - tail:XP_OFF + CHUNK, :]
        xp_scr[XP_OFF - tail:XP_OFF, :] = new_tail
        cnew_ref[0] = new_tail
        xc = y * (1.0 / (1.0 + jnp.exp(-y)))
        xs = xc[:, 0:C_WIDTH]

        pre = misc_ref[rows, :] + prm_ref[0:1, :]
        dt = jnp.maximum(pre, 0.0) + jnp.log(1.0 + jnp.exp(-jnp.abs(pre)))
        ad = dt * (-jnp.exp(prm_ref[1:2, :]))
        li = lax.broadcasted_iota(I32, (CHUNK, CHUNK), 0)
        si = lax.broadcasted_iota(I32, (CHUNK, CHUNK), 1)
        causal = li >= si
        tri = causal.astype(BF16)
        acs = sum(jnp.dot(tri, part, preferred_element_type=F32) for part in _split3(ad))
        acs_t = acs.T
        ex = ex_ref[...]
        spread = lambda v: sum(jnp.dot(part, ex, preferred_element_type=F32) for part in _split3(v))
        dt_x = spread(dt)
        acs_x = spread(acs)
        acs_last_x = acs_x[CHUNK - 1:CHUNK, :]
        xd_all = xs * dt_x
        xd_b = xd_all.astype(BF16)
        xde_b = (xd_all * jnp.exp(acs_last_x - acs_x)).astype(BF16)
        grow_x = jnp.exp(acs_x)
        chunk_decay_x = jnp.exp(acs_last_x)
        head_of_lane = lax.broadcasted_iota(I32, (CHUNK, gw), 1) >> 6

        for g in range(C_GROUPS):
            gl = slice(g * gw, (g + 1) * gw)
            b_g = xc[:, C_WIDTH + g * C_STATE:C_WIDTH + (g + 1) * C_STATE]
            c_g = xc[:, C_WIDTH + (C_GROUPS + g) * C_STATE:C_WIDTH + (C_GROUPS + g + 1) * C_STATE]
            b_gt = b_g.T.astype(BF16)
            c_gb = c_g.astype(BF16)
            cb = jnp.dot(c_gb, b_gt, preferred_element_type=F32)
            st = st_scr[g]
            y_g = jnp.dot(c_gb, st.astype(BF16), preferred_element_type=F32) * grow_x[:, gl]
            xd_g = xd_b[:, gl]
            for r in range(rep):
                col = MISC_DT + g * rep + r
                seg = acs[:, col:col + 1] - acs_t[col:col + 1, :]
                decay_in = jnp.where(causal, jnp.exp(jnp.where(causal, seg, 0.0)), 0.0)
                t_r = jnp.dot((cb * decay_in).astype(BF16), xd_g, preferred_element_type=F32)
                y_g = y_g + jnp.where(head_of_lane == r, t_r, 0.0)
            y_scr[:, gl] = y_g + dskip_ref[:, gl] * xs[:, gl]
            st_scr[g] = st * chunk_decay_x[:, gl] + jnp.dot(b_gt, xde_b[:, gl], preferred_element_type=F32)

        z = z_ref[rows, :]
        gate = y_scr[...] * (z * (1.0 / (1.0 + jnp.exp(-z))))
        for g in range(C_GROUPS):
            gg = gate[:, g * gw:(g + 1) * gw]
            ms = jnp.mean(gg * gg, axis=-1, keepdims=True)
            o_ref[rows, g * gw:(g + 1) * gw] = gg * lax.rsqrt(ms + EPS) * gn_ref[:, g * gw:(g + 1) * gw]

    for i in range(cps):
        chunk(slice(i * CHUNK, (i + 1) * CHUNK))

    @pl.when(c == nsteps - 1)
    def _():
        for h in range(C_HEADS):
            hout_ref[0, h] = st_scr[h // rep, :, (h % rep) * C_HEAD_DIM:(h % rep + 1) * C_HEAD_DIM].T


def _ssd(proj, conv_w, conv_b, prm, gate_norm, out_h, out_c, layer, *, nb, nchunks, row0, state=None):
    cps = SSD_CHUNKS_PER_STEP if nchunks % SSD_CHUNKS_PER_STEP == 0 else 1
    nsteps = nchunks // cps
    rt = cps * CHUNK
    rb0 = row0 // rt
    row = lambda b, c: rb0 + b * nsteps + c
    const = lambda shape: pl.BlockSpec(shape, lambda b, c: tuple(0 for _ in shape))
    per_seq_h = pl.BlockSpec((1, C_HEADS, C_HEAD_DIM, C_STATE), lambda b, c: (layer * nb + b, 0, 0, 0))
    per_seq_c = pl.BlockSpec((1, CONV_W - 1, CONV_DIM), lambda b, c: (layer * nb + b, 0, 0))
    in_specs = [
        pl.BlockSpec((rt, C_WIDTH), lambda b, c: (row(b, c), COL_CZ // C_WIDTH)),
        pl.BlockSpec((rt, CONV_DIM), lambda b, c: (row(b, c), COL_CXBC // CONV_DIM)),
        pl.BlockSpec((rt, LANES), lambda b, c: (row(b, c), COL_MISC // LANES)),
    ]
    args = [proj, proj, proj]
    if state is not None:
        h0, cbuf0, buf = state
        in_specs += [per_seq_h, per_seq_c]
        args += [h0, cbuf0]
    in_specs += [const((CONV_W, CONV_DIM)), const((1, CONV_DIM)), const((8, LANES)), const((1, C_WIDTH)),
                 const((LANES, C_WIDTH)), const((1, C_WIDTH))]
    args += [conv_w, conv_b.reshape(1, CONV_DIM), prm[0], gate_norm.reshape(1, C_WIDTH),
             jnp.asarray(_HEAD_SPREAD, BF16), prm[1]]
    aliases = {}
    if state is not None:
        in_specs.append(ANY)
        args.append(buf)
        aliases[len(args) - 1] = 0
    in_specs += [ANY, ANY]
    args += [out_h, out_c]
    aliases[len(args) - 2] = 1
    aliases[len(args) - 1] = 2
    return pl.pallas_call(
        functools.partial(_ssd_kernel, zero_init=state is None, nsteps=nsteps, cps=cps),
        grid=(nb, nsteps),
        in_specs=in_specs,
        out_specs=[pl.BlockSpec((rt, C_WIDTH), lambda b, c: (row(b, c), 0)), per_seq_h, per_seq_c],
        out_shape=[
            jax.ShapeDtypeStruct((proj.shape[0], C_WIDTH), F32),
            jax.ShapeDtypeStruct(out_h.shape, F32),
            jax.ShapeDtypeStruct(out_c.shape, F32),
        ],
        input_output_aliases=aliases,
        scratch_shapes=[pltpu.VMEM((C_GROUPS, C_STATE, C_WIDTH // C_GROUPS), F32),
                        pltpu.VMEM((XP_OFF + CHUNK, CONV_DIM), F32),
                        pltpu.VMEM((CHUNK, C_WIDTH), F32)],
        compiler_params=_cparams("parallel", "arbitrary"),
        name="ssd_prompt" if state is None else "ssd_sample",
    )(*args)


def _outproj_kernel(x_ref, oa_ref, ob_ref, oc_ref, w_ref, o_ref):
    acc = x_ref[...]
    acc = acc + jnp.dot(oa_ref[...].astype(BF16), w_ref[0:A_WIDTH, :], preferred_element_type=F32)
    acc = acc + jnp.dot(ob_ref[...].astype(BF16), w_ref[A_WIDTH:A_WIDTH + B_WIDTH, :],
                        preferred_element_type=F32)
    acc = acc + jnp.dot(oc_ref[...].astype(BF16), w_ref[A_WIDTH + B_WIDTH:, :],
                        preferred_element_type=F32)
    o_ref[...] = acc


def _out_proj(x, oa, ob, oc, w_bf16):
    t, d = x.shape
    tm = _row_tile(t, 512)
    rows = lambda w: pl.BlockSpec((tm, w), lambda i: (i, 0))
    return pl.pallas_call(
        _outproj_kernel,
        grid=(t // tm,),
        in_specs=[rows(d), rows(A_WIDTH), rows(B_WIDTH), rows(C_WIDTH),
                  pl.BlockSpec((d, d), lambda i: (0, 0))],
        out_specs=rows(d),
        out_shape=jax.ShapeDtypeStruct((t, d), F32),
        compiler_params=_cparams("parallel"),
        name="out_proj",
    )(x, oa, ob, oc, w_bf16)


def _ffn_kernel(x_ref, nw_ref, wu_ref, wd_ref, o_ref, h_scr):
    @pl.when(pl.program_id(1) == 0)
    def _():
        x = x_ref[...]
        ms = jnp.mean(x * x, axis=-1, keepdims=True)
        h_scr[...] = (x * lax.rsqrt(ms + EPS) * nw_ref[...]).astype(BF16)
        o_ref[...] = x

    u = jnp.maximum(jnp.dot(h_scr[...], wu_ref[...], preferred_element_type=F32), 0.0)
    o_ref[...] += jnp.dot((u * u).astype(BF16), wd_ref[...], preferred_element_type=F32)


def _ffn(x, norm_w, wu_bf16, wd_bf16):
    t, d = x.shape
    ff = wu_bf16.shape[1]
    tm = _row_tile(t, 1024)
    tf = 512
    return pl.pallas_call(
        _ffn_kernel,
        grid=(t // tm, ff // tf),
        in_specs=[
            pl.BlockSpec((tm, d), lambda i, f: (i, 0)),
            pl.BlockSpec((1, d), lambda i, f: (0, 0)),
            pl.BlockSpec((d, tf), lambda i, f: (0, f)),
            pl.BlockSpec((tf, d), lambda i, f: (f, 0)),
        ],
        out_specs=pl.BlockSpec((tm, d), lambda i, f: (i, 0)),
        out_shape=jax.ShapeDtypeStruct((t, d), F32),
        scratch_shapes=[pltpu.VMEM((tm, d), BF16)],
        compiler_params=_cparams("parallel", "arbitrary"),
        name="ffn",
    )(x, norm_w.reshape(1, d), wu_bf16, wd_bf16)


def _norm_kernel(x_ref, nw_ref, o_ref):
    x = x_ref[...]
    ms = jnp.mean(x * x, axis=-1, keepdims=True)
    o_ref[...] = x * lax.rsqrt(ms + EPS) * nw_ref[...]


def _final_norm(x, norm_w, row0, nrows):
    d = x.shape[1]
    tm = _row_tile(nrows, 512)
    assert row0 % tm == 0
    return pl.pallas_call(
        _norm_kernel,
        grid=(nrows // tm,),
        in_specs=[pl.BlockSpec((tm, d), lambda i: (row0 // tm + i, 0)),
                  pl.BlockSpec((1, d), lambda i: (0, 0))],
        out_specs=pl.BlockSpec((tm, d), lambda i: (i, 0)),
        out_shape=jax.ShapeDtypeStruct((nrows, d), F32),
        compiler_params=_cparams("parallel"),
        name="final_norm",
    )(x, norm_w.reshape(1, d))


def _lane_row(v, off):
    return jnp.zeros((LANES,), F32).at[off:off + v.shape[0]].set(v)


def _trunk(x_prompt, x_sample, cache_a_k, cache_a_v, cache_a_kidx, cache_b_k, cache_b_v,
           state_ssm, state_conv, norm1, w_in, w_out, b_rel, conv_w, conv_b, dt_bias,
           a_log, d_skip, gate_norm, norm2, w_up, w_down, final_norm):
    bp, tp, d = x_prompt.shape
    bs, ts, _ = x_sample.shape
    depth = w_in.shape[0]
    past = cache_a_k.shape[2]
    nbuf = cache_b_k.shape[2]
    n_p = bp * tp
    n_s = bs * ts
    assert ts == CHUNK and tp % DSA_KB_PROMPT == 0 and tp % BAND_QS_PROMPT == 0 and nbuf == B_WIN
    qs_p = LANES
    nkeep = min(B_WIN, tp)

    x = jnp.concatenate([x_prompt.reshape(n_p, d), x_sample.reshape(n_s, d)], axis=0)

    pos = jnp.concatenate([jnp.tile(jnp.arange(tp), bp), jnp.tile(past + jnp.arange(ts), bs)])
    tab_a = _rope_tables(pos, A_HEAD_DIM)
    tab_i = _rope_tables(pos, IDX_DIM)
    bias_p = _band_bias(b_rel, BAND_QS_PROMPT)
    bias_s = _band_bias(b_rel, ts)
    w_in_r = _regroup_w_in(w_in)
    cak = cache_a_k.reshape(depth * bs * past * A_HEADS, A_HEAD_DIM)
    cav = cache_a_v.reshape(depth * bs * past * A_HEADS, A_HEAD_DIM)
    caki = cache_a_kidx.reshape(depth * bs * past, IDX_DIM)
    cbk = cache_b_k.reshape(depth * bs * nbuf * B_HEADS, B_HEAD_DIM)
    cbv = cache_b_v.reshape(depth * bs * nbuf * B_HEADS, B_HEAD_DIM)
    h0 = state_ssm.reshape(depth * bs, C_HEADS, C_HEAD_DIM, C_STATE)
    cbuf0 = state_conv.reshape(depth * bs, CONV_W - 1, CONV_DIM)

    p_ak = _uninit((depth * bp * tp * A_HEADS, A_HEAD_DIM))
    p_av = _uninit((depth * bp * tp * A_HEADS, A_HEAD_DIM))
    p_aki = _uninit((depth * bp * tp, IDX_DIM))
    p_bk = _uninit((depth * bp * nkeep * B_HEADS, B_HEAD_DIM))
    p_bv = _uninit((depth * bp * nkeep * B_HEADS, B_HEAD_DIM))
    p_h = _uninit((depth * bp, C_HEADS, C_HEAD_DIM, C_STATE))
    p_c = _uninit((depth * bp, CONV_W - 1, CONV_DIM))
    s_ak = _uninit((depth * bs * ts * A_HEADS, A_HEAD_DIM))
    s_av = _uninit((depth * bs * ts * A_HEADS, A_HEAD_DIM))
    s_aki = _uninit((depth * bs * ts, IDX_DIM))
    s_bk = _uninit((depth * bs * ts * B_HEADS, B_HEAD_DIM))
    s_bv = _uninit((depth * bs * ts * B_HEADS, B_HEAD_DIM))
    s_h = _uninit((depth * bs, C_HEADS, C_HEAD_DIM, C_STATE))
    s_c = _uninit((depth * bs, CONV_W - 1, CONV_DIM))

    for l in range(depth):
        proj = _in_proj(x, norm1[l], w_in_r, l, tab_a, tab_i)

        oa, p_ak, p_av, p_aki = _dsa_prompt(proj, p_ak, p_av, p_aki, l, nb=bp, tq=tp, qs=qs_p,
                                            kb=DSA_KB_PROMPT)
        oa, s_ak, s_av, s_aki = _dsa_sample(proj, cak, cav, caki, oa, s_ak, s_av, s_aki, l,
                                            nb=bs, ts=ts, past=past, row0=n_p, kb=DSA_KB_SAMPLE)
        ob, p_bk, p_bv = _band_prompt(proj, bias_p, p_bk, p_bv, l, nb=bp, tq=tp, qs=BAND_QS_PROMPT)
        ob, s_bk, s_bv = _band_sample(proj, cbk, cbv, bias_s, ob, s_bk, s_bv, l, nb=bs, ts=ts, row0=n_p)
        rows = jnp.zeros((8, LANES), F32)
        rows = rows.at[0].set(_lane_row(dt_bias[l], MISC_DT)).at[1].set(_lane_row(a_log[l], MISC_DT))
        prm = (rows, jnp.repeat(d_skip[l], C_HEAD_DIM).reshape(1, C_WIDTH))
        oc, p_h, p_c = _ssd(proj, conv_w[l], conv_b[l], prm, gate_norm[l], p_h, p_c, l,
                            nb=bp, nchunks=tp // CHUNK, row0=0)
        oc, s_h, s_c = _ssd(proj, conv_w[l], conv_b[l], prm, gate_norm[l], s_h, s_c, l,
                            nb=bs, nchunks=1, row0=n_p, state=(h0, cbuf0, oc))

        x = _out_proj(x, oa, ob, oc, w_out[l].astype(BF16))
        x = _ffn(x, norm2[l], w_up[l].astype(BF16), w_down[l].astype(BF16))

    y_prompt = _final_norm(x, final_norm, 0, n_p).reshape(bp, tp, d)
    y_sample = _final_norm(x, final_norm, n_p, n_s).reshape(bs, ts, d)
    return (y_prompt, y_sample,
            p_ak.reshape(depth, bp, tp, A_HEADS, A_HEAD_DIM),
            p_av.reshape(depth, bp, tp, A_HEADS, A_HEAD_DIM),
            p_aki.reshape(depth, bp, tp, IDX_DIM),
            p_bk.reshape(depth, bp, nkeep, B_HEADS, B_HEAD_DIM),
            p_bv.reshape(depth, bp, nkeep, B_HEADS, B_HEAD_DIM),
            p_h.reshape(depth, bp, C_HEADS, C_HEAD_DIM, C_STATE),
            p_c.reshape(depth, bp, CONV_W - 1, CONV_DIM),
            s_ak.reshape(depth, bs, ts, A_HEADS, A_HEAD_DIM),
            s_av.reshape(depth, bs, ts, A_HEADS, A_HEAD_DIM),
            s_aki.reshape(depth, bs, ts, IDX_DIM),
            s_bk.reshape(depth, bs, ts, B_HEADS, B_HEAD_DIM),
            s_bv.reshape(depth, bs, ts, B_HEADS, B_HEAD_DIM),
            s_h.reshape(depth, bs, C_HEADS, C_HEAD_DIM, C_STATE),
            s_c.reshape(depth, bs, CONV_W - 1, CONV_DIM))


def kernel(x_prompt, x_sample, cache_a_k, cache_a_v, cache_a_kidx, cache_b_k, cache_b_v, state_ssm, state_conv, norm1, w_in, w_out, b_rel, conv_w, conv_b, dt_bias, a_log, d_skip, gate_norm, norm2, w_up, w_down, final_norm):
    return _trunk(x_prompt, x_sample, cache_a_k, cache_a_v, cache_a_kidx, cache_b_k, cache_b_v,
                  state_ssm, state_conv, norm1, w_in, w_out, b_rel, conv_w, conv_b, dt_bias,
                  a_log, d_skip, gate_norm, norm2, w_up, w_down, final_norm)
```

```python
import functools

import jax
import jax.numpy as jnp
import numpy as np
from jax import lax
from jax.experimental import pallas as pl
from jax.experimental.pallas import tpu as pltpu

F32 = jnp.float32
BF16 = jnp.bfloat16
I32 = jnp.int32
I16 = jnp.int16

D_MODEL = 2048
CHUNK = 64
A_HEADS = 4
A_HEAD_DIM = 128
A_WIDTH = A_HEADS * A_HEAD_DIM
IDX_HEADS = 16
IDX_DIM = 64
IQ_WIDTH = IDX_HEADS * IDX_DIM
DSA_TOPK = 256
B_HEADS = 4
B_HEAD_DIM = 128
B_WIDTH = B_HEADS * B_HEAD_DIM
B_PREV_CHUNKS = 8
B_WIN = B_PREV_CHUNKS * CHUNK
REL_CLIP = 128
C_WIDTH = 1024
C_HEAD_DIM = 64
C_HEADS = C_WIDTH // C_HEAD_DIM
C_GROUPS = 4
C_STATE = 128
CONV_W = 4
CONV_DIM = C_WIDTH + 2 * C_GROUPS * C_STATE
D_FF = 4 * D_MODEL
ROPE_THETA = 500000.0
EPS = 1e-5

LANES = 128
INT_MIN = -(2 ** 31)
NEG_INF = float("-inf")

COL_AQ = 0
COL_AK = 512
COL_IQ = 1024
COL_AV = 2048
COL_BQ = 2560
COL_BK = 3072
COL_BV = 3584
COL_CXBC = 4096
COL_CZ = 6144
COL_MISC = 7168
MISC_IW = 64
MISC_DT = 80
IN_COLS_PAD = 7680
IN_TN = 1280
SRC_AQ, SRC_AV, SRC_IQ, SRC_IK, SRC_BQ, SRC_CZ, SRC_CXBC, SRC_DT, SRC_END = (
    0, 1024, 1536, 2560, 2640, 4176, 5200, 7248, 7264)

VMEM_LIMIT = 56 * 1024 * 1024
ANY = pl.BlockSpec(memory_space=pl.ANY)


def _cparams(*sem):
    return pltpu.CompilerParams(dimension_semantics=sem, vmem_limit_bytes=VMEM_LIMIT)


def _row_tile(t, cap):
    tm = cap
    while t % tm:
        tm //= 2
    return tm


def _nt_dot(a, b):
    return lax.dot_general(a, b, (((1,), (1,)), ((), ())), preferred_element_type=F32)


def _split3(x):
    hi = x.astype(BF16)
    r1 = x - hi.astype(F32)
    mid = r1.astype(BF16)
    lo = (r1 - mid.astype(F32)).astype(BF16)
    return hi, mid, lo


def _head_rows(ref, h, n, heads):
    return ref.at[pl.ds(h, n, stride=heads), :]


def _uninit_kernel(o_ref):
    del o_ref


def _uninit(shape, dtype=F32):
    return pl.pallas_call(_uninit_kernel, out_specs=ANY, out_shape=jax.ShapeDtypeStruct(shape, dtype),
                          name="alloc")()


def _regroup_kernel(w_ref, o_ref):
    def put(dst, src, width):
        o_ref[0, :, dst:dst + width] = w_ref[0, :, src:src + width].astype(BF16)

    rows = w_ref.shape[1]
    put(COL_AQ, SRC_AQ, 2 * A_WIDTH)
    put(COL_IQ, SRC_IQ, IQ_WIDTH)
    put(COL_AV, SRC_AV, A_WIDTH)
    put(COL_BQ, SRC_BQ, 3 * B_WIDTH)
    put(COL_CXBC, SRC_CXBC, CONV_DIM)
    put(COL_CZ, SRC_CZ, C_WIDTH)
    o_ref[0, :, COL_MISC:] = jnp.zeros((rows, IN_COLS_PAD - COL_MISC), BF16)
    put(COL_MISC, SRC_IK, IDX_DIM + IDX_HEADS)
    put(COL_MISC + MISC_DT, SRC_DT, C_HEADS)


def _regroup_w_in(w_in):
    depth, d, n = w_in.shape
    assert n == SRC_END
    tk = 256
    return pl.pallas_call(
        _regroup_kernel,
        grid=(depth, d // tk),
        in_specs=[pl.BlockSpec((1, tk, n), lambda l, i: (l, i, 0))],
        out_specs=pl.BlockSpec((1, tk, IN_COLS_PAD), lambda l, i: (l, i, 0)),
        out_shape=jax.ShapeDtypeStruct((depth, d, IN_COLS_PAD), BF16),
        compiler_params=_cparams("parallel", "parallel"),
        name="regroup_w_in",
    )(w_in)


def _rope_block(x, c, sm, sp, half):
    return x * c + pltpu.roll(x, LANES - half, 1) * sm + pltpu.roll(x, half, 1) * sp


def _inproj_kernel(x_ref, nw_ref, w_ref, ta_ref, ti_ref, o_ref, h_scr):
    j = pl.program_id(1)

    @pl.when(j == 0)
    def _():
        x = x_ref[...]
        ms = jnp.mean(x * x, axis=-1, keepdims=True)
        h_scr[...] = (x * lax.rsqrt(ms + EPS) * nw_ref[...]).astype(BF16)

    o_ref[...] = jnp.dot(h_scr[...], w_ref[0], preferred_element_type=F32)

    def rotate(tile):
        for blk in range(IN_TN // LANES):
            col = tile * IN_TN + blk * LANES
            sl = slice(blk * LANES, (blk + 1) * LANES)
            if col < COL_IQ:
                o_ref[:, sl] = _rope_block(o_ref[:, sl], ta_ref[0], ta_ref[1], ta_ref[2],
                                           A_HEAD_DIM // 8)
            elif col < COL_AV:
                o_ref[:, sl] = _rope_block(o_ref[:, sl], ti_ref[0], ti_ref[1], ti_ref[2], IDX_DIM // 8)
            elif col == COL_MISC:
                m = o_ref[:, sl]
                lane = lax.broadcasted_iota(I32, m.shape, 1)
                rot = _rope_block(m, ti_ref[0], ti_ref[1], ti_ref[2], IDX_DIM // 8)
                o_ref[:, sl] = jnp.where(lane < IDX_DIM, rot, m)

    for tile in range(IN_COLS_PAD // IN_TN):
        lo, hi = tile * IN_TN, (tile + 1) * IN_TN
        if lo < COL_AV or lo <= COL_MISC < hi:
            pl.when(j == tile)(functools.partial(rotate, tile))


def _in_proj(x, norm_w, w_all, layer, tab_a, tab_i):
    t, d = x.shape
    n = w_all.shape[2]
    tm = _row_tile(t, 1024)
    tab = pl.BlockSpec((3, tm, LANES), lambda i, j: (0, i, 0))
    return pl.pallas_call(
        _inproj_kernel,
        grid=(t // tm, n // IN_TN),
        in_specs=[
            pl.BlockSpec((tm, d), lambda i, j: (i, 0)),
            pl.BlockSpec((1, d), lambda i, j: (0, 0)),
            pl.BlockSpec((1, d, IN_TN), lambda i, j: (layer, 0, j)),
            tab, tab,
        ],
        out_specs=pl.BlockSpec((tm, IN_TN), lambda i, j: (i, j)),
        out_shape=jax.ShapeDtypeStruct((t, n), F32),
        scratch_shapes=[pltpu.VMEM((tm, d), BF16)],
        compiler_params=_cparams("parallel", "arbitrary"),
        name="in_proj",
    )(x, norm_w.reshape(1, d), w_all, tab_a, tab_i)


def _rope_tables(pos, head_dim):
    rot = head_dim // 4
    half = rot // 2
    inv = ROPE_THETA ** (-jnp.arange(half, dtype=F32) * 2.0 / rot)
    ang = pos.astype(F32)[:, None] * inv[None, :]
    cos, sin = jnp.cos(ang), jnp.sin(ang)
    n = pos.shape[0]
    rest = head_dim - rot
    c = jnp.concatenate([cos, cos, jnp.ones((n, rest), F32)], axis=1)
    sm = jnp.concatenate([-sin, jnp.zeros((n, half + rest), F32)], axis=1)
    sp = jnp.concatenate([jnp.zeros((n, half), F32), sin, jnp.zeros((n, rest), F32)], axis=1)
    tab = jnp.stack([c, sm, sp], axis=0)
    return jnp.tile(tab, (1, 1, LANES // head_dim))


TIE_BLOCK = 64
SUB = 128
DSA_KB_PROMPT = 256
DSA_KB_SAMPLE = 384
ATT_BLOCK = 256


def _tree_sum(parts):
    while len(parts) > 1:
        parts = [parts[i] + parts[i + 1] for i in range(0, len(parts) - 1, 2)] + (
            parts[-1:] if len(parts) % 2 else [])
    return parts[0]


def _count_keys(key_scr, n_keys, qs, pred):
    dt = key_scr.dtype
    pack = 8 * (4 // dt.itemsize)
    blocks = []
    for u in range(n_keys // SUB):
        hit = jnp.where(pred(key_scr[u * SUB:(u + 1) * SUB, :]), jnp.ones((), dt), jnp.zeros((), dt))
        blocks.append(_tree_sum([hit[i * pack:(i + 1) * pack, :] for i in range(SUB // pack)]))
    return jnp.sum(_tree_sum(blocks).astype(I32), axis=0, keepdims=True)


def _kth_largest(scr, n_keys, qs, k, nbits):
    dt = scr.dtype
    count = functools.partial(_count_keys, scr, n_keys, qs)
    lowest = -(2 ** (nbits - 1))
    ans0 = jnp.where(count(lambda x: x >= jnp.zeros((1, qs), dt)) >= k, 0, lowest).astype(I32)

    def bit_step(i, ans):
        cand = ans | (jnp.int32(1) << (nbits - 2 - i))
        return jnp.where(count(lambda x: x >= cand.astype(dt)) >= k, cand, ans)

    return lax.fori_loop(0, nbits - 1, bit_step, ans0)


def _dsa_select_attend(q_ref, qi_ref, mq_ref, o_ref, kb_scr, vt_scr, kib_scr, w_scr, score_scr,
                       key_scr, mask_scr, hi_scr, lo_scr, *, n_keys, qs, topk, limit):
    w_scr[...] = mq_ref[...].T * (IDX_HEADS ** -0.5 * IDX_DIM ** -0.5)

    kib = kib_scr[0:n_keys, :]
    for h in range(IDX_HEADS):
        qh = qi_ref[:, h * IDX_DIM:(h + 1) * IDX_DIM].astype(BF16)
        term = jnp.maximum(_nt_dot(kib, qh), 0.0) * w_scr[MISC_IW + h:MISC_IW + h + 1, :]
        if h == 0:
            score_scr[0:n_keys, :] = term
        else:
            score_scr[0:n_keys, :] += term
    sc = score_scr[0:n_keys, :]
    sc = jnp.where(sc == 0.0, 0.0, sc)
    bits = lax.bitcast_convert_type(sc, I32)
    key = bits ^ ((bits >> 31) & 0x7FFFFFFF)
    s_idx = lax.broadcasted_iota(I32, (n_keys, qs), 0)
    key = jnp.where(s_idx < limit, key, INT_MIN)
    key_scr[0:n_keys, :] = key
    hi_scr[0:n_keys, :] = (key >> 16).astype(I16)

    thr_hi = _kth_largest(hi_scr, n_keys, qs, topk, 16)
    above = _count_keys(hi_scr, n_keys, qs, lambda x: x > thr_hi.astype(I16))
    key = key_scr[0:n_keys, :]
    lo = ((key & 0xFFFF) - 2 ** 15).astype(I16)
    lo_scr[0:n_keys, :] = jnp.where((key >> 16) == thr_hi, lo, jnp.int16(-(2 ** 15)))
    thr_lo = _kth_largest(lo_scr, n_keys, qs, topk - above, 16)
    thr = (thr_hi << 16) | (thr_lo + 2 ** 15)

    cnt_gt = _count_keys(key_scr, n_keys, qs, lambda x: x > thr)
    cnt_eq = _count_keys(key_scr, n_keys, qs, lambda x: x == thr)
    room = topk - cnt_gt
    live = thr > INT_MIN
    all_ties = (cnt_eq <= room) & live
    key = key_scr[0:n_keys, :]
    mask_scr[0:n_keys, :] = jnp.where((key > thr) | ((key == thr) & all_ties), 0.0, NEG_INF)
    need_ties = jnp.max(((cnt_eq > room) & live).astype(I32)) > 0

    @pl.when(need_ties)
    def _():
        tri = (lax.broadcasted_iota(I32, (TIE_BLOCK, TIE_BLOCK), 0)
               >= lax.broadcasted_iota(I32, (TIE_BLOCK, TIE_BLOCK), 1)).astype(BF16)

        def blk(b, carry):
            rows = pl.ds(pl.multiple_of(b * TIE_BLOCK, TIE_BLOCK), TIE_BLOCK)
            kblk = key_scr[rows, :]
            eqb = (kblk == thr) & live
            eqf = eqb.astype(F32)
            prefix = jnp.dot(tri, eqf.astype(BF16), preferred_element_type=F32) + carry
            keep = (kblk > thr) | (eqb & (prefix <= room.astype(F32)))
            mask_scr[rows, :] = jnp.where(keep, 0.0, NEG_INF)
            return carry + jnp.sum(eqf, axis=0, keepdims=True)

        lax.fori_loop(0, n_keys // TIE_BLOCK, blk, jnp.zeros((1, qs), F32))

    ab = ATT_BLOCK if n_keys % ATT_BLOCK == 0 else SUB
    for h in range(A_HEADS):
        sl = slice(h * A_HEAD_DIM, (h + 1) * A_HEAD_DIM)
        qh = q_ref[:, sl].astype(BF16)
        parts = []
        for b in range(n_keys // ab):
            rows = slice(b * ab, (b + 1) * ab)
            lg = _nt_dot(kb_scr[rows, sl], qh) * (A_HEAD_DIM ** -0.5) + mask_scr[rows, :]
            m_b = jnp.max(lg, axis=0, keepdims=True)
            e = jnp.exp(lg - jnp.where(m_b == NEG_INF, 0.0, m_b))
            den_b = jnp.sum(e, axis=0, keepdims=True)
            o_b = jnp.dot(vt_scr[sl, rows], e.astype(BF16), preferred_element_type=F32)
            parts.append((m_b, den_b, o_b))
        m = functools.reduce(jnp.maximum, [p[0] for p in parts])
        den = jnp.zeros((1, qs), F32)
        o_t = jnp.zeros((A_HEAD_DIM, qs), F32)
        for m_b, den_b, o_b in parts:
            wgt = jnp.exp(m_b - m)
            den = den + wgt * den_b
            o_t = o_t + wgt * o_b
        o_ref[:, sl] = (o_t / den).T


def _dsa_prompt_kernel(q_ref, qi_ref, mq_ref, k_ref, v_ref, mk_ref, bufk_ref, bufv_ref, bufi_ref,
                       o_ref, ck_ref, cv_ref, ci_ref,
                       kb_scr, vt_scr, kib_scr, w_scr, score_scr, key_scr, mask_scr, hi_scr, lo_scr,
                       *, s_keys, qs, kb, topk):
    del bufk_ref, bufv_ref, bufi_ref
    j = pl.program_id(1)

    @pl.when(j == 0)
    def _():
        kb_scr[...] = k_ref[...].astype(BF16)
        kib_scr[...] = mk_ref[:, 0:IDX_DIM].astype(BF16)
        for i in range(s_keys // kb):
            vt_scr[:, i * kb:(i + 1) * kb] = v_ref[i * kb:(i + 1) * kb, :].T.astype(BF16)
        ci_ref[...] = mk_ref[:, 0:IDX_DIM]
        for h in range(A_HEADS):
            sl = slice(h * A_HEAD_DIM, (h + 1) * A_HEAD_DIM)
            _head_rows(ck_ref, h, s_keys, A_HEADS)[...] = k_ref[:, sl]
            _head_rows(cv_ref, h, s_keys, A_HEADS)[...] = v_ref[:, sl]

    q_pos = j * qs + lax.broadcasted_iota(I32, (1, qs), 1)
    limit = ((q_pos >> 6) + 1) * CHUNK
    nsb = (j * qs + qs + kb - 1) // kb
    for n in range(1, s_keys // kb + 1):
        pl.when(nsb == n)(functools.partial(
            _dsa_select_attend, q_ref, qi_ref, mq_ref, o_ref, kb_scr, vt_scr, kib_scr, w_scr, score_scr,
            key_scr, mask_scr, hi_scr, lo_scr, n_keys=n * kb, qs=qs, topk=topk, limit=limit))


def _dsa_sample_kernel(q_ref, qi_ref, mq_ref, kc_ref, vc_ref, kic_ref, kn_ref, vn_ref, mkn_ref,
                       bufo_ref, bufk_ref, bufv_ref, bufi_ref,
                       o_ref, ck_ref, cv_ref, ci_ref,
                       kb_scr, vt_scr, kib_scr, w_scr, score_scr, key_scr, mask_scr, hi_scr, lo_scr,
                       vrow_scr, *, s_keys, qs, kb, topk, past):
    del bufo_ref, bufk_ref, bufv_ref, bufi_ref
    n_new = kn_ref.shape[0]
    live = past + n_new
    for h in range(A_HEADS):
        sl = slice(h * A_HEAD_DIM, (h + 1) * A_HEAD_DIM)
        kb_scr[0:past, sl] = _head_rows(kc_ref, h, past, A_HEADS)[...].astype(BF16)
        vrow_scr[0:past, sl] = _head_rows(vc_ref, h, past, A_HEADS)[...]
        _head_rows(ck_ref, h, n_new, A_HEADS)[...] = kn_ref[:, sl]
        _head_rows(cv_ref, h, n_new, A_HEADS)[...] = vn_ref[:, sl]
    ci_ref[...] = mkn_ref[:, 0:IDX_DIM]
    kb_scr[past:live, :] = kn_ref[...].astype(BF16)
    kib_scr[0:past, :] = kic_ref[...].astype(BF16)
    kib_scr[past:live, :] = mkn_ref[:, 0:IDX_DIM].astype(BF16)
    vrow_scr[past:live, :] = vn_ref[...]
    if s_keys > live:
        kb_scr[live:, :] = jnp.zeros((s_keys - live, A_WIDTH), BF16)
        kib_scr[live:, :] = jnp.zeros((s_keys - live, IDX_DIM), BF16)
        vrow_scr[live:, :] = jnp.zeros((s_keys - live, A_WIDTH), F32)
    for i in range(s_keys // kb):
        vt_scr[:, i * kb:(i + 1) * kb] = vrow_scr[i * kb:(i + 1) * kb, :].T.astype(BF16)

    limit = jnp.full((1, qs), live, I32)
    _dsa_select_attend(q_ref, qi_ref, mq_ref, o_ref, kb_scr, vt_scr, kib_scr, w_scr, score_scr,
                       key_scr, mask_scr, hi_scr, lo_scr, n_keys=s_keys, qs=qs, topk=topk, limit=limit)


def _dsa_scratch(s_keys, qs):
    return [
        pltpu.VMEM((s_keys, A_WIDTH), BF16),
        pltpu.VMEM((A_WIDTH, s_keys), BF16),
        pltpu.VMEM((s_keys, IDX_DIM), BF16),
        pltpu.VMEM((LANES, qs), F32),
        pltpu.VMEM((s_keys, qs), F32),
        pltpu.VMEM((s_keys, qs), I32),
        pltpu.VMEM((s_keys, qs), F32),
        pltpu.VMEM((s_keys, qs), I16),
        pltpu.VMEM((s_keys, qs), I16),
    ]


def _dsa_prompt(proj, out_k, out_v, out_ki, layer, *, nb, tq, qs, kb):
    nsteps = tq // qs
    qcol = lambda w, off: pl.BlockSpec((qs, w), lambda b, j: (b * nsteps + j, off // w))
    kcol = lambda w, off: pl.BlockSpec((tq, w), lambda b, j: (b, off // w))
    cache = lambda rows, w: pl.BlockSpec((rows, w), lambda b, j: (layer * nb + b, 0))
    kern = functools.partial(_dsa_prompt_kernel, s_keys=tq, qs=qs, kb=kb, topk=min(DSA_TOPK, tq // 4))
    return pl.pallas_call(
        kern,
        grid=(nb, nsteps),
        in_specs=[qcol(A_WIDTH, COL_AQ), qcol(IQ_WIDTH, COL_IQ), qcol(LANES, COL_MISC),
                  kcol(A_WIDTH, COL_AK), kcol(A_WIDTH, COL_AV), kcol(LANES, COL_MISC), ANY, ANY, ANY],
        out_specs=[pl.BlockSpec((qs, A_WIDTH), lambda b, j: (b * nsteps + j, 0)),
                   cache(tq * A_HEADS, A_HEAD_DIM), cache(tq * A_HEADS, A_HEAD_DIM), cache(tq, IDX_DIM)],
        out_shape=[jax.ShapeDtypeStruct((proj.shape[0], A_WIDTH), F32),
                   jax.ShapeDtypeStruct(out_k.shape, F32), jax.ShapeDtypeStruct(out_v.shape, F32),
                   jax.ShapeDtypeStruct(out_ki.shape, F32)],
        input_output_aliases={6: 1, 7: 2, 8: 3},
        scratch_shapes=_dsa_scratch(tq, qs),
        compiler_params=_cparams("parallel", "arbitrary"),
        name="dsa_prompt",
    )(proj, proj, proj, proj, proj, proj, out_k, out_v, out_ki)


def _dsa_sample(proj, cache_k, cache_v, cache_ki, buf, out_k, out_v, out_ki, layer,
                *, nb, ts, past, row0, kb):
    rb0 = row0 // ts
    live = past + ts
    s_keys = -(-live // kb) * kb
    qcol = lambda w, off: pl.BlockSpec((ts, w), lambda b, j: (rb0 + b, off // w))
    per_seq = lambda rows, w: pl.BlockSpec((rows, w), lambda b, j: (layer * nb + b, 0))
    kern = functools.partial(_dsa_sample_kernel, s_keys=s_keys, qs=ts, kb=kb,
                             topk=min(DSA_TOPK, live // 4), past=past)
    return pl.pallas_call(
        kern,
        grid=(nb, 1),
        in_specs=[qcol(A_WIDTH, COL_AQ), qcol(IQ_WIDTH, COL_IQ), qcol(LANES, COL_MISC),
                  per_seq(past * A_HEADS, A_HEAD_DIM), per_seq(past * A_HEADS, A_HEAD_DIM),
                  per_seq(past, IDX_DIM),
                  qcol(A_WIDTH, COL_AK), qcol(A_WIDTH, COL_AV), qcol(LANES, COL_MISC),
                  ANY, ANY, ANY, ANY],
        out_specs=[pl.BlockSpec((ts, A_WIDTH), lambda b, j: (rb0 + b, 0)),
                   per_seq(ts * A_HEADS, A_HEAD_DIM), per_seq(ts * A_HEADS, A_HEAD_DIM),
                   per_seq(ts, IDX_DIM)],
        out_shape=[jax.ShapeDtypeStruct(buf.shape, F32), jax.ShapeDtypeStruct(out_k.shape, F32),
                   jax.ShapeDtypeStruct(out_v.shape, F32), jax.ShapeDtypeStruct(out_ki.shape, F32)],
        input_output_aliases={9: 0, 10: 1, 11: 2, 12: 3},
        scratch_shapes=_dsa_scratch(s_keys, ts) + [pltpu.VMEM((s_keys, A_WIDTH), F32)],
        compiler_params=_cparams("parallel", "arbitrary"),
        name="dsa_sample",
    )(proj, proj, proj, cache_k, cache_v, cache_ki, proj, proj, proj, buf, out_k, out_v, out_ki)


BAND_QS_PROMPT = 256


def _band_bias_kernel(tab_ref, o_ref, *, qs, w):
    layer = pl.program_id(0)
    wp = -(-(w + qs) // LANES) * LANES
    jj = lax.broadcasted_iota(I32, (8, wp), 1)
    d = jnp.where(jj < w, jj, jj - wp)
    idx = jnp.clip(B_WIN - d, -REL_CLIP, REL_CLIP) + REL_CLIP
    q = lax.broadcasted_iota(I32, (qs, w), 0)
    jx = lax.broadcasted_iota(I32, (qs, w), 1)
    back = (q >> 6) + B_PREV_CHUNKS - (jx >> 6)
    allowed = (back >= 0) & (back <= B_PREV_CHUNKS)
    for h in range(B_HEADS):
        row = layer * B_HEADS + h

        def body(v, acc):
            return jnp.where(idx == v, tab_ref[row, v], acc)

        base = lax.fori_loop(0, 2 * REL_CLIP + 1, body, jnp.zeros((8, wp), F32))
        rows = jnp.broadcast_to(base[0:1, :], (qs, wp))
        toeplitz = pltpu.roll(rows, 0, 1, stride=1, stride_axis=0)
        o_ref[0, h] = jnp.where(allowed, toeplitz[:, 0:w], NEG_INF)


def _band_bias(b_rel, qs):
    depth = b_rel.shape[0]
    w = qs + B_WIN
    return pl.pallas_call(
        functools.partial(_band_bias_kernel, qs=qs, w=w),
        grid=(depth,),
        in_specs=[pl.BlockSpec(memory_space=pltpu.SMEM)],
        out_specs=pl.BlockSpec((1, B_HEADS, qs, w), lambda l: (l, 0, 0, 0)),
        out_shape=jax.ShapeDtypeStruct((depth, B_HEADS, qs, w), F32),
        compiler_params=_cparams("arbitrary"),
        name=f"band_bias_{qs}",
    )(b_rel.reshape(depth * B_HEADS, 2 * REL_CLIP + 1))


def _band_attend(q_ref, bias_ref, o_ref, kp_scr, vp_scr, start, valid, qs):
    w = qs + B_WIN
    for h in range(B_HEADS):
        sl = slice(h * B_HEAD_DIM, (h + 1) * B_HEAD_DIM)
        qh = q_ref[:, sl].astype(BF16)
        lg = _nt_dot(qh, kp_scr[pl.ds(start, w), sl]) * (B_HEAD_DIM ** -0.5) + bias_ref[0, h]
        if valid is not None:
            lg = jnp.where(valid, lg, NEG_INF)
        m = jnp.max(lg, axis=-1, keepdims=True)
        e = jnp.exp(lg - m)
        den = jnp.sum(e, axis=-1, keepdims=True)
        o = jnp.dot(e.astype(BF16), vp_scr[pl.ds(start, w), sl], preferred_element_type=F32)
        o_ref[:, sl] = o / den


def _band_prompt_kernel(q_ref, k_ref, v_ref, bias_ref, bufk_ref, bufv_ref, o_ref, ck_ref, cv_ref,
                        kp_scr, vp_scr, *, s_keys, qs, nkeep):
    del bufk_ref, bufv_ref
    j = pl.program_id(1)

    @pl.when(j == 0)
    def _():
        kp_scr[0:B_WIN, :] = jnp.zeros((B_WIN, B_WIDTH), BF16)
        vp_scr[0:B_WIN, :] = jnp.zeros((B_WIN, B_WIDTH), BF16)
        kp_scr[B_WIN:B_WIN + s_keys, :] = k_ref[...].astype(BF16)
        vp_scr[B_WIN:B_WIN + s_keys, :] = v_ref[...].astype(BF16)
        for h in range(B_HEADS):
            sl = slice(h * B_HEAD_DIM, (h + 1) * B_HEAD_DIM)
            _head_rows(ck_ref, h, nkeep, B_HEADS)[...] = k_ref[s_keys - nkeep:, sl]
            _head_rows(cv_ref, h, nkeep, B_HEADS)[...] = v_ref[s_keys - nkeep:, sl]

    start = pl.multiple_of(j * qs, qs)
    valid = lax.broadcasted_iota(I32, (qs, qs + B_WIN), 1) >= B_WIN - j * qs
    _band_attend(q_ref, bias_ref, o_ref, kp_scr, vp_scr, start, valid, qs)


def _band_sample_kernel(q_ref, kc_ref, vc_ref, kn_ref, vn_ref, bias_ref, bufo_ref, bufk_ref, bufv_ref,
                        o_ref, ck_ref, cv_ref, kp_scr, vp_scr, *, qs):
    del bufo_ref, bufk_ref, bufv_ref
    for h in range(B_HEADS):
        sl = slice(h * B_HEAD_DIM, (h + 1) * B_HEAD_DIM)
        kp_scr[0:B_WIN, sl] = _head_rows(kc_ref, h, B_WIN, B_HEADS)[...].astype(BF16)
        vp_scr[0:B_WIN, sl] = _head_rows(vc_ref, h, B_WIN, B_HEADS)[...].astype(BF16)
        _head_rows(ck_ref, h, qs, B_HEADS)[...] = kn_ref[:, sl]
        _head_rows(cv_ref, h, qs, B_HEADS)[...] = vn_ref[:, sl]
    kp_scr[B_WIN:B_WIN + qs, :] = kn_ref[...].astype(BF16)
    vp_scr[B_WIN:B_WIN + qs, :] = vn_ref[...].astype(BF16)
    _band_attend(q_ref, bias_ref, o_ref, kp_scr, vp_scr, 0, None, qs)


def _band_prompt(proj, bias, out_k, out_v, layer, *, nb, tq, qs):
    nsteps = tq // qs
    w = qs + B_WIN
    nkeep = min(B_WIN, tq)
    kcol = lambda off: pl.BlockSpec((tq, B_WIDTH), lambda b, j: (b, off // B_WIDTH))
    cache = pl.BlockSpec((nkeep * B_HEADS, B_HEAD_DIM), lambda b, j: (layer * nb + b, 0))
    return pl.pallas_call(
        functools.partial(_band_prompt_kernel, s_keys=tq, qs=qs, nkeep=nkeep),
        grid=(nb, nsteps),
        in_specs=[
            pl.BlockSpec((qs, B_WIDTH), lambda b, j: (b * nsteps + j, COL_BQ // B_WIDTH)),
            kcol(COL_BK), kcol(COL_BV),
            pl.BlockSpec((1, B_HEADS, qs, w), lambda b, j: (layer, 0, 0, 0)),
            ANY, ANY,
        ],
        out_specs=[pl.BlockSpec((qs, B_WIDTH), lambda b, j: (b * nsteps + j, 0)), cache, cache],
        out_shape=[jax.ShapeDtypeStruct((proj.shape[0], B_WIDTH), F32),
                   jax.ShapeDtypeStruct(out_k.shape, F32), jax.ShapeDtypeStruct(out_v.shape, F32)],
        input_output_aliases={4: 1, 5: 2},
        scratch_shapes=[pltpu.VMEM((B_WIN + tq, B_WIDTH), BF16),
                        pltpu.VMEM((B_WIN + tq, B_WIDTH), BF16)],
        compiler_params=_cparams("parallel", "arbitrary"),
        name="band_prompt",
    )(proj, proj, proj, bias, out_k, out_v)


def _band_sample(proj, cache_k, cache_v, bias, buf, out_k, out_v, layer, *, nb, ts, row0):
    rb0 = row0 // ts
    w = ts + B_WIN
    qcol = lambda off: pl.BlockSpec((ts, B_WIDTH), lambda b, j: (rb0 + b, off // B_WIDTH))
    per_seq = lambda rows: pl.BlockSpec((rows * B_HEADS, B_HEAD_DIM), lambda b, j: (layer * nb + b, 0))
    return pl.pallas_call(
        functools.partial(_band_sample_kernel, qs=ts),
        grid=(nb, 1),
        in_specs=[qcol(COL_BQ), per_seq(B_WIN), per_seq(B_WIN), qcol(COL_BK), qcol(COL_BV),
                  pl.BlockSpec((1, B_HEADS, ts, w), lambda b, j: (layer, 0, 0, 0)),
                  ANY, ANY, ANY],
        out_specs=[pl.BlockSpec((ts, B_WIDTH), lambda b, j: (rb0 + b, 0)), per_seq(ts), per_seq(ts)],
        out_shape=[jax.ShapeDtypeStruct(buf.shape, F32), jax.ShapeDtypeStruct(out_k.shape, F32),
                   jax.ShapeDtypeStruct(out_v.shape, F32)],
        input_output_aliases={6: 0, 7: 1, 8: 2},
        scratch_shapes=[pltpu.VMEM((w, B_WIDTH), BF16), pltpu.VMEM((w, B_WIDTH), BF16)],
        compiler_params=_cparams("parallel", "arbitrary"),
        name="band_sample",
    )(proj, cache_k, cache_v, proj, proj, bias, buf, out_k, out_v)


XP_OFF = 8
SSD_CHUNKS_PER_STEP = 2
_HEAD_SPREAD = np.zeros((LANES, C_WIDTH), np.float32)
for _h in range(C_HEADS):
    _HEAD_SPREAD[MISC_DT + _h, _h * C_HEAD_DIM:(_h + 1) * C_HEAD_DIM] = 1.0


def _ssd_kernel(*refs, zero_init, nsteps, cps):
    if zero_init:
        (z_ref, xbc_ref, misc_ref, cw_ref, cbias_ref, prm_ref, gn_ref, ex_ref, dskip_ref,
         bufh_ref, bufc_ref, o_ref, hout_ref, cnew_ref, st_scr, xp_scr, y_scr) = refs
    else:
        (z_ref, xbc_ref, misc_ref, h0_ref, cb0_ref, cw_ref, cbias_ref, prm_ref, gn_ref, ex_ref, dskip_ref,
         bufo_ref, bufh_ref, bufc_ref, o_ref, hout_ref, cnew_ref, st_scr, xp_scr, y_scr) = refs
        del bufo_ref
    del bufh_ref, bufc_ref
    c = pl.program_id(1)
    tail = CONV_W - 1
    rep = C_HEADS // C_GROUPS
    gw = C_WIDTH // C_GROUPS

    @pl.when(c == 0)
    def _():
        if zero_init:
            st_scr[...] = jnp.zeros(st_scr.shape, F32)
            xp_scr[XP_OFF - tail:XP_OFF, :] = jnp.zeros((tail, CONV_DIM), F32)
        else:
            for h in range(C_HEADS):
                st_scr[h // rep, :, (h % rep) * C_HEAD_DIM:(h % rep + 1) * C_HEAD_DIM] = h0_ref[0, h].T
            xp_scr[XP_OFF - tail:XP_OFF, :] = cb0_ref[0]

    def chunk(rows):
        xp_scr[XP_OFF:XP_OFF + CHUNK, :] = xbc_ref[rows, :]
        y = cbias_ref[...] + xp_scr[XP_OFF - tail:XP_OFF - tail + CHUNK, :] * cw_ref[0:1, :]
        for t in range(1, CONV_W):
            y = y + xp_scr[XP_OFF - tail + t:XP_OFF - tail + t + CHUNK, :] * cw_ref[t:t + 1, :]
        new_tail = xp_scr[XP_OFF + CHUNK - tail:XP_OFF + CHUNK, :]
        xp_scr[XP_OFF - tail:XP_OFF, :] = new_tail
        cnew_ref[0] = new_tail
        xc = y * (1.0 / (1.0 + jnp.exp(-y)))
        xs = xc[:, 0:C_WIDTH]

        pre = misc_ref[rows, :] + prm_ref[0:1, :]
        dt = jnp.maximum(pre, 0.0) + jnp.log(1.0 + jnp.exp(-jnp.abs(pre)))
        ad = dt * (-jnp.exp(prm_ref[1:2, :]))
        li = lax.broadcasted_iota(I32, (CHUNK, CHUNK), 0)
        si = lax.broadcasted_iota(I32, (CHUNK, CHUNK), 1)
        causal = li >= si
        tri = causal.astype(BF16)
        acs = sum(jnp.dot(tri, part, preferred_element_type=F32) for part in _split3(ad))
        acs_t = acs.T
        ex = ex_ref[...]
        spread = lambda v: sum(jnp.dot(part, ex, preferred_element_type=F32) for part in _split3(v))
        dt_x = spread(dt)
        acs_x = spread(acs)
        acs_last_x = acs_x[CHUNK - 1:CHUNK, :]
        xd_all = xs * dt_x
        xd_b = xd_all.astype(BF16)
        xde_b = (xd_all * jnp.exp(acs_last_x - acs_x)).astype(BF16)
        grow_x = jnp.exp(acs_x)
        chunk_decay_x = jnp.exp(acs_last_x)
        head_of_lane = lax.broadcasted_iota(I32, (CHUNK, gw), 1) >> 6

        for g in range(C_GROUPS):
            gl = slice(g * gw, (g + 1) * gw)
            b_g = xc[:, C_WIDTH + g * C_STATE:C_WIDTH + (g + 1) * C_STATE]
            c_g = xc[:, C_WIDTH + (C_GROUPS + g) * C_STATE:C_WIDTH + (C_GROUPS + g + 1) * C_STATE]
            b_gt = b_g.T.astype(BF16)
            c_gb = c_g.astype(BF16)
            cb = jnp.dot(c_gb, b_gt, preferred_element_type=F32)
            st = st_scr[g]
            y_g = jnp.dot(c_gb, st.astype(BF16), preferred_element_type=F32) * grow_x[:, gl]
            xd_g = xd_b[:, gl]
            for r in range(rep):
                col = MISC_DT + g * rep + r
                seg = acs[:, col:col + 1] - acs_t[col:col + 1, :]
                decay_in = jnp.where(causal, jnp.exp(jnp.where(causal, seg, 0.0)), 0.0)
                t_r = jnp.dot((cb * decay_in).astype(BF16), xd_g, preferred_element_type=F32)
                y_g = y_g + jnp.where(head_of_lane == r, t_r, 0.0)
            y_scr[:, gl] = y_g + dskip_ref[:, gl] * xs[:, gl]
            st_scr[g] = st * chunk_decay_x[:, gl] + jnp.dot(b_gt, xde_b[:, gl], preferred_element_type=F32)

        z = z_ref[rows, :]
        gate = y_scr[...] * (z * (1.0 / (1.0 + jnp.exp(-z))))
        for g in range(C_GROUPS):
            gg = gate[:, g * gw:(g + 1) * gw]
            ms = jnp.mean(gg * gg, axis=-1, keepdims=True)
            o_ref[rows, g * gw:(g + 1) * gw] = gg * lax.rsqrt(ms + EPS) * gn_ref[:, g * gw:(g + 1) * gw]

    for i in range(cps):
        chunk(slice(i * CHUNK, (i + 1) * CHUNK))

    @pl.when(c == nsteps - 1)
    def _():
        for h in range(C_HEADS):
            hout_ref[0, h] = st_scr[h // rep, :, (h % rep) * C_HEAD_DIM:(h % rep + 1) * C_HEAD_DIM].T


def _ssd(proj, conv_w, conv_b, prm, gate_norm, out_h, out_c, layer, *, nb, nchunks, row0, state=None):
    cps = SSD_CHUNKS_PER_STEP if nchunks % SSD_CHUNKS_PER_STEP == 0 else 1
    nsteps = nchunks // cps
    rt = cps * CHUNK
    rb0 = row0 // rt
    row = lambda b, c: rb0 + b * nsteps + c
    const = lambda shape: pl.BlockSpec(shape, lambda b, c: tuple(0 for _ in shape))
    per_seq_h = pl.BlockSpec((1, C_HEADS, C_HEAD_DIM, C_STATE), lambda b, c: (layer * nb + b, 0, 0, 0))
    per_seq_c = pl.BlockSpec((1, CONV_W - 1, CONV_DIM), lambda b, c: (layer * nb + b, 0, 0))
    in_specs = [
        pl.BlockSpec((rt, C_WIDTH), lambda b, c: (row(b, c), COL_CZ // C_WIDTH)),
        pl.BlockSpec((rt, CONV_DIM), lambda b, c: (row(b, c), COL_CXBC // CONV_DIM)),
        pl.BlockSpec((rt, LANES), lambda b, c: (row(b, c), COL_MISC // LANES)),
    ]
    args = [proj, proj, proj]
    if state is not None:
        h0, cbuf0, buf = state
        in_specs += [per_seq_h, per_seq_c]
        args += [h0, cbuf0]
    in_specs += [const((CONV_W, CONV_DIM)), const((1, CONV_DIM)), const((8, LANES)), const((1, C_WIDTH)),
                 const((LANES, C_WIDTH)), const((1, C_WIDTH))]
    args += [conv_w, conv_b.reshape(1, CONV_DIM), prm[0], gate_norm.reshape(1, C_WIDTH),
             jnp.asarray(_HEAD_SPREAD, BF16), prm[1]]
    aliases = {}
    if state is not None:
        in_specs.append(ANY)
        args.append(buf)
        aliases[len(args) - 1] = 0
    in_specs += [ANY, ANY]
    args += [out_h, out_c]
    aliases[len(args) - 2] = 1
    aliases[len(args) - 1] = 2
    return pl.pallas_call(
        functools.partial(_ssd_kernel, zero_init=state is None, nsteps=nsteps, cps=cps),
        grid=(nb, nsteps),
        in_specs=in_specs,
        out_specs=[pl.BlockSpec((rt, C_WIDTH), lambda b, c: (row(b, c), 0)), per_seq_h, per_seq_c],
        out_shape=[
            jax.ShapeDtypeStruct((proj.shape[0], C_WIDTH), F32),
            jax.ShapeDtypeStruct(out_h.shape, F32),
            jax.ShapeDtypeStruct(out_c.shape, F32),
        ],
        input_output_aliases=aliases,
        scratch_shapes=[pltpu.VMEM((C_GROUPS, C_STATE, C_WIDTH // C_GROUPS), F32),
                        pltpu.VMEM((XP_OFF + CHUNK, CONV_DIM), F32),
                        pltpu.VMEM((CHUNK, C_WIDTH), F32)],
        compiler_params=_cparams("parallel", "arbitrary"),
        name="ssd_prompt" if state is None else "ssd_sample",
    )(*args)


def _outproj_kernel(x_ref, oa_ref, ob_ref, oc_ref, w_ref, o_ref):
    acc = x_ref[...]
    acc = acc + jnp.dot(oa_ref[...].astype(BF16), w_ref[0:A_WIDTH, :], preferred_element_type=F32)
    acc = acc + jnp.dot(ob_ref[...].astype(BF16), w_ref[A_WIDTH:A_WIDTH + B_WIDTH, :],
                        preferred_element_type=F32)
    acc = acc + jnp.dot(oc_ref[...].astype(BF16), w_ref[A_WIDTH + B_WIDTH:, :],
                        preferred_element_type=F32)
    o_ref[...] = acc


def _out_proj(x, oa, ob, oc, w_bf16):
    t, d = x.shape
    tm = _row_tile(t, 512)
    rows = lambda w: pl.BlockSpec((tm, w), lambda i: (i, 0))
    return pl.pallas_call(
        _outproj_kernel,
        grid=(t // tm,),
        in_specs=[rows(d), rows(A_WIDTH), rows(B_WIDTH), rows(C_WIDTH),
                  pl.BlockSpec((d, d), lambda i: (0, 0))],
        out_specs=rows(d),
        out_shape=jax.ShapeDtypeStruct((t, d), F32),
        compiler_params=_cparams("parallel"),
        name="out_proj",
    )(x, oa, ob, oc, w_bf16)


def _ffn_kernel(x_ref, nw_ref, wu_ref, wd_ref, o_ref, h_scr):
    @pl.when(pl.program_id(1) == 0)
    def _():
        x = x_ref[...]
        ms = jnp.mean(x * x, axis=-1, keepdims=True)
        h_scr[...] = (x * lax.rsqrt(ms + EPS) * nw_ref[...]).astype(BF16)
        o_ref[...] = x

    u = jnp.maximum(jnp.dot(h_scr[...], wu_ref[...], preferred_element_type=F32), 0.0)
    o_ref[...] += jnp.dot((u * u).astype(BF16), wd_ref[...], preferred_element_type=F32)


def _ffn(x, norm_w, wu_bf16, wd_bf16):
    t, d = x.shape
    ff = wu_bf16.shape[1]
    tm = _row_tile(t, 1024)
    tf = 512
    return pl.pallas_call(
        _ffn_kernel,
        grid=(t // tm, ff // tf),
        in_specs=[
            pl.BlockSpec((tm, d), lambda i, f: (i, 0)),
            pl.BlockSpec((1, d), lambda i, f: (0, 0)),
            pl.BlockSpec((d, tf), lambda i, f: (0, f)),
            pl.BlockSpec((tf, d), lambda i, f: (f, 0)),
        ],
        out_specs=pl.BlockSpec((tm, d), lambda i, f: (i, 0)),
        out_shape=jax.ShapeDtypeStruct((t, d), F32),
        scratch_shapes=[pltpu.VMEM((tm, d), BF16)],
        compiler_params=_cparams("parallel", "arbitrary"),
        name="ffn",
    )(x, norm_w.reshape(1, d), wu_bf16, wd_bf16)


def _norm_kernel(x_ref, nw_ref, o_ref):
    x = x_ref[...]
    ms = jnp.mean(x * x, axis=-1, keepdims=True)
    o_ref[...] = x * lax.rsqrt(ms + EPS) * nw_ref[...]


def _final_norm(x, norm_w, row0, nrows):
    d = x.shape[1]
    tm = _row_tile(nrows, 512)
    assert row0 % tm == 0
    return pl.pallas_call(
        _norm_kernel,
        grid=(nrows // tm,),
        in_specs=[pl.BlockSpec((tm, d), lambda i: (row0 // tm + i, 0)),
                  pl.BlockSpec((1, d), lambda i: (0, 0))],
        out_specs=pl.BlockSpec((tm, d), lambda i: (i, 0)),
        out_shape=jax.ShapeDtypeStruct((nrows, d), F32),
        compiler_params=_cparams("parallel"),
        name="final_norm",
    )(x, norm_w.reshape(1, d))


def _lane_row(v, off):
    return jnp.zeros((LANES,), F32).at[off:off + v.shape[0]].set(v)


def _trunk(x_prompt, x_sample, cache_a_k, cache_a_v, cache_a_kidx, cache_b_k, cache_b_v,
           state_ssm, state_conv, norm1, w_in, w_out, b_rel, conv_w, conv_b, dt_bias,
           a_log, d_skip, gate_norm, norm2, w_up, w_down, final_norm):
    bp, tp, d = x_prompt.shape
    bs, ts, _ = x_sample.shape
    depth = w_in.shape[0]
    past = cache_a_k.shape[2]
    nbuf = cache_b_k.shape[2]
    n_p = bp * tp
    n_s = bs * ts
    assert ts == CHUNK and tp % DSA_KB_PROMPT == 0 and tp % BAND_QS_PROMPT == 0 and nbuf == B_WIN
    qs_p = LANES
    nkeep = min(B_WIN, tp)

    x = jnp.concatenate([x_prompt.reshape(n_p, d), x_sample.reshape(n_s, d)], axis=0)

    pos = jnp.concatenate([jnp.tile(jnp.arange(tp), bp), jnp.tile(past + jnp.arange(ts), bs)])
    tab_a = _rope_tables(pos, A_HEAD_DIM)
    tab_i = _rope_tables(pos, IDX_DIM)
    bias_p = _band_bias(b_rel, BAND_QS_PROMPT)
    bias_s = _band_bias(b_rel, ts)
    w_in_r = _regroup_w_in(w_in)
    cak = cache_a_k.reshape(depth * bs * past * A_HEADS, A_HEAD_DIM)
    cav = cache_a_v.reshape(depth * bs * past * A_HEADS, A_HEAD_DIM)
    caki = cache_a_kidx.reshape(depth * bs * past, IDX_DIM)
    cbk = cache_b_k.reshape(depth * bs * nbuf * B_HEADS, B_HEAD_DIM)
    cbv = cache_b_v.reshape(depth * bs * nbuf * B_HEADS, B_HEAD_DIM)
    h0 = state_ssm.reshape(depth * bs, C_HEADS, C_HEAD_DIM, C_STATE)
    cbuf0 = state_conv.reshape(depth * bs, CONV_W - 1, CONV_DIM)

    p_ak = _uninit((depth * bp * tp * A_HEADS, A_HEAD_DIM))
    p_av = _uninit((depth * bp * tp * A_HEADS, A_HEAD_DIM))
    p_aki = _uninit((depth * bp * tp, IDX_DIM))
    p_bk = _uninit((depth * bp * nkeep * B_HEADS, B_HEAD_DIM))
    p_bv = _uninit((depth * bp * nkeep * B_HEADS, B_HEAD_DIM))
    p_h = _uninit((depth * bp, C_HEADS, C_HEAD_DIM, C_STATE))
    p_c = _uninit((depth * bp, CONV_W - 1, CONV_DIM))
    s_ak = _uninit((depth * bs * ts * A_HEADS, A_HEAD_DIM))
    s_av = _uninit((depth * bs * ts * A_HEADS, A_HEAD_DIM))
    s_aki = _uninit((depth * bs * ts, IDX_DIM))
    s_bk = _uninit((depth * bs * ts * B_HEADS, B_HEAD_DIM))
    s_bv = _uninit((depth * bs * ts * B_HEADS, B_HEAD_DIM))
    s_h = _uninit((depth * bs, C_HEADS, C_HEAD_DIM, C_STATE))
    s_c = _uninit((depth * bs, CONV_W - 1, CONV_DIM))

    for l in range(depth):
        proj = _in_proj(x, norm1[l], w_in_r, l, tab_a, tab_i)

        oa, p_ak, p_av, p_aki = _dsa_prompt(proj, p_ak, p_av, p_aki, l, nb=bp, tq=tp, qs=qs_p,
                                            kb=DSA_KB_PROMPT)
        oa, s_ak, s_av, s_aki = _dsa_sample(proj, cak, cav, caki, oa, s_ak, s_av, s_aki, l,
                                            nb=bs, ts=ts, past=past, row0=n_p, kb=DSA_KB_SAMPLE)
        ob, p_bk, p_bv = _band_prompt(proj, bias_p, p_bk, p_bv, l, nb=bp, tq=tp, qs=BAND_QS_PROMPT)
        ob, s_bk, s_bv = _band_sample(proj, cbk, cbv, bias_s, ob, s_bk, s_bv, l, nb=bs, ts=ts, row0=n_p)
        rows = jnp.zeros((8, LANES), F32)
        rows = rows.at[0].set(_lane_row(dt_bias[l], MISC_DT)).at[1].set(_lane_row(a_log[l], MISC_DT))
        prm = (rows, jnp.repeat(d_skip[l], C_HEAD_DIM).reshape(1, C_WIDTH))
        oc, p_h, p_c = _ssd(proj, conv_w[l], conv_b[l], prm, gate_norm[l], p_h, p_c, l,
                            nb=bp, nchunks=tp // CHUNK, row0=0)
        oc, s_h, s_c = _ssd(proj, conv_w[l], conv_b[l], prm, gate_norm[l], s_h, s_c, l,
                            nb=bs, nchunks=1, row0=n_p, state=(h0, cbuf0, oc))

        x = _out_proj(x, oa, ob, oc, w_out[l].astype(BF16))
        x = _ffn(x, norm2[l], w_up[l].astype(BF16), w_down[l].astype(BF16))

    y_prompt = _final_norm(x, final_norm, 0, n_p).reshape(bp, tp, d)
    y_sample = _final_norm(x, final_norm, n_p, n_s).reshape(bs, ts, d)
    return (y_prompt, y_sample,
            p_ak.reshape(depth, bp, tp, A_HEADS, A_HEAD_DIM),
            p_av.reshape(depth, bp, tp, A_HEADS, A_HEAD_DIM),
            p_aki.reshape(depth, bp, tp, IDX_DIM),
            p_bk.reshape(depth, bp, nkeep, B_HEADS, B_HEAD_DIM),
            p_bv.reshape(depth, bp, nkeep, B_HEADS, B_HEAD_DIM),
            p_h.reshape(depth, bp, C_HEADS, C_HEAD_DIM, C_STATE),
            p_c.reshape(depth, bp, CONV_W - 1, CONV_DIM),
            s_ak.reshape(depth, bs, ts, A_HEADS, A_HEAD_DIM),
            s_av.reshape(depth, bs, ts, A_HEADS, A_HEAD_DIM),
            s_aki.reshape(depth, bs, ts, IDX_DIM),
            s_bk.reshape(depth, bs, ts, B_HEADS, B_HEAD_DIM),
            s_bv.reshape(depth, bs, ts, B_HEADS, B_HEAD_DIM),
            s_h.reshape(depth, bs, C_HEADS, C_HEAD_DIM, C_STATE),
            s_c.reshape(depth, bs, CONV_W - 1, CONV_DIM))


def kernel(x_prompt, x_sample, cache_a_k, cache_a_v, cache_a_kidx, cache_b_k, cache_b_v, state_ssm, state_conv, norm1, w_in, w_out, b_rel, conv_w, conv_b, dt_bias, a_log, d_skip, gate_norm, norm2, w_up, w_down, final_norm):
    return _trunk(x_prompt, x_sample, cache_a_k, cache_a_v, cache_a_kidx, cache_b_k, cache_b_v,
                  state_ssm, state_conv, norm1, w_in, w_out, b_rel, conv_w, conv_b, dt_bias,
                  a_log, d_skip, gate_norm, norm2, w_up, w_down, final_norm)
```

```python
import functools

import jax
import jax.numpy as jnp
import numpy as np
from jax import lax
from jax.experimental import pallas as pl
from jax.experimental.pallas import tpu as pltpu

F32 = jnp.float32
BF16 = jnp.bfloat16
I32 = jnp.int32

D_MODEL = 2048
CHUNK = 64
A_HEADS = 4
A_HEAD_DIM = 128
A_WIDTH = A_HEADS * A_HEAD_DIM
IDX_HEADS = 16
IDX_DIM = 64
IQ_WIDTH = IDX_HEADS * IDX_DIM
DSA_TOPK = 256
B_HEADS = 4
B_HEAD_DIM = 128
B_WIDTH = B_HEADS * B_HEAD_DIM
B_PREV_CHUNKS = 8
B_WIN = B_PREV_CHUNKS * CHUNK
REL_CLIP = 128
C_WIDTH = 1024
C_HEAD_DIM = 64
C_HEADS = C_WIDTH // C_HEAD_DIM
C_GROUPS = 4
C_STATE = 128
CONV_W = 4
CONV_DIM = C_WIDTH + 2 * C_GROUPS * C_STATE
D_FF = 4 * D_MODEL
ROPE_THETA = 500000.0
EPS = 1e-5

LANES = 128
INT_MIN = -(2 ** 31)
NEG_INF = float("-inf")

COL_AQ = 0
COL_AK = 512
COL_IQ = 1024
COL_AV = 2048
COL_BQ = 2560
COL_BK = 3072
COL_BV = 3584
COL_CXBC = 4096
COL_CZ = 6144
COL_MISC = 7168
MISC_IW = 64
MISC_DT = 80
IN_COLS_PAD = 7680
IN_TN = 1280
SRC_AQ, SRC_AV, SRC_IQ, SRC_IK, SRC_BQ, SRC_CZ, SRC_CXBC, SRC_DT, SRC_END = (
    0, 1024, 1536, 2560, 2640, 4176, 5200, 7248, 7264)

VMEM_LIMIT = 56 * 1024 * 1024
ANY = pl.BlockSpec(memory_space=pl.ANY)


def _cparams(*sem):
    return pltpu.CompilerParams(dimension_semantics=sem, vmem_limit_bytes=VMEM_LIMIT)


def _row_tile(t, cap):
    tm = cap
    while t % tm:
        tm //= 2
    return tm


def _nt_dot(a, b):
    return lax.dot_general(a, b, (((1,), (1,)), ((), ())), preferred_element_type=F32)


def _split3(x):
    hi = x.astype(BF16)
    r1 = x - hi.astype(F32)
    mid = r1.astype(BF16)
    lo = (r1 - mid.astype(F32)).astype(BF16)
    return hi, mid, lo


def _head_rows(ref, h, n, heads):
    return ref.at[pl.ds(h, n, stride=heads), :]


def _uninit_kernel(o_ref):
    del o_ref


def _uninit(shape, dtype=F32):
    return pl.pallas_call(_uninit_kernel, out_specs=ANY, out_shape=jax.ShapeDtypeStruct(shape, dtype),
                          name="alloc")()


def _regroup_kernel(w_ref, o_ref):
    def put(dst, src, width):
        o_ref[0, :, dst:dst + width] = w_ref[0, :, src:src + width].astype(BF16)

    rows = w_ref.shape[1]
    put(COL_AQ, SRC_AQ, 2 * A_WIDTH)
    put(COL_IQ, SRC_IQ, IQ_WIDTH)
    put(COL_AV, SRC_AV, A_WIDTH)
    put(COL_BQ, SRC_BQ, 3 * B_WIDTH)
    put(COL_CXBC, SRC_CXBC, CONV_DIM)
    put(COL_CZ, SRC_CZ, C_WIDTH)
    o_ref[0, :, COL_MISC:] = jnp.zeros((rows, IN_COLS_PAD - COL_MISC), BF16)
    put(COL_MISC, SRC_IK, IDX_DIM + IDX_HEADS)
    put(COL_MISC + MISC_DT, SRC_DT, C_HEADS)


def _regroup_w_in(w_in):
    depth, d, n = w_in.shape
    assert n == SRC_END
    tk = 256
    return pl.pallas_call(
        _regroup_kernel,
        grid=(depth, d // tk),
        in_specs=[pl.BlockSpec((1, tk, n), lambda l, i: (l, i, 0))],
        out_specs=pl.BlockSpec((1, tk, IN_COLS_PAD), lambda l, i: (l, i, 0)),
        out_shape=jax.ShapeDtypeStruct((depth, d, IN_COLS_PAD), BF16),
        compiler_params=_cparams("parallel", "parallel"),
        name="regroup_w_in",
    )(w_in)


def _rope_block(x, c, sm, sp, half):
    return x * c + pltpu.roll(x, LANES - half, 1) * sm + pltpu.roll(x, half, 1) * sp


def _inproj_kernel(x_ref, nw_ref, w_ref, ta_ref, ti_ref, o_ref, h_scr):
    j = pl.program_id(1)

    @pl.when(j == 0)
    def _():
        x = x_ref[...]
        ms = jnp.mean(x * x, axis=-1, keepdims=True)
        h_scr[...] = (x * lax.rsqrt(ms + EPS) * nw_ref[...]).astype(BF16)

    o_ref[...] = jnp.dot(h_scr[...], w_ref[0], preferred_element_type=F32)

    def rotate(tile):
        for blk in range(IN_TN // LANES):
            col = tile * IN_TN + blk * LANES
            sl = slice(blk * LANES, (blk + 1) * LANES)
            if col < COL_IQ:
                o_ref[:, sl] = _rope_block(o_ref[:, sl], ta_ref[0], ta_ref[1], ta_ref[2],
                                           A_HEAD_DIM // 8)
            elif col < COL_AV:
                o_ref[:, sl] = _rope_block(o_ref[:, sl], ti_ref[0], ti_ref[1], ti_ref[2], IDX_DIM // 8)
            elif col == COL_MISC:
                m = o_ref[:, sl]
                lane = lax.broadcasted_iota(I32, m.shape, 1)
                rot = _rope_block(m, ti_ref[0], ti_ref[1], ti_ref[2], IDX_DIM // 8)
                o_ref[:, sl] = jnp.where(lane < IDX_DIM, rot, m)

    for tile in range(IN_COLS_PAD // IN_TN):
        lo, hi = tile * IN_TN, (tile + 1) * IN_TN
        if lo < COL_AV or lo <= COL_MISC < hi:
            pl.when(j == tile)(functools.partial(rotate, tile))


def _in_proj(x, norm_w, w_all, layer, tab_a, tab_i):
    t, d = x.shape
    n = w_all.shape[2]
    tm = _row_tile(t, 1024)
    tab = pl.BlockSpec((3, tm, LANES), lambda i, j: (0, i, 0))
    return pl.pallas_call(
        _inproj_kernel,
        grid=(t // tm, n // IN_TN),
        in_specs=[
            pl.BlockSpec((tm, d), lambda i, j: (i, 0)),
            pl.BlockSpec((1, d), lambda i, j: (0, 0)),
            pl.BlockSpec((1, d, IN_TN), lambda i, j: (layer, 0, j)),
            tab, tab,
        ],
        out_specs=pl.BlockSpec((tm, IN_TN), lambda i, j: (i, j)),
        out_shape=jax.ShapeDtypeStruct((t, n), F32),
        scratch_shapes=[pltpu.VMEM((tm, d), BF16)],
        compiler_params=_cparams("parallel", "arbitrary"),
        name="in_proj",
    )(x, norm_w.reshape(1, d), w_all, tab_a, tab_i)


def _rope_tables(pos, head_dim):
    rot = head_dim // 4
    half = rot // 2
    inv = ROPE_THETA ** (-jnp.arange(half, dtype=F32) * 2.0 / rot)
    ang = pos.astype(F32)[:, None] * inv[None, :]
    cos, sin = jnp.cos(ang), jnp.sin(ang)
    n = pos.shape[0]
    rest = head_dim - rot
    c = jnp.concatenate([cos, cos, jnp.ones((n, rest), F32)], axis=1)
    sm = jnp.concatenate([-sin, jnp.zeros((n, half + rest), F32)], axis=1)
    sp = jnp.concatenate([jnp.zeros((n, half), F32), sin, jnp.zeros((n, rest), F32)], axis=1)
    tab = jnp.stack([c, sm, sp], axis=0)
    return jnp.tile(tab, (1, 1, LANES // head_dim))


TIE_BLOCK = 64
SUB = 128
DSA_KB_PROMPT = 256
DSA_KB_SAMPLE = 384
ATT_BLOCK = 256


def _tree_sum(parts):
    while len(parts) > 1:
        parts = [parts[i] + parts[i + 1] for i in range(0, len(parts) - 1, 2)] + (
            parts[-1:] if len(parts) % 2 else [])
    return parts[0]


def _count_keys(key_scr, n_keys, qs, pred):
    blocks = []
    for u in range(n_keys // SUB):
        hit = jnp.where(pred(key_scr[u * SUB:(u + 1) * SUB, :]), 1, 0).astype(I32)
        blocks.append(_tree_sum([hit[i * 8:(i + 1) * 8, :] for i in range(SUB // 8)]))
    return jnp.sum(_tree_sum(blocks), axis=0, keepdims=True)


def _kth_largest(key_scr, n_keys, qs, k):
    count = functools.partial(_count_keys, key_scr, n_keys, qs)
    ans0 = jnp.where(count(lambda x: x >= 0) >= k, 0, INT_MIN).astype(I32)

    def bit_step(i, ans):
        cand = ans | (jnp.int32(1) << (30 - i))
        return jnp.where(count(lambda x: x >= cand) >= k, cand, ans)

    return lax.fori_loop(0, 31, bit_step, ans0)


def _dsa_select_attend(q_ref, qi_ref, mq_ref, o_ref, kb_scr, vt_scr, kib_scr, w_scr, score_scr,
                       key_scr, mask_scr, *, n_keys, qs, topk, limit):
    w_scr[...] = mq_ref[...].T * (IDX_HEADS ** -0.5 * IDX_DIM ** -0.5)

    kib = kib_scr[0:n_keys, :]
    for h in range(IDX_HEADS):
        qh = qi_ref[:, h * IDX_DIM:(h + 1) * IDX_DIM].astype(BF16)
        term = jnp.maximum(_nt_dot(kib, qh), 0.0) * w_scr[MISC_IW + h:MISC_IW + h + 1, :]
        if h == 0:
            score_scr[0:n_keys, :] = term
        else:
            score_scr[0:n_keys, :] += term
    sc = score_scr[0:n_keys, :]
    sc = jnp.where(sc == 0.0, 0.0, sc)
    bits = lax.bitcast_convert_type(sc, I32)
    key = bits ^ ((bits >> 31) & 0x7FFFFFFF)
    s_idx = lax.broadcasted_iota(I32, (n_keys, qs), 0)
    key_scr[0:n_keys, :] = jnp.where(s_idx < limit, key, INT_MIN)

    thr = _kth_largest(key_scr, n_keys, qs, topk)
    cnt_gt = _count_keys(key_scr, n_keys, qs, lambda x: x > thr)
    cnt_eq = _count_keys(key_scr, n_keys, qs, lambda x: x == thr)
    room = topk - cnt_gt
    live = thr > INT_MIN
    all_ties = (cnt_eq <= room) & live
    key = key_scr[0:n_keys, :]
    mask_scr[0:n_keys, :] = jnp.where((key > thr) | ((key == thr) & all_ties), 0.0, NEG_INF)
    need_ties = jnp.max(((cnt_eq > room) & live).astype(I32)) > 0

    @pl.when(need_ties)
    def _():
        tri = (lax.broadcasted_iota(I32, (TIE_BLOCK, TIE_BLOCK), 0)
               >= lax.broadcasted_iota(I32, (TIE_BLOCK, TIE_BLOCK), 1)).astype(BF16)

        def blk(b, carry):
            rows = pl.ds(pl.multiple_of(b * TIE_BLOCK, TIE_BLOCK), TIE_BLOCK)
            kblk = key_scr[rows, :]
            eqb = (kblk == thr) & live
            eqf = eqb.astype(F32)
            prefix = jnp.dot(tri, eqf.astype(BF16), preferred_element_type=F32) + carry
            keep = (kblk > thr) | (eqb & (prefix <= room.astype(F32)))
            mask_scr[rows, :] = jnp.where(keep, 0.0, NEG_INF)
            return carry + jnp.sum(eqf, axis=0, keepdims=True)

        lax.fori_loop(0, n_keys // TIE_BLOCK, blk, jnp.zeros((1, qs), F32))

    ab = ATT_BLOCK if n_keys % ATT_BLOCK == 0 else SUB
    for h in range(A_HEADS):
        sl = slice(h * A_HEAD_DIM, (h + 1) * A_HEAD_DIM)
        qh = q_ref[:, sl].astype(BF16)
        parts = []
        for b in range(n_keys // ab):
            rows = slice(b * ab, (b + 1) * ab)
            lg = _nt_dot(kb_scr[rows, sl], qh) * (A_HEAD_DIM ** -0.5) + mask_scr[rows, :]
            m_b = jnp.max(lg, axis=0, keepdims=True)
            e = jnp.exp(lg - jnp.where(m_b == NEG_INF, 0.0, m_b))
            den_b = jnp.sum(e, axis=0, keepdims=True)
            o_b = jnp.dot(vt_scr[sl, rows], e.astype(BF16), preferred_element_type=F32)
            parts.append((m_b, den_b, o_b))
        m = functools.reduce(jnp.maximum, [p[0] for p in parts])
        den = jnp.zeros((1, qs), F32)
        o_t = jnp.zeros((A_HEAD_DIM, qs), F32)
        for m_b, den_b, o_b in parts:
            wgt = jnp.exp(m_b - m)
            den = den + wgt * den_b
            o_t = o_t + wgt * o_b
        o_ref[:, sl] = (o_t / den).T


def _dsa_prompt_kernel(q_ref, qi_ref, mq_ref, k_ref, v_ref, mk_ref, bufk_ref, bufv_ref, bufi_ref,
                       o_ref, ck_ref, cv_ref, ci_ref,
                       kb_scr, vt_scr, kib_scr, w_scr, score_scr, key_scr, mask_scr,
                       *, s_keys, qs, kb, topk):
    del bufk_ref, bufv_ref, bufi_ref
    j = pl.program_id(1)

    @pl.when(j == 0)
    def _():
        kb_scr[...] = k_ref[...].astype(BF16)
        kib_scr[...] = mk_ref[:, 0:IDX_DIM].astype(BF16)
        for i in range(s_keys // kb):
            vt_scr[:, i * kb:(i + 1) * kb] = v_ref[i * kb:(i + 1) * kb, :].T.astype(BF16)
        ci_ref[...] = mk_ref[:, 0:IDX_DIM]
        for h in range(A_HEADS):
            sl = slice(h * A_HEAD_DIM, (h + 1) * A_HEAD_DIM)
            _head_rows(ck_ref, h, s_keys, A_HEADS)[...] = k_ref[:, sl]
            _head_rows(cv_ref, h, s_keys, A_HEADS)[...] = v_ref[:, sl]

    q_pos = j * qs + lax.broadcasted_iota(I32, (1, qs), 1)
    limit = ((q_pos >> 6) + 1) * CHUNK
    nsb = (j * qs + qs + kb - 1) // kb
    for n in range(1, s_keys // kb + 1):
        pl.when(nsb == n)(functools.partial(
            _dsa_select_attend, q_ref, qi_ref, mq_ref, o_ref, kb_scr, vt_scr, kib_scr, w_scr, score_scr,
            key_scr, mask_scr, n_keys=n * kb, qs=qs, topk=topk, limit=limit))


def _dsa_sample_kernel(q_ref, qi_ref, mq_ref, kc_ref, vc_ref, kic_ref, kn_ref, vn_ref, mkn_ref,
                       bufo_ref, bufk_ref, bufv_ref, bufi_ref,
                       o_ref, ck_ref, cv_ref, ci_ref,
                       kb_scr, vt_scr, kib_scr, w_scr, score_scr, key_scr, mask_scr,
                       vrow_scr, *, s_keys, qs, kb, topk, past):
    del bufo_ref, bufk_ref, bufv_ref, bufi_ref
    n_new = kn_ref.shape[0]
    live = past + n_new
    for h in range(A_HEADS):
        sl = slice(h * A_HEAD_DIM, (h + 1) * A_HEAD_DIM)
        kb_scr[0:past, sl] = _head_rows(kc_ref, h, past, A_HEADS)[...].astype(BF16)
        vrow_scr[0:past, sl] = _head_rows(vc_ref, h, past, A_HEADS)[...]
        _head_rows(ck_ref, h, n_new, A_HEADS)[...] = kn_ref[:, sl]
        _head_rows(cv_ref, h, n_new, A_HEADS)[...] = vn_ref[:, sl]
    ci_ref[...] = mkn_ref[:, 0:IDX_DIM]
    kb_scr[past:live, :] = kn_ref[...].astype(BF16)
    kib_scr[0:past, :] = kic_ref[...].astype(BF16)
    kib_scr[past:live, :] = mkn_ref[:, 0:IDX_DIM].astype(BF16)
    vrow_scr[past:live, :] = vn_ref[...]
    if s_keys > live:
        kb_scr[live:, :] = jnp.zeros((s_keys - live, A_WIDTH), BF16)
        kib_scr[live:, :] = jnp.zeros((s_keys - live, IDX_DIM), BF16)
        vrow_scr[live:, :] = jnp.zeros((s_keys - live, A_WIDTH), F32)
    for i in range(s_keys // kb):
        vt_scr[:, i * kb:(i + 1) * kb] = vrow_scr[i * kb:(i + 1) * kb, :].T.astype(BF16)

    limit = jnp.full((1, qs), live, I32)
    _dsa_select_attend(q_ref, qi_ref, mq_ref, o_ref, kb_scr, vt_scr, kib_scr, w_scr, score_scr,
                       key_scr, mask_scr, n_keys=s_keys, qs=qs, topk=topk, limit=limit)


def _dsa_scratch(s_keys, qs):
    return [
        pltpu.VMEM((s_keys, A_WIDTH), BF16),
        pltpu.VMEM((A_WIDTH, s_keys), BF16),
        pltpu.VMEM((s_keys, IDX_DIM), BF16),
        pltpu.VMEM((LANES, qs), F32),
        pltpu.VMEM((s_keys, qs), F32),
        pltpu.VMEM((s_keys, qs), I32),
        pltpu.VMEM((s_keys, qs), F32),
    ]


def _dsa_prompt(proj, out_k, out_v, out_ki, layer, *, nb, tq, qs, kb):
    nsteps = tq // qs
    qcol = lambda w, off: pl.BlockSpec((qs, w), lambda b, j: (b * nsteps + j, off // w))
    kcol = lambda w, off: pl.BlockSpec((tq, w), lambda b, j: (b, off // w))
    cache = lambda rows, w: pl.BlockSpec((rows, w), lambda b, j: (layer * nb + b, 0))
    kern = functools.partial(_dsa_prompt_kernel, s_keys=tq, qs=qs, kb=kb, topk=min(DSA_TOPK, tq // 4))
    return pl.pallas_call(
        kern,
        grid=(nb, nsteps),
        in_specs=[qcol(A_WIDTH, COL_AQ), qcol(IQ_WIDTH, COL_IQ), qcol(LANES, COL_MISC),
                  kcol(A_WIDTH, COL_AK), kcol(A_WIDTH, COL_AV), kcol(LANES, COL_MISC), ANY, ANY, ANY],
        out_specs=[pl.BlockSpec((qs, A_WIDTH), lambda b, j: (b * nsteps + j, 0)),
                   cache(tq * A_HEADS, A_HEAD_DIM), cache(tq * A_HEADS, A_HEAD_DIM), cache(tq, IDX_DIM)],
        out_shape=[jax.ShapeDtypeStruct((proj.shape[0], A_WIDTH), F32),
                   jax.ShapeDtypeStruct(out_k.shape, F32), jax.ShapeDtypeStruct(out_v.shape, F32),
                   jax.ShapeDtypeStruct(out_ki.shape, F32)],
        input_output_aliases={6: 1, 7: 2, 8: 3},
        scratch_shapes=_dsa_scratch(tq, qs),
        compiler_params=_cparams("parallel", "arbitrary"),
        name="dsa_prompt",
    )(proj, proj, proj, proj, proj, proj, out_k, out_v, out_ki)


def _dsa_sample(proj, cache_k, cache_v, cache_ki, buf, out_k, out_v, out_ki, layer,
                *, nb, ts, past, row0, kb):
    rb0 = row0 // ts
    live = past + ts
    s_keys = -(-live // kb) * kb
    qcol = lambda w, off: pl.BlockSpec((ts, w), lambda b, j: (rb0 + b, off // w))
    per_seq = lambda rows, w: pl.BlockSpec((rows, w), lambda b, j: (layer * nb + b, 0))
    kern = functools.partial(_dsa_sample_kernel, s_keys=s_keys, qs=ts, kb=kb,
                             topk=min(DSA_TOPK, live // 4), past=past)
    return pl.pallas_call(
        kern,
        grid=(nb, 1),
        in_specs=[qcol(A_WIDTH, COL_AQ), qcol(IQ_WIDTH, COL_IQ), qcol(LANES, COL_MISC),
                  per_seq(past * A_HEADS, A_HEAD_DIM), per_seq(past * A_HEADS, A_HEAD_DIM),
                  per_seq(past, IDX_DIM),
                  qcol(A_WIDTH, COL_AK), qcol(A_WIDTH, COL_AV), qcol(LANES, COL_MISC),
                  ANY, ANY, ANY, ANY],
        out_specs=[pl.BlockSpec((ts, A_WIDTH), lambda b, j: (rb0 + b, 0)),
                   per_seq(ts * A_HEADS, A_HEAD_DIM), per_seq(ts * A_HEADS, A_HEAD_DIM),
                   per_seq(ts, IDX_DIM)],
        out_shape=[jax.ShapeDtypeStruct(buf.shape, F32), jax.ShapeDtypeStruct(out_k.shape, F32),
                   jax.ShapeDtypeStruct(out_v.shape, F32), jax.ShapeDtypeStruct(out_ki.shape, F32)],
        input_output_aliases={9: 0, 10: 1, 11: 2, 12: 3},
        scratch_shapes=_dsa_scratch(s_keys, ts) + [pltpu.VMEM((s_keys, A_WIDTH), F32)],
        compiler_params=_cparams("parallel", "arbitrary"),
        name="dsa_sample",
    )(proj, proj, proj, cache_k, cache_v, cache_ki, proj, proj, proj, buf, out_k, out_v, out_ki)


BAND_QS_PROMPT = 256


def _band_bias_kernel(tab_ref, o_ref, *, qs, w):
    layer = pl.program_id(0)
    wp = -(-(w + qs) // LANES) * LANES
    jj = lax.broadcasted_iota(I32, (8, wp), 1)
    d = jnp.where(jj < w, jj, jj - wp)
    idx = jnp.clip(B_WIN - d, -REL_CLIP, REL_CLIP) + REL_CLIP
    q = lax.broadcasted_iota(I32, (qs, w), 0)
    jx = lax.broadcasted_iota(I32, (qs, w), 1)
    back = (q >> 6) + B_PREV_CHUNKS - (jx >> 6)
    allowed = (back >= 0) & (back <= B_PREV_CHUNKS)
    for h in range(B_HEADS):
        row = layer * B_HEADS + h

        def body(v, acc):
            return jnp.where(idx == v, tab_ref[row, v], acc)

        base = lax.fori_loop(0, 2 * REL_CLIP + 1, body, jnp.zeros((8, wp), F32))
        rows = jnp.broadcast_to(base[0:1, :], (qs, wp))
        toeplitz = pltpu.roll(rows, 0, 1, stride=1, stride_axis=0)
        o_ref[0, h] = jnp.where(allowed, toeplitz[:, 0:w], NEG_INF)


def _band_bias(b_rel, qs):
    depth = b_rel.shape[0]
    w = qs + B_WIN
    return pl.pallas_call(
        functools.partial(_band_bias_kernel, qs=qs, w=w),
        grid=(depth,),
        in_specs=[pl.BlockSpec(memory_space=pltpu.SMEM)],
        out_specs=pl.BlockSpec((1, B_HEADS, qs, w), lambda l: (l, 0, 0, 0)),
        out_shape=jax.ShapeDtypeStruct((depth, B_HEADS, qs, w), F32),
        compiler_params=_cparams("arbitrary"),
        name=f"band_bias_{qs}",
    )(b_rel.reshape(depth * B_HEADS, 2 * REL_CLIP + 1))


def _band_attend(q_ref, bias_ref, o_ref, kp_scr, vp_scr, start, valid, qs):
    w = qs + B_WIN
    for h in range(B_HEADS):
        sl = slice(h * B_HEAD_DIM, (h + 1) * B_HEAD_DIM)
        qh = q_ref[:, sl].astype(BF16)
        lg = _nt_dot(qh, kp_scr[pl.ds(start, w), sl]) * (B_HEAD_DIM ** -0.5) + bias_ref[0, h]
        if valid is not None:
            lg = jnp.where(valid, lg, NEG_INF)
        m = jnp.max(lg, axis=-1, keepdims=True)
        e = jnp.exp(lg - m)
        den = jnp.sum(e, axis=-1, keepdims=True)
        o = jnp.dot(e.astype(BF16), vp_scr[pl.ds(start, w), sl], preferred_element_type=F32)
        o_ref[:, sl] = o / den


def _band_prompt_kernel(q_ref, k_ref, v_ref, bias_ref, bufk_ref, bufv_ref, o_ref, ck_ref, cv_ref,
                        kp_scr, vp_scr, *, s_keys, qs, nkeep):
    del bufk_ref, bufv_ref
    j = pl.program_id(1)

    @pl.when(j == 0)
    def _():
        kp_scr[0:B_WIN, :] = jnp.zeros((B_WIN, B_WIDTH), BF16)
        vp_scr[0:B_WIN, :] = jnp.zeros((B_WIN, B_WIDTH), BF16)
        kp_scr[B_WIN:B_WIN + s_keys, :] = k_ref[...].astype(BF16)
        vp_scr[B_WIN:B_WIN + s_keys, :] = v_ref[...].astype(BF16)
        for h in range(B_HEADS):
            sl = slice(h * B_HEAD_DIM, (h + 1) * B_HEAD_DIM)
            _head_rows(ck_ref, h, nkeep, B_HEADS)[...] = k_ref[s_keys - nkeep:, sl]
            _head_rows(cv_ref, h, nkeep, B_HEADS)[...] = v_ref[s_keys - nkeep:, sl]

    start = pl.multiple_of(j * qs, qs)
    valid = lax.broadcasted_iota(I32, (qs, qs + B_WIN), 1) >= B_WIN - j * qs
    _band_attend(q_ref, bias_ref, o_ref, kp_scr, vp_scr, start, valid, qs)


def _band_sample_kernel(q_ref, kc_ref, vc_ref, kn_ref, vn_ref, bias_ref, bufo_ref, bufk_ref, bufv_ref,
                        o_ref, ck_ref, cv_ref, kp_scr, vp_scr, *, qs):
    del bufo_ref, bufk_ref, bufv_ref
    for h in range(B_HEADS):
        sl = slice(h * B_HEAD_DIM, (h + 1) * B_HEAD_DIM)
        kp_scr[0:B_WIN, sl] = _head_rows(kc_ref, h, B_WIN, B_HEADS)[...].astype(BF16)
        vp_scr[0:B_WIN, sl] = _head_rows(vc_ref, h, B_WIN, B_HEADS)[...].astype(BF16)
        _head_rows(ck_ref, h, qs, B_HEADS)[...] = kn_ref[:, sl]
        _head_rows(cv_ref, h, qs, B_HEADS)[...] = vn_ref[:, sl]
    kp_scr[B_WIN:B_WIN + qs, :] = kn_ref[...].astype(BF16)
    vp_scr[B_WIN:B_WIN + qs, :] = vn_ref[...].astype(BF16)
    _band_attend(q_ref, bias_ref, o_ref, kp_scr, vp_scr, 0, None, qs)


def _band_prompt(proj, bias, out_k, out_v, layer, *, nb, tq, qs):
    nsteps = tq // qs
    w = qs + B_WIN
    nkeep = min(B_WIN, tq)
    kcol = lambda off: pl.BlockSpec((tq, B_WIDTH), lambda b, j: (b, off // B_WIDTH))
    cache = pl.BlockSpec((nkeep * B_HEADS, B_HEAD_DIM), lambda b, j: (layer * nb + b, 0))
    return pl.pallas_call(
        functools.partial(_band_prompt_kernel, s_keys=tq, qs=qs, nkeep=nkeep),
        grid=(nb, nsteps),
        in_specs=[
            pl.BlockSpec((qs, B_WIDTH), lambda b, j: (b * nsteps + j, COL_BQ // B_WIDTH)),
            kcol(COL_BK), kcol(COL_BV),
            pl.BlockSpec((1, B_HEADS, qs, w), lambda b, j: (layer, 0, 0, 0)),
            ANY, ANY,
        ],
        out_specs=[pl.BlockSpec((qs, B_WIDTH), lambda b, j: (b * nsteps + j, 0)), cache, cache],
        out_shape=[jax.ShapeDtypeStruct((proj.shape[0], B_WIDTH), F32),
                   jax.ShapeDtypeStruct(out_k.shape, F32), jax.ShapeDtypeStruct(out_v.shape, F32)],
        input_output_aliases={4: 1, 5: 2},
        scratch_shapes=[pltpu.VMEM((B_WIN + tq, B_WIDTH), BF16),
                        pltpu.VMEM((B_WIN + tq, B_WIDTH), BF16)],
        compiler_params=_cparams("parallel", "arbitrary"),
        name="band_prompt",
    )(proj, proj, proj, bias, out_k, out_v)


def _band_sample(proj, cache_k, cache_v, bias, buf, out_k, out_v, layer, *, nb, ts, row0):
    rb0 = row0 // ts
    w = ts + B_WIN
    qcol = lambda off: pl.BlockSpec((ts, B_WIDTH), lambda b, j: (rb0 + b, off // B_WIDTH))
    per_seq = lambda rows: pl.BlockSpec((rows * B_HEADS, B_HEAD_DIM), lambda b, j: (layer * nb + b, 0))
    return pl.pallas_call(
        functools.partial(_band_sample_kernel, qs=ts),
        grid=(nb, 1),
        in_specs=[qcol(COL_BQ), per_seq(B_WIN), per_seq(B_WIN), qcol(COL_BK), qcol(COL_BV),
                  pl.BlockSpec((1, B_HEADS, ts, w), lambda b, j: (layer, 0, 0, 0)),
                  ANY, ANY, ANY],
        out_specs=[pl.BlockSpec((ts, B_WIDTH), lambda b, j: (rb0 + b, 0)), per_seq(ts), per_seq(ts)],
        out_shape=[jax.ShapeDtypeStruct(buf.shape, F32), jax.ShapeDtypeStruct(out_k.shape, F32),
                   jax.ShapeDtypeStruct(out_v.shape, F32)],
        input_output_aliases={6: 0, 7: 1, 8: 2},
        scratch_shapes=[pltpu.VMEM((w, B_WIDTH), BF16), pltpu.VMEM((w, B_WIDTH), BF16)],
        compiler_params=_cparams("parallel", "arbitrary"),
        name="band_sample",
    )(proj, cache_k, cache_v, proj, proj, bias, buf, out_k, out_v)


XP_OFF = 8
SSD_CHUNKS_PER_STEP = 4
_HEAD_SPREAD = np.zeros((LANES, C_WIDTH), np.float32)
for _h in range(C_HEADS):
    _HEAD_SPREAD[MISC_DT + _h, _h * C_HEAD_DIM:(_h + 1) * C_HEAD_DIM] = 1.0


def _ssd_kernel(*refs, zero_init, nsteps, cps):
    if zero_init:
        (z_ref, xbc_ref, misc_ref, cw_ref, cbias_ref, prm_ref, gn_ref, ex_ref, dskip_ref,
         bufh_ref, bufc_ref, o_ref, hout_ref, cnew_ref, st_scr, xp_scr, y_scr) = refs
    else:
        (z_ref, xbc_ref, misc_ref, h0_ref, cb0_ref, cw_ref, cbias_ref, prm_ref, gn_ref, ex_ref, dskip_ref,
         bufo_ref, bufh_ref, bufc_ref, o_ref, hout_ref, cnew_ref, st_scr, xp_scr, y_scr) = refs
        del bufo_ref
    del bufh_ref, bufc_ref
    c = pl.program_id(1)
    tail = CONV_W - 1
    rep = C_HEADS // C_GROUPS
    gw = C_WIDTH // C_GROUPS

    @pl.when(c == 0)
    def _():
        if zero_init:
            st_scr[...] = jnp.zeros(st_scr.shape, F32)
            xp_scr[XP_OFF - tail:XP_OFF, :] = jnp.zeros((tail, CONV_DIM), F32)
        else:
            for h in range(C_HEADS):
                st_scr[h // rep, :, (h % rep) * C_HEAD_DIM:(h % rep + 1) * C_HEAD_DIM] = h0_ref[0, h].T
            xp_scr[XP_OFF - tail:XP_OFF, :] = cb0_ref[0]

    def chunk(rows):
        xp_scr[XP_OFF:XP_OFF + CHUNK, :] = xbc_ref[rows, :]
        y = cbias_ref[...] + xp_scr[XP_OFF - tail:XP_OFF - tail + CHUNK, :] * cw_ref[0:1, :]
        for t in range(1, CONV_W):
            y = y + xp_scr[XP_OFF - tail + t:XP_OFF - tail + t + CHUNK, :] * cw_ref[t:t + 1, :]
        new_tail = xp_scr[XP_OFF + CHUNK - tail:XP_OFF + CHUNK, :]
        xp_scr[XP_OFF - tail:XP_OFF, :] = new_tail
        cnew_ref[0] = new_tail
        xc = y * (1.0 / (1.0 + jnp.exp(-y)))
        xs = xc[:, 0:C_WIDTH]

        pre = misc_ref[rows, :] + prm_ref[0:1, :]
        dt = jnp.maximum(pre, 0.0) + jnp.log(1.0 + jnp.exp(-jnp.abs(pre)))
        ad = dt * (-jnp.exp(prm_ref[1:2, :]))
        li = lax.broadcasted_iota(I32, (CHUNK, CHUNK), 0)
        si = lax.broadcasted_iota(I32, (CHUNK, CHUNK), 1)
        causal = li >= si
        tri = causal.astype(BF16)
        acs = sum(jnp.dot(tri, part, preferred_element_type=F32) for part in _split3(ad))
        acs_t = acs.T
        ex = ex_ref[...]
        spread = lambda v: sum(jnp.dot(part, ex, preferred_element_type=F32) for part in _split3(v))
        dt_x = spread(dt)
        acs_x = spread(acs)
        acs_last_x = acs_x[CHUNK - 1:CHUNK, :]
        xd_all = xs * dt_x
        xd_b = xd_all.astype(BF16)
        xde_b = (xd_all * jnp.exp(acs_last_x - acs_x)).astype(BF16)
        grow_x = jnp.exp(acs_x)
        chunk_decay_x = jnp.exp(acs_last_x)
        head_of_lane = lax.broadcasted_iota(I32, (CHUNK, gw), 1) >> 6

        for g in range(C_GROUPS):
            gl = slice(g * gw, (g + 1) * gw)
            b_g = xc[:, C_WIDTH + g * C_STATE:C_WIDTH + (g + 1) * C_STATE]
            c_g = xc[:, C_WIDTH + (C_GROUPS + g) * C_STATE:C_WIDTH + (C_GROUPS + g + 1) * C_STATE]
            b_gt = b_g.T.astype(BF16)
            c_gb = c_g.astype(BF16)
            cb = jnp.dot(c_gb, b_gt, preferred_element_type=F32)
            st = st_scr[g]
            y_g = jnp.dot(c_gb, st.astype(BF16), preferred_element_type=F32) * grow_x[:, gl]
            xd_g = xd_b[:, gl]
            for r in range(rep):
                col = MISC_DT + g * rep + r
                seg = acs[:, col:col + 1] - acs_t[col:col + 1, :]
                decay_in = jnp.where(causal, jnp.exp(jnp.where(causal, seg, 0.0)), 0.0)
                t_r = jnp.dot((cb * decay_in).astype(BF16), xd_g, preferred_element_type=F32)
                y_g = y_g + jnp.where(head_of_lane == r, t_r, 0.0)
            y_scr[:, gl] = y_g + dskip_ref[:, gl] * xs[:, gl]
            st_scr[g] = st * chunk_decay_x[:, gl] + jnp.dot(b_gt, xde_b[:, gl], preferred_element_type=F32)

        z = z_ref[rows, :]
        gate = y_scr[...] * (z * (1.0 / (1.0 + jnp.exp(-z))))
        for g in range(C_GROUPS):
            gg = gate[:, g * gw:(g + 1) * gw]
            ms = jnp.mean(gg * gg, axis=-1, keepdims=True)
            o_ref[rows, g * gw:(g + 1) * gw] = gg * lax.rsqrt(ms + EPS) * gn_ref[:, g * gw:(g + 1) * gw]

    for i in range(cps):
        chunk(slice(i * CHUNK, (i + 1) * CHUNK))

    @pl.when(c == nsteps - 1)
    def _():
        for h in range(C_HEADS):
            hout_ref[0, h] = st_scr[h // rep, :, (h % rep) * C_HEAD_DIM:(h % rep + 1) * C_HEAD_DIM].T


def _ssd(proj, conv_w, conv_b, prm, gate_norm, out_h, out_c, layer, *, nb, nchunks, row0, state=None):
    cps = SSD_CHUNKS_PER_STEP if nchunks % SSD_CHUNKS_PER_STEP == 0 else 1
    nsteps = nchunks // cps
    rt = cps * CHUNK
    rb0 = row0 // rt
    row = lambda b, c: rb0 + b * nsteps + c
    const = lambda shape: pl.BlockSpec(shape, lambda b, c: tuple(0 for _ in shape))
    per_seq_h = pl.BlockSpec((1, C_HEADS, C_HEAD_DIM, C_STATE), lambda b, c: (layer * nb + b, 0, 0, 0))
    per_seq_c = pl.BlockSpec((1, CONV_W - 1, CONV_DIM), lambda b, c: (layer * nb + b, 0, 0))
    in_specs = [
        pl.BlockSpec((rt, C_WIDTH), lambda b, c: (row(b, c), COL_CZ // C_WIDTH)),
        pl.BlockSpec((rt, CONV_DIM), lambda b, c: (row(b, c), COL_CXBC // CONV_DIM)),
        pl.BlockSpec((rt, LANES), lambda b, c: (row(b, c), COL_MISC // LANES)),
    ]
    args = [proj, proj, proj]
    if state is not None:
        h0, cbuf0, buf = state
        in_specs += [per_seq_h, per_seq_c]
        args += [h0, cbuf0]
    in_specs += [const((CONV_W, CONV_DIM)), const((1, CONV_DIM)), const((8, LANES)), const((1, C_WIDTH)),
                 const((LANES, C_WIDTH)), const((1, C_WIDTH))]
    args += [conv_w, conv_b.reshape(1, CONV_DIM), prm[0], gate_norm.reshape(1, C_WIDTH),
             jnp.asarray(_HEAD_SPREAD, BF16), prm[1]]
    aliases = {}
    if state is not None:
        in_specs.append(ANY)
        args.append(buf)
        aliases[len(args) - 1] = 0
    in_specs += [ANY, ANY]
    args += [out_h, out_c]
    aliases[len(args) - 2] = 1
    aliases[len(args) - 1] = 2
    return pl.pallas_call(
        functools.partial(_ssd_kernel, zero_init=state is None, nsteps=nsteps, cps=cps),
        grid=(nb, nsteps),
        in_specs=in_specs,
        out_specs=[pl.BlockSpec((rt, C_WIDTH), lambda b, c: (row(b, c), 0)), per_seq_h, per_seq_c],
        out_shape=[
            jax.ShapeDtypeStruct((proj.shape[0], C_WIDTH), F32),
            jax.ShapeDtypeStruct(out_h.shape, F32),
            jax.ShapeDtypeStruct(out_c.shape, F32),
        ],
        input_output_aliases=aliases,
        scratch_shapes=[pltpu.VMEM((C_GROUPS, C_STATE, C_WIDTH // C_GROUPS), F32),
                        pltpu.VMEM((XP_OFF + CHUNK, CONV_DIM), F32),
                        pltpu.VMEM((CHUNK, C_WIDTH), F32)],
        compiler_params=_cparams("parallel", "arbitrary"),
        name="ssd_prompt" if state is None else "ssd_sample",
    )(*args)


def _outproj_kernel(x_ref, oa_ref, ob_ref, oc_ref, w_ref, o_ref):
    acc = x_ref[...]
    acc = acc + jnp.dot(oa_ref[...].astype(BF16), w_ref[0:A_WIDTH, :], preferred_element_type=F32)
    acc = acc + jnp.dot(ob_ref[...].astype(BF16), w_ref[A_WIDTH:A_WIDTH + B_WIDTH, :],
                        preferred_element_type=F32)
    acc = acc + jnp.dot(oc_ref[...].astype(BF16), w_ref[A_WIDTH + B_WIDTH:, :],
                        preferred_element_type=F32)
    o_ref[...] = acc


def _out_proj(x, oa, ob, oc, w_bf16):
    t, d = x.shape
    tm = _row_tile(t, 512)
    rows = lambda w: pl.BlockSpec((tm, w), lambda i: (i, 0))
    return pl.pallas_call(
        _outproj_kernel,
        grid=(t // tm,),
        in_specs=[rows(d), rows(A_WIDTH), rows(B_WIDTH), rows(C_WIDTH),
                  pl.BlockSpec((d, d), lambda i: (0, 0))],
        out_specs=rows(d),
        out_shape=jax.ShapeDtypeStruct((t, d), F32),
        compiler_params=_cparams("parallel"),
        name="out_proj",
    )(x, oa, ob, oc, w_bf16)


def _ffn_kernel(x_ref, nw_ref, wu_ref, wd_ref, o_ref, h_scr):
    @pl.when(pl.program_id(1) == 0)
    def _():
        x = x_ref[...]
        ms = jnp.mean(x * x, axis=-1, keepdims=True)
        h_scr[...] = (x * lax.rsqrt(ms + EPS) * nw_ref[...]).astype(BF16)
        o_ref[...] = x

    u = jnp.maximum(jnp.dot(h_scr[...], wu_ref[...], preferred_element_type=F32), 0.0)
    o_ref[...] += jnp.dot((u * u).astype(BF16), wd_ref[...], preferred_element_type=F32)


def _ffn(x, norm_w, wu_bf16, wd_bf16):
    t, d = x.shape
    ff = wu_bf16.shape[1]
    tm = _row_tile(t, 1024)
    tf = 512
    return pl.pallas_call(
        _ffn_kernel,
        grid=(t // tm, ff // tf),
        in_specs=[
            pl.BlockSpec((tm, d), lambda i, f: (i, 0)),
            pl.BlockSpec((1, d), lambda i, f: (0, 0)),
            pl.BlockSpec((d, tf), lambda i, f: (0, f)),
            pl.BlockSpec((tf, d), lambda i, f: (f, 0)),
        ],
        out_specs=pl.BlockSpec((tm, d), lambda i, f: (i, 0)),
        out_shape=jax.ShapeDtypeStruct((t, d), F32),
        scratch_shapes=[pltpu.VMEM((tm, d), BF16)],
        compiler_params=_cparams("parallel", "arbitrary"),
        name="ffn",
    )(x, norm_w.reshape(1, d), wu_bf16, wd_bf16)


def _norm_kernel(x_ref, nw_ref, o_ref):
    x = x_ref[...]
    ms = jnp.mean(x * x, axis=-1, keepdims=True)
    o_ref[...] = x * lax.rsqrt(ms + EPS) * nw_ref[...]


def _final_norm(x, norm_w, row0, nrows):
    d = x.shape[1]
    tm = _row_tile(nrows, 512)
    assert row0 % tm == 0
    return pl.pallas_call(
        _norm_kernel,
        grid=(nrows // tm,),
        in_specs=[pl.BlockSpec((tm, d), lambda i: (row0 // tm + i, 0)),
                  pl.BlockSpec((1, d), lambda i: (0, 0))],
        out_specs=pl.BlockSpec((tm, d), lambda i: (i, 0)),
        out_shape=jax.ShapeDtypeStruct((nrows, d), F32),
        compiler_params=_cparams("parallel"),
        name="final_norm",
    )(x, norm_w.reshape(1, d))


def _lane_row(v, off):
    return jnp.zeros((LANES,), F32).at[off:off + v.shape[0]].set(v)


def _trunk(x_prompt, x_sample, cache_a_k, cache_a_v, cache_a_kidx, cache_b_k, cache_b_v,
           state_ssm, state_conv, norm1, w_in, w_out, b_rel, conv_w, conv_b, dt_bias,
           a_log, d_skip, gate_norm, norm2, w_up, w_down, final_norm):
    bp, tp, d = x_prompt.shape
    bs, ts, _ = x_sample.shape
    depth = w_in.shape[0]
    past = cache_a_k.shape[2]
    nbuf = cache_b_k.shape[2]
    n_p = bp * tp
    n_s = bs * ts
    assert ts == CHUNK and tp % DSA_KB_PROMPT == 0 and tp % BAND_QS_PROMPT == 0 and nbuf == B_WIN
    qs_p = LANES
    nkeep = min(B_WIN, tp)

    x = jnp.concatenate([x_prompt.reshape(n_p, d), x_sample.reshape(n_s, d)], axis=0)

    pos = jnp.concatenate([jnp.tile(jnp.arange(tp), bp), jnp.tile(past + jnp.arange(ts), bs)])
    tab_a = _rope_tables(pos, A_HEAD_DIM)
    tab_i = _rope_tables(pos, IDX_DIM)
    bias_p = _band_bias(b_rel, BAND_QS_PROMPT)
    bias_s = _band_bias(b_rel, ts)
    w_in_r = _regroup_w_in(w_in)
    cak = cache_a_k.reshape(depth * bs * past * A_HEADS, A_HEAD_DIM)
    cav = cache_a_v.reshape(depth * bs * past * A_HEADS, A_HEAD_DIM)
    caki = cache_a_kidx.reshape(depth * bs * past, IDX_DIM)
    cbk = cache_b_k.reshape(depth * bs * nbuf * B_HEADS, B_HEAD_DIM)
    cbv = cache_b_v.reshape(depth * bs * nbuf * B_HEADS, B_HEAD_DIM)
    h0 = state_ssm.reshape(depth * bs, C_HEADS, C_HEAD_DIM, C_STATE)
    cbuf0 = state_conv.reshape(depth * bs, CONV_W - 1, CONV_DIM)

    p_ak = _uninit((depth * bp * tp * A_HEADS, A_HEAD_DIM))
    p_av = _uninit((depth * bp * tp * A_HEADS, A_HEAD_DIM))
    p_aki = _uninit((depth * bp * tp, IDX_DIM))
    p_bk = _uninit((depth * bp * nkeep * B_HEADS, B_HEAD_DIM))
    p_bv = _uninit((depth * bp * nkeep * B_HEADS, B_HEAD_DIM))
    p_h = _uninit((depth * bp, C_HEADS, C_HEAD_DIM, C_STATE))
    p_c = _uninit((depth * bp, CONV_W - 1, CONV_DIM))
    s_ak = _uninit((depth * bs * ts * A_HEADS, A_HEAD_DIM))
    s_av = _uninit((depth * bs * ts * A_HEADS, A_HEAD_DIM))
    s_aki = _uninit((depth * bs * ts, IDX_DIM))
    s_bk = _uninit((depth * bs * ts * B_HEADS, B_HEAD_DIM))
    s_bv = _uninit((depth * bs * ts * B_HEADS, B_HEAD_DIM))
    s_h = _uninit((depth * bs, C_HEADS, C_HEAD_DIM, C_STATE))
    s_c = _uninit((depth * bs, CONV_W - 1, CONV_DIM))

    for l in range(depth):
        proj = _in_proj(x, norm1[l], w_in_r, l, tab_a, tab_i)

        oa, p_ak, p_av, p_aki = _dsa_prompt(proj, p_ak, p_av, p_aki, l, nb=bp, tq=tp, qs=qs_p,
                                            kb=DSA_KB_PROMPT)
        oa, s_ak, s_av, s_aki = _dsa_sample(proj, cak, cav, caki, oa, s_ak, s_av, s_aki, l,
                                            nb=bs, ts=ts, past=past, row0=n_p, kb=DSA_KB_SAMPLE)
        ob, p_bk, p_bv = _band_prompt(proj, bias_p, p_bk, p_bv, l, nb=bp, tq=tp, qs=BAND_QS_PROMPT)
        ob, s_bk, s_bv = _band_sample(proj, cbk, cbv, bias_s, ob, s_bk, s_bv, l, nb=bs, ts=ts, row0=n_p)
        rows = jnp.zeros((8, LANES), F32)
        rows = rows.at[0].set(_lane_row(dt_bias[l], MISC_DT)).at[1].set(_lane_row(a_log[l], MISC_DT))
        prm = (rows, jnp.repeat(d_skip[l], C_HEAD_DIM).reshape(1, C_WIDTH))
        oc, p_h, p_c = _ssd(proj, conv_w[l], conv_b[l], prm, gate_norm[l], p_h, p_c, l,
                            nb=bp, nchunks=tp // CHUNK, row0=0)
        oc, s_h, s_c = _ssd(proj, conv_w[l], conv_b[l], prm, gate_norm[l], s_h, s_c, l,
                            nb=bs, nchunks=1, row0=n_p, state=(h0, cbuf0, oc))

        x = _out_proj(x, oa, ob, oc, w_out[l].astype(BF16))
        x = _ffn(x, norm2[l], w_up[l].astype(BF16), w_down[l].astype(BF16))

    y_prompt = _final_norm(x, final_norm, 0, n_p).reshape(bp, tp, d)
    y_sample = _final_norm(x, final_norm, n_p, n_s).reshape(bs, ts, d)
    return (y_prompt, y_sample,
            p_ak.reshape(depth, bp, tp, A_HEADS, A_HEAD_DIM),
            p_av.reshape(depth, bp, tp, A_HEADS, A_HEAD_DIM),
            p_aki.reshape(depth, bp, tp, IDX_DIM),
            p_bk.reshape(depth, bp, nkeep, B_HEADS, B_HEAD_DIM),
            p_bv.reshape(depth, bp, nkeep, B_HEADS, B_HEAD_DIM),
            p_h.reshape(depth, bp, C_HEADS, C_HEAD_DIM, C_STATE),
            p_c.reshape(depth, bp, CONV_W - 1, CONV_DIM),
            s_ak.reshape(depth, bs, ts, A_HEADS, A_HEAD_DIM),
            s_av.reshape(depth, bs, ts, A_HEADS, A_HEAD_DIM),
            s_aki.reshape(depth, bs, ts, IDX_DIM),
            s_bk.reshape(depth, bs, ts, B_HEADS, B_HEAD_DIM),
            s_bv.reshape(depth, bs, ts, B_HEADS, B_HEAD_DIM),
            s_h.reshape(depth, bs, C_HEADS, C_HEAD_DIM, C_STATE),
            s_c.reshape(depth, bs, CONV_W - 1, CONV_DIM))


def kernel(x_prompt, x_sample, cache_a_k, cache_a_v, cache_a_kidx, cache_b_k, cache_b_v, state_ssm, state_conv, norm1, w_in, w_out, b_rel, conv_w, conv_b, dt_bias, a_log, d_skip, gate_norm, norm2, w_up, w_down, final_norm):
    return _trunk(x_prompt, x_sample, cache_a_k, cache_a_v, cache_a_kidx, cache_b_k, cache_b_v,
                  state_ssm, state_conv, norm1, w_in, w_out, b_rel, conv_w, conv_b, dt_bias,
                  a_log, d_skip, gate_norm, norm2, w_up, w_down, final_norm)
```

```python
import functools

import jax
import jax.numpy as jnp
import numpy as np
from jax import lax
from jax.experimental import pallas as pl
from jax.experimental.pallas import tpu as pltpu

F32 = jnp.float32
BF16 = jnp.bfloat16
I32 = jnp.int32

D_MODEL = 2048
CHUNK = 64
A_HEADS = 4
A_HEAD_DIM = 128
A_WIDTH = A_HEADS * A_HEAD_DIM
IDX_HEADS = 16
IDX_DIM = 64
IQ_WIDTH = IDX_HEADS * IDX_DIM
DSA_TOPK = 256
B_HEADS = 4
B_HEAD_DIM = 128
B_WIDTH = B_HEADS * B_HEAD_DIM
B_PREV_CHUNKS = 8
B_WIN = B_PREV_CHUNKS * CHUNK
REL_CLIP = 128
C_WIDTH = 1024
C_HEAD_DIM = 64
C_HEADS = C_WIDTH // C_HEAD_DIM
C_GROUPS = 4
C_STATE = 128
CONV_W = 4
CONV_DIM = C_WIDTH + 2 * C_GROUPS * C_STATE
D_FF = 4 * D_MODEL
ROPE_THETA = 500000.0
EPS = 1e-5

LANES = 128
MXU_N = 256
INT_MIN = -(2 ** 31)
NEG_INF = float("-inf")

COL_AQ = 0
COL_AK = 512
COL_IQ = 1024
COL_AV = 2048
COL_BQ = 2560
COL_BK = 3072
COL_BV = 3584
COL_CXBC = 4096
COL_CZ = 6144
COL_MISC = 7168
MISC_IW = 64
MISC_DT = 80
IN_COLS_PAD = 7680
IN_TN = 1280
SRC_AQ, SRC_AV, SRC_IQ, SRC_IK, SRC_BQ, SRC_CZ, SRC_CXBC, SRC_DT, SRC_END = (
    0, 1024, 1536, 2560, 2640, 4176, 5200, 7248, 7264)

VMEM_LIMIT = 56 * 1024 * 1024
ANY = pl.BlockSpec(memory_space=pl.ANY)


def _cparams(*sem):
    return pltpu.CompilerParams(dimension_semantics=sem, vmem_limit_bytes=VMEM_LIMIT)


def _row_tile(t, cap):
    tm = cap
    while t % tm:
        tm //= 2
    return tm


def _nt_dot(a, b):
    return lax.dot_general(a, b, (((1,), (1,)), ((), ())), preferred_element_type=F32)


def _split3(x):
    hi = x.astype(BF16)
    r1 = x - hi.astype(F32)
    mid = r1.astype(BF16)
    lo = (r1 - mid.astype(F32)).astype(BF16)
    return hi, mid, lo


def _head_rows(ref, h, n, heads):
    return ref.at[pl.ds(h, n, stride=heads), :]


def _uninit_kernel(o_ref):
    del o_ref


def _uninit(shape, dtype=F32):
    return pl.pallas_call(_uninit_kernel, out_specs=ANY, out_shape=jax.ShapeDtypeStruct(shape, dtype),
                          name="alloc")()


def _regroup_kernel(w_ref, o_ref):
    def put(dst, src, width):
        o_ref[0, :, dst:dst + width] = w_ref[0, :, src:src + width].astype(BF16)

    rows = w_ref.shape[1]
    put(COL_AQ, SRC_AQ, 2 * A_WIDTH)
    put(COL_IQ, SRC_IQ, IQ_WIDTH)
    put(COL_AV, SRC_AV, A_WIDTH)
    put(COL_BQ, SRC_BQ, 3 * B_WIDTH)
    put(COL_CXBC, SRC_CXBC, CONV_DIM)
    put(COL_CZ, SRC_CZ, C_WIDTH)
    o_ref[0, :, COL_MISC:] = jnp.zeros((rows, IN_COLS_PAD - COL_MISC), BF16)
    put(COL_MISC, SRC_IK, IDX_DIM + IDX_HEADS)
    put(COL_MISC + MISC_DT, SRC_DT, C_HEADS)


def _regroup_w_in(w_in):
    depth, d, n = w_in.shape
    assert n == SRC_END
    tk = 256
    return pl.pallas_call(
        _regroup_kernel,
        grid=(depth, d // tk),
        in_specs=[pl.BlockSpec((1, tk, n), lambda l, i: (l, i, 0))],
        out_specs=pl.BlockSpec((1, tk, IN_COLS_PAD), lambda l, i: (l, i, 0)),
        out_shape=jax.ShapeDtypeStruct((depth, d, IN_COLS_PAD), BF16),
        compiler_params=_cparams("parallel", "parallel"),
        name="regroup_w_in",
    )(w_in)


def _rope_block(x, c, sm, sp, half):
    return x * c + pltpu.roll(x, LANES - half, 1) * sm + pltpu.roll(x, half, 1) * sp


def _inproj_kernel(x_ref, nw_ref, w_ref, ta_ref, ti_ref, o_ref, h_scr):
    j = pl.program_id(1)

    def project(tile):
        if tile == 0:
            x = x_ref[...]
            ms = jnp.mean(x * x, axis=-1, keepdims=True)
            h_scr[...] = (x * lax.rsqrt(ms + EPS) * nw_ref[...]).astype(BF16)
        for pair in range(IN_TN // MXU_N):
            cols = slice(pair * MXU_N, (pair + 1) * MXU_N)
            both = jnp.dot(h_scr[...], w_ref[0, :, cols], preferred_element_type=F32)
            for half in range(MXU_N // LANES):
                blk = pair * (MXU_N // LANES) + half
                sl = slice(blk * LANES, (blk + 1) * LANES)
                acc = both[:, half * LANES:(half + 1) * LANES]
                col = -1 if tile is None else tile * IN_TN + blk * LANES
                if 0 <= col < COL_IQ:
                    acc = _rope_block(acc, ta_ref[0], ta_ref[1], ta_ref[2], A_HEAD_DIM // 8)
                elif COL_IQ <= col < COL_AV:
                    acc = _rope_block(acc, ti_ref[0], ti_ref[1], ti_ref[2], IDX_DIM // 8)
                elif col == COL_MISC:
                    lane = lax.broadcasted_iota(I32, acc.shape, 1)
                    rot = _rope_block(acc, ti_ref[0], ti_ref[1], ti_ref[2], IDX_DIM // 8)
                    acc = jnp.where(lane < IDX_DIM, rot, acc)
                o_ref[:, sl] = acc

    rope_tiles = [t for t in range(IN_COLS_PAD // IN_TN)
                  if t * IN_TN < COL_AV or t * IN_TN <= COL_MISC < (t + 1) * IN_TN]
    plain = j >= 0
    for tile in rope_tiles:
        pl.when(j == tile)(functools.partial(project, tile))
        plain = plain & (j != tile)
    pl.when(plain)(functools.partial(project, None))


def _in_proj(x, norm_w, w_all, layer, tab_a, tab_i):
    t, d = x.shape
    n = w_all.shape[2]
    tm = _row_tile(t, 1024)
    tab = pl.BlockSpec((3, tm, LANES), lambda i, j: (0, i, 0))
    return pl.pallas_call(
        _inproj_kernel,
        grid=(t // tm, n // IN_TN),
        in_specs=[
            pl.BlockSpec((tm, d), lambda i, j: (i, 0)),
            pl.BlockSpec((1, d), lambda i, j: (0, 0)),
            pl.BlockSpec((1, d, IN_TN), lambda i, j: (layer, 0, j)),
            tab, tab,
        ],
        out_specs=pl.BlockSpec((tm, IN_TN), lambda i, j: (i, j)),
        out_shape=jax.ShapeDtypeStruct((t, n), F32),
        scratch_shapes=[pltpu.VMEM((tm, d), BF16)],
        compiler_params=_cparams("parallel", "arbitrary"),
        name="in_proj",
    )(x, norm_w.reshape(1, d), w_all, tab_a, tab_i)


def _rope_tables(pos, head_dim):
    rot = head_dim // 4
    half = rot // 2
    inv = ROPE_THETA ** (-jnp.arange(half, dtype=F32) * 2.0 / rot)
    ang = pos.astype(F32)[:, None] * inv[None, :]
    cos, sin = jnp.cos(ang), jnp.sin(ang)
    n = pos.shape[0]
    rest = head_dim - rot
    c = jnp.concatenate([cos, cos, jnp.ones((n, rest), F32)], axis=1)
    sm = jnp.concatenate([-sin, jnp.zeros((n, half + rest), F32)], axis=1)
    sp = jnp.concatenate([jnp.zeros((n, half), F32), sin, jnp.zeros((n, rest), F32)], axis=1)
    tab = jnp.stack([c, sm, sp], axis=0)
    return jnp.tile(tab, (1, 1, LANES // head_dim))


TIE_BLOCK = 64
SUB = 128
DSA_KB_PROMPT = 256
DSA_KB_SAMPLE = 384
ATT_BLOCK = 256


def _tree_sum(parts):
    while len(parts) > 1:
        parts = [parts[i] + parts[i + 1] for i in range(0, len(parts) - 1, 2)] + (
            parts[-1:] if len(parts) % 2 else [])
    return parts[0]


def _count_keys(key_scr, n_keys, qs, pred):
    blocks = []
    for u in range(n_keys // SUB):
        hit = jnp.where(pred(key_scr[u * SUB:(u + 1) * SUB, :]), 1, 0).astype(I32)
        blocks.append(_tree_sum([hit[i * 8:(i + 1) * 8, :] for i in range(SUB // 8)]))
    return jnp.sum(_tree_sum(blocks), axis=0, keepdims=True)


def _kth_largest(key_scr, n_keys, qs, k):
    count = functools.partial(_count_keys, key_scr, n_keys, qs)
    ans0 = jnp.where(count(lambda x: x >= 0) >= k, 0, INT_MIN).astype(I32)

    def bit_step(i, ans):
        cand = ans | (jnp.int32(1) << (30 - i))
        return jnp.where(count(lambda x: x >= cand) >= k, cand, ans)

    return lax.fori_loop(0, 31, bit_step, ans0)


def _dsa_select_attend(q_ref, qi_ref, mq_ref, o_ref, kb_scr, vt_scr, kib_scr, w_scr, score_scr,
                       key_scr, mask_scr, *, n_keys, qs, topk, limit):
    w_scr[...] = mq_ref[...].T * (IDX_HEADS ** -0.5 * IDX_DIM ** -0.5)

    kib = kib_scr[0:n_keys, :]
    for h in range(IDX_HEADS):
        qh = qi_ref[:, h * IDX_DIM:(h + 1) * IDX_DIM].astype(BF16)
        term = jnp.maximum(_nt_dot(kib, qh), 0.0) * w_scr[MISC_IW + h:MISC_IW + h + 1, :]
        if h == 0:
            score_scr[0:n_keys, :] = term
        else:
            score_scr[0:n_keys, :] += term
    sc = score_scr[0:n_keys, :]
    sc = jnp.where(sc == 0.0, 0.0, sc)
    bits = lax.bitcast_convert_type(sc, I32)
    key = bits ^ ((bits >> 31) & 0x7FFFFFFF)
    s_idx = lax.broadcasted_iota(I32, (n_keys, qs), 0)
    key_scr[0:n_keys, :] = jnp.where(s_idx < limit, key, INT_MIN)

    thr = _kth_largest(key_scr, n_keys, qs, topk)
    cnt_gt = _count_keys(key_scr, n_keys, qs, lambda x: x > thr)
    cnt_eq = _count_keys(key_scr, n_keys, qs, lambda x: x == thr)
    room = topk - cnt_gt
    live = thr > INT_MIN
    all_ties = (cnt_eq <= room) & live
    key = key_scr[0:n_keys, :]
    mask_scr[0:n_keys, :] = jnp.where((key > thr) | ((key == thr) & all_ties), 0.0, NEG_INF)
    need_ties = jnp.max(((cnt_eq > room) & live).astype(I32)) > 0

    @pl.when(need_ties)
    def _():
        tri = (lax.broadcasted_iota(I32, (TIE_BLOCK, TIE_BLOCK), 0)
               >= lax.broadcasted_iota(I32, (TIE_BLOCK, TIE_BLOCK), 1)).astype(BF16)

        def blk(b, carry):
            rows = pl.ds(pl.multiple_of(b * TIE_BLOCK, TIE_BLOCK), TIE_BLOCK)
            kblk = key_scr[rows, :]
            eqb = (kblk == thr) & live
            eqf = eqb.astype(F32)
            prefix = jnp.dot(tri, eqf.astype(BF16), preferred_element_type=F32) + carry
            keep = (kblk > thr) | (eqb & (prefix <= room.astype(F32)))
            mask_scr[rows, :] = jnp.where(keep, 0.0, NEG_INF)
            return carry + jnp.sum(eqf, axis=0, keepdims=True)

        lax.fori_loop(0, n_keys // TIE_BLOCK, blk, jnp.zeros((1, qs), F32))

    ab = ATT_BLOCK if n_keys % ATT_BLOCK == 0 else SUB
    for h in range(A_HEADS):
        sl = slice(h * A_HEAD_DIM, (h + 1) * A_HEAD_DIM)
        qh = q_ref[:, sl].astype(BF16)
        parts = []
        for b in range(n_keys // ab):
            rows = slice(b * ab, (b + 1) * ab)
            lg = _nt_dot(kb_scr[rows, sl], qh) * (A_HEAD_DIM ** -0.5) + mask_scr[rows, :]
            m_b = jnp.max(lg, axis=0, keepdims=True)
            e = jnp.exp(lg - jnp.where(m_b == NEG_INF, 0.0, m_b))
            den_b = jnp.sum(e, axis=0, keepdims=True)
            o_b = jnp.dot(vt_scr[sl, rows], e.astype(BF16), preferred_element_type=F32)
            parts.append((m_b, den_b, o_b))
        m = functools.reduce(jnp.maximum, [p[0] for p in parts])
        den = jnp.zeros((1, qs), F32)
        o_t = jnp.zeros((A_HEAD_DIM, qs), F32)
        for m_b, den_b, o_b in parts:
            wgt = jnp.exp(m_b - m)
            den = den + wgt * den_b
            o_t = o_t + wgt * o_b
        o_ref[:, sl] = (o_t / den).T


def _dsa_prompt_kernel(q_ref, qi_ref, mq_ref, k_ref, v_ref, mk_ref, bufk_ref, bufv_ref, bufi_ref,
                       o_ref, ck_ref, cv_ref, ci_ref,
                       kb_scr, vt_scr, kib_scr, w_scr, score_scr, key_scr, mask_scr,
                       *, s_keys, qs, kb, topk):
    del bufk_ref, bufv_ref, bufi_ref
    j = pl.program_id(1)

    @pl.when(j == 0)
    def _():
        kb_scr[...] = k_ref[...].astype(BF16)
        kib_scr[...] = mk_ref[:, 0:IDX_DIM].astype(BF16)
        for i in range(s_keys // kb):
            vt_scr[:, i * kb:(i + 1) * kb] = v_ref[i * kb:(i + 1) * kb, :].T.astype(BF16)
        ci_ref[...] = mk_ref[:, 0:IDX_DIM]
        for h in range(A_HEADS):
            sl = slice(h * A_HEAD_DIM, (h + 1) * A_HEAD_DIM)
            _head_rows(ck_ref, h, s_keys, A_HEADS)[...] = k_ref[:, sl]
            _head_rows(cv_ref, h, s_keys, A_HEADS)[...] = v_ref[:, sl]

    q_pos = j * qs + lax.broadcasted_iota(I32, (1, qs), 1)
    limit = ((q_pos >> 6) + 1) * CHUNK
    nsb = (j * qs + qs + kb - 1) // kb
    for n in range(1, s_keys // kb + 1):
        pl.when(nsb == n)(functools.partial(
            _dsa_select_attend, q_ref, qi_ref, mq_ref, o_ref, kb_scr, vt_scr, kib_scr, w_scr, score_scr,
            key_scr, mask_scr, n_keys=n * kb, qs=qs, topk=topk, limit=limit))


def _dsa_sample_kernel(q_ref, qi_ref, mq_ref, kc_ref, vc_ref, kic_ref, kn_ref, vn_ref, mkn_ref,
                       bufo_ref, bufk_ref, bufv_ref, bufi_ref,
                       o_ref, ck_ref, cv_ref, ci_ref,
                       kb_scr, vt_scr, kib_scr, w_scr, score_scr, key_scr, mask_scr,
                       vrow_scr, *, s_keys, qs, kb, topk, past):
    del bufo_ref, bufk_ref, bufv_ref, bufi_ref
    n_new = kn_ref.shape[0]
    live = past + n_new
    for h in range(A_HEADS):
        sl = slice(h * A_HEAD_DIM, (h + 1) * A_HEAD_DIM)
        kb_scr[0:past, sl] = _head_rows(kc_ref, h, past, A_HEADS)[...].astype(BF16)
        vrow_scr[0:past, sl] = _head_rows(vc_ref, h, past, A_HEADS)[...]
        _head_rows(ck_ref, h, n_new, A_HEADS)[...] = kn_ref[:, sl]
        _head_rows(cv_ref, h, n_new, A_HEADS)[...] = vn_ref[:, sl]
    ci_ref[...] = mkn_ref[:, 0:IDX_DIM]
    kb_scr[past:live, :] = kn_ref[...].astype(BF16)
    kib_scr[0:past, :] = kic_ref[...].astype(BF16)
    kib_scr[past:live, :] = mkn_ref[:, 0:IDX_DIM].astype(BF16)
    vrow_scr[past:live, :] = vn_ref[...]
    if s_keys > live:
        kb_scr[live:, :] = jnp.zeros((s_keys - live, A_WIDTH), BF16)
        kib_scr[live:, :] = jnp.zeros((s_keys - live, IDX_DIM), BF16)
        vrow_scr[live:, :] = jnp.zeros((s_keys - live, A_WIDTH), F32)
    for i in range(s_keys // kb):
        vt_scr[:, i * kb:(i + 1) * kb] = vrow_scr[i * kb:(i + 1) * kb, :].T.astype(BF16)

    limit = jnp.full((1, qs), live, I32)
    _dsa_select_attend(q_ref, qi_ref, mq_ref, o_ref, kb_scr, vt_scr, kib_scr, w_scr, score_scr,
                       key_scr, mask_scr, n_keys=s_keys, qs=qs, topk=topk, limit=limit)


def _dsa_scratch(s_keys, qs):
    return [
        pltpu.VMEM((s_keys, A_WIDTH), BF16),
        pltpu.VMEM((A_WIDTH, s_keys), BF16),
        pltpu.VMEM((s_keys, IDX_DIM), BF16),
        pltpu.VMEM((LANES, qs), F32),
        pltpu.VMEM((s_keys, qs), F32),
        pltpu.VMEM((s_keys, qs), I32),
        pltpu.VMEM((s_keys, qs), F32),
    ]


def _dsa_prompt(proj, out_k, out_v, out_ki, layer, *, nb, tq, qs, kb):
    nsteps = tq // qs
    qcol = lambda w, off: pl.BlockSpec((qs, w), lambda b, j: (b * nsteps + j, off // w))
    kcol = lambda w, off: pl.BlockSpec((tq, w), lambda b, j: (b, off // w))
    cache = lambda rows, w: pl.BlockSpec((rows, w), lambda b, j: (layer * nb + b, 0))
    kern = functools.partial(_dsa_prompt_kernel, s_keys=tq, qs=qs, kb=kb, topk=min(DSA_TOPK, tq // 4))
    return pl.pallas_call(
        kern,
        grid=(nb, nsteps),
        in_specs=[qcol(A_WIDTH, COL_AQ), qcol(IQ_WIDTH, COL_IQ), qcol(LANES, COL_MISC),
                  kcol(A_WIDTH, COL_AK), kcol(A_WIDTH, COL_AV), kcol(LANES, COL_MISC), ANY, ANY, ANY],
        out_specs=[pl.BlockSpec((qs, A_WIDTH), lambda b, j: (b * nsteps + j, 0)),
                   cache(tq * A_HEADS, A_HEAD_DIM), cache(tq * A_HEADS, A_HEAD_DIM), cache(tq, IDX_DIM)],
        out_shape=[jax.ShapeDtypeStruct((proj.shape[0], A_WIDTH), F32),
                   jax.ShapeDtypeStruct(out_k.shape, F32), jax.ShapeDtypeStruct(out_v.shape, F32),
                   jax.ShapeDtypeStruct(out_ki.shape, F32)],
        input_output_aliases={6: 1, 7: 2, 8: 3},
        scratch_shapes=_dsa_scratch(tq, qs),
        compiler_params=_cparams("parallel", "arbitrary"),
        name="dsa_prompt",
    )(proj, proj, proj, proj, proj, proj, out_k, out_v, out_ki)


def _dsa_sample(proj, cache_k, cache_v, cache_ki, buf, out_k, out_v, out_ki, layer,
                *, nb, ts, past, row0, kb):
    rb0 = row0 // ts
    live = past + ts
    s_keys = -(-live // kb) * kb
    qcol = lambda w, off: pl.BlockSpec((ts, w), lambda b, j: (rb0 + b, off // w))
    per_seq = lambda rows, w: pl.BlockSpec((rows, w), lambda b, j: (layer * nb + b, 0))
    kern = functools.partial(_dsa_sample_kernel, s_keys=s_keys, qs=ts, kb=kb,
                             topk=min(DSA_TOPK, live // 4), past=past)
    return pl.pallas_call(
        kern,
        grid=(nb, 1),
        in_specs=[qcol(A_WIDTH, COL_AQ), qcol(IQ_WIDTH, COL_IQ), qcol(LANES, COL_MISC),
                  per_seq(past * A_HEADS, A_HEAD_DIM), per_seq(past * A_HEADS, A_HEAD_DIM),
                  per_seq(past, IDX_DIM),
                  qcol(A_WIDTH, COL_AK), qcol(A_WIDTH, COL_AV), qcol(LANES, COL_MISC),
                  ANY, ANY, ANY, ANY],
        out_specs=[pl.BlockSpec((ts, A_WIDTH), lambda b, j: (rb0 + b, 0)),
                   per_seq(ts * A_HEADS, A_HEAD_DIM), per_seq(ts * A_HEADS, A_HEAD_DIM),
                   per_seq(ts, IDX_DIM)],
        out_shape=[jax.ShapeDtypeStruct(buf.shape, F32), jax.ShapeDtypeStruct(out_k.shape, F32),
                   jax.ShapeDtypeStruct(out_v.shape, F32), jax.ShapeDtypeStruct(out_ki.shape, F32)],
        input_output_aliases={9: 0, 10: 1, 11: 2, 12: 3},
        scratch_shapes=_dsa_scratch(s_keys, ts) + [pltpu.VMEM((s_keys, A_WIDTH), F32)],
        compiler_params=_cparams("parallel", "arbitrary"),
        name="dsa_sample",
    )(proj, proj, proj, cache_k, cache_v, cache_ki, proj, proj, proj, buf, out_k, out_v, out_ki)


BAND_QS_PROMPT = 256


def _band_bias_kernel(tab_ref, o_ref, *, qs, w):
    layer = pl.program_id(0)
    wp = -(-(w + qs) // LANES) * LANES
    jj = lax.broadcasted_iota(I32, (8, wp), 1)
    d = jnp.where(jj < w, jj, jj - wp)
    idx = jnp.clip(B_WIN - d, -REL_CLIP, REL_CLIP) + REL_CLIP
    q = lax.broadcasted_iota(I32, (qs, w), 0)
    jx = lax.broadcasted_iota(I32, (qs, w), 1)
    back = (q >> 6) + B_PREV_CHUNKS - (jx >> 6)
    allowed = (back >= 0) & (back <= B_PREV_CHUNKS)
    for h in range(B_HEADS):
        row = layer * B_HEADS + h

        def body(v, acc):
            return jnp.where(idx == v, tab_ref[row, v], acc)

        base = lax.fori_loop(0, 2 * REL_CLIP + 1, body, jnp.zeros((8, wp), F32))
        rows = jnp.broadcast_to(base[0:1, :], (qs, wp))
        toeplitz = pltpu.roll(rows, 0, 1, stride=1, stride_axis=0)
        o_ref[0, h] = jnp.where(allowed, toeplitz[:, 0:w], NEG_INF)


def _band_bias(b_rel, qs):
    depth = b_rel.shape[0]
    w = qs + B_WIN
    return pl.pallas_call(
        functools.partial(_band_bias_kernel, qs=qs, w=w),
        grid=(depth,),
        in_specs=[pl.BlockSpec(memory_space=pltpu.SMEM)],
        out_specs=pl.BlockSpec((1, B_HEADS, qs, w), lambda l: (l, 0, 0, 0)),
        out_shape=jax.ShapeDtypeStruct((depth, B_HEADS, qs, w), F32),
        compiler_params=_cparams("arbitrary"),
        name=f"band_bias_{qs}",
    )(b_rel.reshape(depth * B_HEADS, 2 * REL_CLIP + 1))


def _band_attend(q_ref, bias_ref, o_ref, kp_scr, vp_scr, start, valid, qs):
    w = qs + B_WIN
    for h in range(B_HEADS):
        sl = slice(h * B_HEAD_DIM, (h + 1) * B_HEAD_DIM)
        qh = q_ref[:, sl].astype(BF16)
        lg = _nt_dot(qh, kp_scr[pl.ds(start, w), sl]) * (B_HEAD_DIM ** -0.5) + bias_ref[0, h]
        if valid is not None:
            lg = jnp.where(valid, lg, NEG_INF)
        m = jnp.max(lg, axis=-1, keepdims=True)
        e = jnp.exp(lg - m)
        den = jnp.sum(e, axis=-1, keepdims=True)
        o = jnp.dot(e.astype(BF16), vp_scr[pl.ds(start, w), sl], preferred_element_type=F32)
        o_ref[:, sl] = o / den


def _band_prompt_kernel(q_ref, k_ref, v_ref, bias_ref, bufk_ref, bufv_ref, o_ref, ck_ref, cv_ref,
                        kp_scr, vp_scr, *, s_keys, qs, nkeep):
    del bufk_ref, bufv_ref
    j = pl.program_id(1)

    @pl.when(j == 0)
    def _():
        kp_scr[0:B_WIN, :] = jnp.zeros((B_WIN, B_WIDTH), BF16)
        vp_scr[0:B_WIN, :] = jnp.zeros((B_WIN, B_WIDTH), BF16)
        kp_scr[B_WIN:B_WIN + s_keys, :] = k_ref[...].astype(BF16)
        vp_scr[B_WIN:B_WIN + s_keys, :] = v_ref[...].astype(BF16)
        for h in range(B_HEADS):
            sl = slice(h * B_HEAD_DIM, (h + 1) * B_HEAD_DIM)
            _head_rows(ck_ref, h, nkeep, B_HEADS)[...] = k_ref[s_keys - nkeep:, sl]
            _head_rows(cv_ref, h, nkeep, B_HEADS)[...] = v_ref[s_keys - nkeep:, sl]

    start = pl.multiple_of(j * qs, qs)
    valid = lax.broadcasted_iota(I32, (qs, qs + B_WIN), 1) >= B_WIN - j * qs
    _band_attend(q_ref, bias_ref, o_ref, kp_scr, vp_scr, start, valid, qs)


def _band_sample_kernel(q_ref, kc_ref, vc_ref, kn_ref, vn_ref, bias_ref, bufo_ref, bufk_ref, bufv_ref,
                        o_ref, ck_ref, cv_ref, kp_scr, vp_scr, *, qs):
    del bufo_ref, bufk_ref, bufv_ref
    for h in range(B_HEADS):
        sl = slice(h * B_HEAD_DIM, (h + 1) * B_HEAD_DIM)
        kp_scr[0:B_WIN, sl] = _head_rows(kc_ref, h, B_WIN, B_HEADS)[...].astype(BF16)
        vp_scr[0:B_WIN, sl] = _head_rows(vc_ref, h, B_WIN, B_HEADS)[...].astype(BF16)
        _head_rows(ck_ref, h, qs, B_HEADS)[...] = kn_ref[:, sl]
        _head_rows(cv_ref, h, qs, B_HEADS)[...] = vn_ref[:, sl]
    kp_scr[B_WIN:B_WIN + qs, :] = kn_ref[...].astype(BF16)
    vp_scr[B_WIN:B_WIN + qs, :] = vn_ref[...].astype(BF16)
    _band_attend(q_ref, bias_ref, o_ref, kp_scr, vp_scr, 0, None, qs)


def _band_prompt(proj, bias, out_k, out_v, layer, *, nb, tq, qs):
    nsteps = tq // qs
    w = qs + B_WIN
    nkeep = min(B_WIN, tq)
    kcol = lambda off: pl.BlockSpec((tq, B_WIDTH), lambda b, j: (b, off // B_WIDTH))
    cache = pl.BlockSpec((nkeep * B_HEADS, B_HEAD_DIM), lambda b, j: (layer * nb + b, 0))
    return pl.pallas_call(
        functools.partial(_band_prompt_kernel, s_keys=tq, qs=qs, nkeep=nkeep),
        grid=(nb, nsteps),
        in_specs=[
            pl.BlockSpec((qs, B_WIDTH), lambda b, j: (b * nsteps + j, COL_BQ // B_WIDTH)),
            kcol(COL_BK), kcol(COL_BV),
            pl.BlockSpec((1, B_HEADS, qs, w), lambda b, j: (layer, 0, 0, 0)),
            ANY, ANY,
        ],
        out_specs=[pl.BlockSpec((qs, B_WIDTH), lambda b, j: (b * nsteps + j, 0)), cache, cache],
        out_shape=[jax.ShapeDtypeStruct((proj.shape[0], B_WIDTH), F32),
                   jax.ShapeDtypeStruct(out_k.shape, F32), jax.ShapeDtypeStruct(out_v.shape, F32)],
        input_output_aliases={4: 1, 5: 2},
        scratch_shapes=[pltpu.VMEM((B_WIN + tq, B_WIDTH), BF16),
                        pltpu.VMEM((B_WIN + tq, B_WIDTH), BF16)],
        compiler_params=_cparams("parallel", "arbitrary"),
        name="band_prompt",
    )(proj, proj, proj, bias, out_k, out_v)


def _band_sample(proj, cache_k, cache_v, bias, buf, out_k, out_v, layer, *, nb, ts, row0):
    rb0 = row0 // ts
    w = ts + B_WIN
    qcol = lambda off: pl.BlockSpec((ts, B_WIDTH), lambda b, j: (rb0 + b, off // B_WIDTH))
    per_seq = lambda rows: pl.BlockSpec((rows * B_HEADS, B_HEAD_DIM), lambda b, j: (layer * nb + b, 0))
    return pl.pallas_call(
        functools.partial(_band_sample_kernel, qs=ts),
        grid=(nb, 1),
        in_specs=[qcol(COL_BQ), per_seq(B_WIN), per_seq(B_WIN), qcol(COL_BK), qcol(COL_BV),
                  pl.BlockSpec((1, B_HEADS, ts, w), lambda b, j: (layer, 0, 0, 0)),
                  ANY, ANY, ANY],
        out_specs=[pl.BlockSpec((ts, B_WIDTH), lambda b, j: (rb0 + b, 0)), per_seq(ts), per_seq(ts)],
        out_shape=[jax.ShapeDtypeStruct(buf.shape, F32), jax.ShapeDtypeStruct(out_k.shape, F32),
                   jax.ShapeDtypeStruct(out_v.shape, F32)],
        input_output_aliases={6: 0, 7: 1, 8: 2},
        scratch_shapes=[pltpu.VMEM((w, B_WIDTH), BF16), pltpu.VMEM((w, B_WIDTH), BF16)],
        compiler_params=_cparams("parallel", "arbitrary"),
        name="band_sample",
    )(proj, cache_k, cache_v, proj, proj, bias, buf, out_k, out_v)


XP_OFF = 8
SSD_CHUNKS_PER_STEP = 4
_HEAD_SPREAD = np.zeros((LANES, C_WIDTH), np.float32)
for _h in range(C_HEADS):
    _HEAD_SPREAD[MISC_DT + _h, _h * C_HEAD_DIM:(_h + 1) * C_HEAD_DIM] = 1.0


def _ssd_kernel(*refs, zero_init, nsteps, cps):
    if zero_init:
        (z_ref, xbc_ref, misc_ref, cw_ref, cbias_ref, prm_ref, gn_ref, ex_ref, dskip_ref,
         bufh_ref, bufc_ref, o_ref, hout_ref, cnew_ref, st_scr, xp_scr, y_scr) = refs
    else:
        (z_ref, xbc_ref, misc_ref, h0_ref, cb0_ref, cw_ref, cbias_ref, prm_ref, gn_ref, ex_ref, dskip_ref,
         bufo_ref, bufh_ref, bufc_ref, o_ref, hout_ref, cnew_ref, st_scr, xp_scr, y_scr) = refs
        del bufo_ref
    del bufh_ref, bufc_ref
    c = pl.program_id(1)
    tail = CONV_W - 1
    rep = C_HEADS // C_GROUPS
    gw = C_WIDTH // C_GROUPS

    @pl.when(c == 0)
    def _():
        if zero_init:
            st_scr[...] = jnp.zeros(st_scr.shape, F32)
            xp_scr[XP_OFF - tail:XP_OFF, :] = jnp.zeros((tail, CONV_DIM), F32)
        else:
            for h in range(C_HEADS):
                st_scr[h // rep, :, (h % rep) * C_HEAD_DIM:(h % rep + 1) * C_HEAD_DIM] = h0_ref[0, h].T
            xp_scr[XP_OFF - tail:XP_OFF, :] = cb0_ref[0]

    def chunk(rows):
        xp_scr[XP_OFF:XP_OFF + CHUNK, :] = xbc_ref[rows, :]
        y = cbias_ref[...] + xp_scr[XP_OFF - tail:XP_OFF - tail + CHUNK, :] * cw_ref[0:1, :]
        for t in range(1, CONV_W):
            y = y + xp_scr[XP_OFF - tail + t:XP_OFF - tail + t + CHUNK, :] * cw_ref[t:t + 1, :]
        new_tail = xp_scr[XP_OFF + CHUNK - tail:XP_OFF + CHUNK, :]
        xp_scr[XP_OFF - tail:XP_OFF, :] = new_tail
        cnew_ref[0] = new_tail
        xc = y * (1.0 / (1.0 + jnp.exp(-y)))
        xs = xc[:, 0:C_WIDTH]

        pre = misc_ref[rows, :] + prm_ref[0:1, :]
        dt = jnp.maximum(pre, 0.0) + jnp.log(1.0 + jnp.exp(-jnp.abs(pre)))
        ad = dt * (-jnp.exp(prm_ref[1:2, :]))
        li = lax.broadcasted_iota(I32, (CHUNK, CHUNK), 0)
        si = lax.broadcasted_iota(I32, (CHUNK, CHUNK), 1)
        causal = li >= si
        tri = causal.astype(BF16)
        acs = sum(jnp.dot(tri, part, preferred_element_type=F32) for part in _split3(ad))
        acs_t = acs.T
        ex = ex_ref[...]
        spread = lambda v: sum(jnp.dot(part, ex, preferred_element_type=F32) for part in _split3(v))
        dt_x = spread(dt)
        acs_x = spread(acs)
        acs_last_x = acs_x[CHUNK - 1:CHUNK, :]
        xd_all = xs * dt_x
        xd_b = xd_all.astype(BF16)
        xde_b = (xd_all * jnp.exp(acs_last_x - acs_x)).astype(BF16)
        grow_x = jnp.exp(acs_x)
        chunk_decay_x = jnp.exp(acs_last_x)
        head_of_lane = lax.broadcasted_iota(I32, (CHUNK, gw), 1) >> 6

        for g in range(C_GROUPS):
            gl = slice(g * gw, (g + 1) * gw)
            b_g = xc[:, C_WIDTH + g * C_STATE:C_WIDTH + (g + 1) * C_STATE]
            c_g = xc[:, C_WIDTH + (C_GROUPS + g) * C_STATE:C_WIDTH + (C_GROUPS + g + 1) * C_STATE]
            b_gt = b_g.T.astype(BF16)
            c_gb = c_g.astype(BF16)
            cb = jnp.dot(c_gb, b_gt, preferred_element_type=F32)
            st = st_scr[g]
            y_g = jnp.dot(c_gb, st.astype(BF16), preferred_element_type=F32) * grow_x[:, gl]
            xd_g = xd_b[:, gl]
            for r in range(rep):
                col = MISC_DT + g * rep + r
                seg = acs[:, col:col + 1] - acs_t[col:col + 1, :]
                decay_in = jnp.where(causal, jnp.exp(jnp.where(causal, seg, 0.0)), 0.0)
                t_r = jnp.dot((cb * decay_in).astype(BF16), xd_g, preferred_element_type=F32)
                y_g = y_g + jnp.where(head_of_lane == r, t_r, 0.0)
            y_scr[:, gl] = y_g + dskip_ref[:, gl] * xs[:, gl]
            st_scr[g] = st * chunk_decay_x[:, gl] + jnp.dot(b_gt, xde_b[:, gl], preferred_element_type=F32)

        z = z_ref[rows, :]
        gate = y_scr[...] * (z * (1.0 / (1.0 + jnp.exp(-z))))
        for g in range(C_GROUPS):
            gg = gate[:, g * gw:(g + 1) * gw]
            ms = jnp.mean(gg * gg, axis=-1, keepdims=True)
            o_ref[rows, g * gw:(g + 1) * gw] = gg * lax.rsqrt(ms + EPS) * gn_ref[:, g * gw:(g + 1) * gw]

    for i in range(cps):
        chunk(slice(i * CHUNK, (i + 1) * CHUNK))

    @pl.when(c == nsteps - 1)
    def _():
        for h in range(C_HEADS):
            hout_ref[0, h] = st_scr[h // rep, :, (h % rep) * C_HEAD_DIM:(h % rep + 1) * C_HEAD_DIM].T


def _ssd(proj, conv_w, conv_b, prm, gate_norm, out_h, out_c, layer, *, nb, nchunks, row0, state=None):
    cps = SSD_CHUNKS_PER_STEP if nchunks % SSD_CHUNKS_PER_STEP == 0 else 1
    nsteps = nchunks // cps
    rt = cps * CHUNK
    rb0 = row0 // rt
    row = lambda b, c: rb0 + b * nsteps + c
    const = lambda shape: pl.BlockSpec(shape, lambda b, c: tuple(0 for _ in shape))
    per_seq_h = pl.BlockSpec((1, C_HEADS, C_HEAD_DIM, C_STATE), lambda b, c: (layer * nb + b, 0, 0, 0))
    per_seq_c = pl.BlockSpec((1, CONV_W - 1, CONV_DIM), lambda b, c: (layer * nb + b, 0, 0))
    in_specs = [
        pl.BlockSpec((rt, C_WIDTH), lambda b, c: (row(b, c), COL_CZ // C_WIDTH)),
        pl.BlockSpec((rt, CONV_DIM), lambda b, c: (row(b, c), COL_CXBC // CONV_DIM)),
        pl.BlockSpec((rt, LANES), lambda b, c: (row(b, c), COL_MISC // LANES)),
    ]
    args = [proj, proj, proj]
    if state is not None:
        h0, cbuf0, buf = state
        in_specs += [per_seq_h, per_seq_c]
        args += [h0, cbuf0]
    in_specs += [const((CONV_W, CONV_DIM)), const((1, CONV_DIM)), const((8, LANES)), const((1, C_WIDTH)),
                 const((LANES, C_WIDTH)), const((1, C_WIDTH))]
    args += [conv_w, conv_b.reshape(1, CONV_DIM), prm[0], gate_norm.reshape(1, C_WIDTH),
             jnp.asarray(_HEAD_SPREAD, BF16), prm[1]]
    aliases = {}
    if state is not None:
        in_specs.append(ANY)
        args.append(buf)
        aliases[len(args) - 1] = 0
    in_specs += [ANY, ANY]
    args += [out_h, out_c]
    aliases[len(args) - 2] = 1
    aliases[len(args) - 1] = 2
    return pl.pallas_call(
        functools.partial(_ssd_kernel, zero_init=state is None, nsteps=nsteps, cps=cps),
        grid=(nb, nsteps),
        in_specs=in_specs,
        out_specs=[pl.BlockSpec((rt, C_WIDTH), lambda b, c: (row(b, c), 0)), per_seq_h, per_seq_c],
        out_shape=[
            jax.ShapeDtypeStruct((proj.shape[0], C_WIDTH), F32),
            jax.ShapeDtypeStruct(out_h.shape, F32),
            jax.ShapeDtypeStruct(out_c.shape, F32),
        ],
        input_output_aliases=aliases,
        scratch_shapes=[pltpu.VMEM((C_GROUPS, C_STATE, C_WIDTH // C_GROUPS), F32),
                        pltpu.VMEM((XP_OFF + CHUNK, CONV_DIM), F32),
                        pltpu.VMEM((CHUNK, C_WIDTH), F32)],
        compiler_params=_cparams("parallel", "arbitrary"),
        name="ssd_prompt" if state is None else "ssd_sample",
    )(*args)


def _outproj_kernel(x_ref, oa_ref, ob_ref, oc_ref, w_ref, o_ref):
    acc = x_ref[...]
    acc = acc + jnp.dot(oa_ref[...].astype(BF16), w_ref[0:A_WIDTH, :], preferred_element_type=F32)
    acc = acc + jnp.dot(ob_ref[...].astype(BF16), w_ref[A_WIDTH:A_WIDTH + B_WIDTH, :],
                        preferred_element_type=F32)
    acc = acc + jnp.dot(oc_ref[...].astype(BF16), w_ref[A_WIDTH + B_WIDTH:, :],
                        preferred_element_type=F32)
    o_ref[...] = acc


def _out_proj(x, oa, ob, oc, w_bf16):
    t, d = x.shape
    tm = _row_tile(t, 512)
    rows = lambda w: pl.BlockSpec((tm, w), lambda i: (i, 0))
    return pl.pallas_call(
        _outproj_kernel,
        grid=(t // tm,),
        in_specs=[rows(d), rows(A_WIDTH), rows(B_WIDTH), rows(C_WIDTH),
                  pl.BlockSpec((d, d), lambda i: (0, 0))],
        out_specs=rows(d),
        out_shape=jax.ShapeDtypeStruct((t, d), F32),
        compiler_params=_cparams("parallel"),
        name="out_proj",
    )(x, oa, ob, oc, w_bf16)


def _ffn_kernel(x_ref, nw_ref, wu_ref, wd_ref, o_ref, h_scr):
    def step(first):
        if first:
            x = x_ref[...]
            ms = jnp.mean(x * x, axis=-1, keepdims=True)
            h_scr[...] = (x * lax.rsqrt(ms + EPS) * nw_ref[...]).astype(BF16)
        u = jnp.maximum(jnp.dot(h_scr[...], wu_ref[...], preferred_element_type=F32), 0.0)
        down = jnp.dot((u * u).astype(BF16), wd_ref[...], preferred_element_type=F32)
        if first:
            o_ref[...] = x_ref[...] + down
        else:
            o_ref[...] += down

    f = pl.program_id(1)
    pl.when(f == 0)(functools.partial(step, True))
    pl.when(f != 0)(functools.partial(step, False))


def _ffn(x, norm_w, wu_bf16, wd_bf16):
    t, d = x.shape
    ff = wu_bf16.shape[1]
    tm = _row_tile(t, 1024)
    tf = 512
    return pl.pallas_call(
        _ffn_kernel,
        grid=(t // tm, ff // tf),
        in_specs=[
            pl.BlockSpec((tm, d), lambda i, f: (i, 0)),
            pl.BlockSpec((1, d), lambda i, f: (0, 0)),
            pl.BlockSpec((d, tf), lambda i, f: (0, f)),
            pl.BlockSpec((tf, d), lambda i, f: (f, 0)),
        ],
        out_specs=pl.BlockSpec((tm, d), lambda i, f: (i, 0)),
        out_shape=jax.ShapeDtypeStruct((t, d), F32),
        scratch_shapes=[pltpu.VMEM((tm, d), BF16)],
        compiler_params=_cparams("parallel", "arbitrary"),
        name="ffn",
    )(x, norm_w.reshape(1, d), wu_bf16, wd_bf16)


def _norm_kernel(x_ref, nw_ref, o_ref):
    x = x_ref[...]
    ms = jnp.mean(x * x, axis=-1, keepdims=True)
    o_ref[...] = x * lax.rsqrt(ms + EPS) * nw_ref[...]


def _final_norm(x, norm_w, row0, nrows):
    d = x.shape[1]
    tm = _row_tile(nrows, 512)
    assert row0 % tm == 0
    return pl.pallas_call(
        _norm_kernel,
        grid=(nrows // tm,),
        in_specs=[pl.BlockSpec((tm, d), lambda i: (row0 // tm + i, 0)),
                  pl.BlockSpec((1, d), lambda i: (0, 0))],
        out_specs=pl.BlockSpec((tm, d), lambda i: (i, 0)),
        out_shape=jax.ShapeDtypeStruct((nrows, d), F32),
        compiler_params=_cparams("parallel"),
        name="final_norm",
    )(x, norm_w.reshape(1, d))


def _lane_row(v, off):
    return jnp.zeros((LANES,), F32).at[off:off + v.shape[0]].set(v)


def _trunk(x_prompt, x_sample, cache_a_k, cache_a_v, cache_a_kidx, cache_b_k, cache_b_v,
           state_ssm, state_conv, norm1, w_in, w_out, b_rel, conv_w, conv_b, dt_bias,
           a_log, d_skip, gate_norm, norm2, w_up, w_down, final_norm):
    bp, tp, d = x_prompt.shape
    bs, ts, _ = x_sample.shape
    depth = w_in.shape[0]
    past = cache_a_k.shape[2]
    nbuf = cache_b_k.shape[2]
    n_p = bp * tp
    n_s = bs * ts
    assert ts == CHUNK and tp % DSA_KB_PROMPT == 0 and tp % BAND_QS_PROMPT == 0 and nbuf == B_WIN
    qs_p = LANES
    nkeep = min(B_WIN, tp)

    x = jnp.concatenate([x_prompt.reshape(n_p, d), x_sample.reshape(n_s, d)], axis=0)

    pos = jnp.concatenate([jnp.tile(jnp.arange(tp), bp), jnp.tile(past + jnp.arange(ts), bs)])
    tab_a = _rope_tables(pos, A_HEAD_DIM)
    tab_i = _rope_tables(pos, IDX_DIM)
    bias_p = _band_bias(b_rel, BAND_QS_PROMPT)
    bias_s = _band_bias(b_rel, ts)
    w_in_r = _regroup_w_in(w_in)
    cak = cache_a_k.reshape(depth * bs * past * A_HEADS, A_HEAD_DIM)
    cav = cache_a_v.reshape(depth * bs * past * A_HEADS, A_HEAD_DIM)
    caki = cache_a_kidx.reshape(depth * bs * past, IDX_DIM)
    cbk = cache_b_k.reshape(depth * bs * nbuf * B_HEADS, B_HEAD_DIM)
    cbv = cache_b_v.reshape(depth * bs * nbuf * B_HEADS, B_HEAD_DIM)
    h0 = state_ssm.reshape(depth * bs, C_HEADS, C_HEAD_DIM, C_STATE)
    cbuf0 = state_conv.reshape(depth * bs, CONV_W - 1, CONV_DIM)

    p_ak = _uninit((depth * bp * tp * A_HEADS, A_HEAD_DIM))
    p_av = _uninit((depth * bp * tp * A_HEADS, A_HEAD_DIM))
    p_aki = _uninit((depth * bp * tp, IDX_DIM))
    p_bk = _uninit((depth * bp * nkeep * B_HEADS, B_HEAD_DIM))
    p_bv = _uninit((depth * bp * nkeep * B_HEADS, B_HEAD_DIM))
    p_h = _uninit((depth * bp, C_HEADS, C_HEAD_DIM, C_STATE))
    p_c = _uninit((depth * bp, CONV_W - 1, CONV_DIM))
    s_ak = _uninit((depth * bs * ts * A_HEADS, A_HEAD_DIM))
    s_av = _uninit((depth * bs * ts * A_HEADS, A_HEAD_DIM))
    s_aki = _uninit((depth * bs * ts, IDX_DIM))
    s_bk = _uninit((depth * bs * ts * B_HEADS, B_HEAD_DIM))
    s_bv = _uninit((depth * bs * ts * B_HEADS, B_HEAD_DIM))
    s_h = _uninit((depth * bs, C_HEADS, C_HEAD_DIM, C_STATE))
    s_c = _uninit((depth * bs, CONV_W - 1, CONV_DIM))

    for l in range(depth):
        proj = _in_proj(x, norm1[l], w_in_r, l, tab_a, tab_i)

        oa, p_ak, p_av, p_aki = _dsa_prompt(proj, p_ak, p_av, p_aki, l, nb=bp, tq=tp, qs=qs_p,
                                            kb=DSA_KB_PROMPT)
        oa, s_ak, s_av, s_aki = _dsa_sample(proj, cak, cav, caki, oa, s_ak, s_av, s_aki, l,
                                            nb=bs, ts=ts, past=past, row0=n_p, kb=DSA_KB_SAMPLE)
        ob, p_bk, p_bv = _band_prompt(proj, bias_p, p_bk, p_bv, l, nb=bp, tq=tp, qs=BAND_QS_PROMPT)
        ob, s_bk, s_bv = _band_sample(proj, cbk, cbv, bias_s, ob, s_bk, s_bv, l, nb=bs, ts=ts, row0=n_p)
        rows = jnp.zeros((8, LANES), F32)
        rows = rows.at[0].set(_lane_row(dt_bias[l], MISC_DT)).at[1].set(_lane_row(a_log[l], MISC_DT))
        prm = (rows, jnp.repeat(d_skip[l], C_HEAD_DIM).reshape(1, C_WIDTH))
        oc, p_h, p_c = _ssd(proj, conv_w[l], conv_b[l], prm, gate_norm[l], p_h, p_c, l,
                            nb=bp, nchunks=tp // CHUNK, row0=0)
        oc, s_h, s_c = _ssd(proj, conv_w[l], conv_b[l], prm, gate_norm[l], s_h, s_c, l,
                            nb=bs, nchunks=1, row0=n_p, state=(h0, cbuf0, oc))

        x = _out_proj(x, oa, ob, oc, w_out[l].astype(BF16))
        x = _ffn(x, norm2[l], w_up[l].astype(BF16), w_down[l].astype(BF16))

    y_prompt = _final_norm(x, final_norm, 0, n_p).reshape(bp, tp, d)
    y_sample = _final_norm(x, final_norm, n_p, n_s).reshape(bs, ts, d)
    return (y_prompt, y_sample,
            p_ak.reshape(depth, bp, tp, A_HEADS, A_HEAD_DIM),
            p_av.reshape(depth, bp, tp, A_HEADS, A_HEAD_DIM),
            p_aki.reshape(depth, bp, tp, IDX_DIM),
            p_bk.reshape(depth, bp, nkeep, B_HEADS, B_HEAD_DIM),
            p_bv.reshape(depth, bp, nkeep, B_HEADS, B_HEAD_DIM),
            p_h.reshape(depth, bp, C_HEADS, C_HEAD_DIM, C_STATE),
            p_c.reshape(depth, bp, CONV_W - 1, CONV_DIM),
            s_ak.reshape(depth, bs, ts, A_HEADS, A_HEAD_DIM),
            s_av.reshape(depth, bs, ts, A_HEADS, A_HEAD_DIM),
            s_aki.reshape(depth, bs, ts, IDX_DIM),
            s_bk.reshape(depth, bs, ts, B_HEADS, B_HEAD_DIM),
            s_bv.reshape(depth, bs, ts, B_HEADS, B_HEAD_DIM),
            s_h.reshape(depth, bs, C_HEADS, C_HEAD_DIM, C_STATE),
            s_c.reshape(depth, bs, CONV_W - 1, CONV_DIM))


def kernel(x_prompt, x_sample, cache_a_k, cache_a_v, cache_a_kidx, cache_b_k, cache_b_v, state_ssm, state_conv, norm1, w_in, w_out, b_rel, conv_w, conv_b, dt_bias, a_log, d_skip, gate_norm, norm2, w_up, w_down, final_norm):
    return _trunk(x_prompt, x_sample, cache_a_k, cache_a_v, cache_a_kidx, cache_b_k, cache_b_v,
                  state_ssm, state_conv, norm1, w_in, w_out, b_rel, conv_w, conv_b, dt_bias,
                  a_log, d_skip, gate_norm, norm2, w_up, w_down, final_norm)
```

```python
import functools
import math

import jax
import jax.numpy as jnp
import numpy as np
from jax import lax
from jax.experimental import pallas as pl
from jax.experimental.pallas import tpu as pltpu

F32 = jnp.float32
BF16 = jnp.bfloat16
I32 = jnp.int32

D_MODEL = 2048
CHUNK = 64
A_HEADS = 4
A_HEAD_DIM = 128
A_WIDTH = A_HEADS * A_HEAD_DIM
IDX_HEADS = 16
IDX_DIM = 64
IQ_WIDTH = IDX_HEADS * IDX_DIM
DSA_TOPK = 256
B_HEADS = 4
B_HEAD_DIM = 128
B_WIDTH = B_HEADS * B_HEAD_DIM
B_PREV_CHUNKS = 8
B_WIN = B_PREV_CHUNKS * CHUNK
REL_CLIP = 128
C_WIDTH = 1024
C_HEAD_DIM = 64
C_HEADS = C_WIDTH // C_HEAD_DIM
C_GROUPS = 4
C_STATE = 128
CONV_W = 4
CONV_DIM = C_WIDTH + 2 * C_GROUPS * C_STATE
D_FF = 4 * D_MODEL
ROPE_THETA = 500000.0
EPS = 1e-5

LANES = 128
MXU_N = 256
INT_MIN = -(2 ** 31)
NEG_INF = float("-inf")

COL_AQ = 0
COL_AK = 512
COL_IQ = 1024
COL_AV = 2048
COL_BQ = 2560
COL_BK = 3072
COL_BV = 3584
COL_CXBC = 4096
COL_CZ = 6144
COL_MISC = 7168
MISC_IW = 64
MISC_DT = 80
IN_COLS_PAD = 7680
IN_TN = 1280
SRC_AQ, SRC_AV, SRC_IQ, SRC_IK, SRC_BQ, SRC_CZ, SRC_CXBC, SRC_DT, SRC_END = (
    0, 1024, 1536, 2560, 2640, 4176, 5200, 7248, 7264)

VMEM_LIMIT = 56 * 1024 * 1024
ANY = pl.BlockSpec(memory_space=pl.ANY)


def _cparams(*sem):
    return pltpu.CompilerParams(dimension_semantics=sem, vmem_limit_bytes=VMEM_LIMIT)


def _row_tile(t, cap):
    tm = cap
    while t % tm:
        tm //= 2
    return tm


def _nt_dot(a, b):
    return lax.dot_general(a, b, (((1,), (1,)), ((), ())), preferred_element_type=F32)


def _split3(x):
    hi = x.astype(BF16)
    r1 = x - hi.astype(F32)
    mid = r1.astype(BF16)
    lo = (r1 - mid.astype(F32)).astype(BF16)
    return hi, mid, lo


def _head_rows(ref, h, n, heads):
    return ref.at[pl.ds(h, n, stride=heads), :]


def _uninit_kernel(o_ref):
    del o_ref


def _uninit(shape, dtype=F32):
    return pl.pallas_call(_uninit_kernel, out_specs=ANY, out_shape=jax.ShapeDtypeStruct(shape, dtype),
                          name="alloc")()


def _regroup_kernel(w_ref, o_ref):
    def put(dst, src, width):
        o_ref[0, :, dst:dst + width] = w_ref[0, :, src:src + width].astype(BF16)

    rows = w_ref.shape[1]
    put(COL_AQ, SRC_AQ, 2 * A_WIDTH)
    put(COL_IQ, SRC_IQ, IQ_WIDTH)
    put(COL_AV, SRC_AV, A_WIDTH)
    put(COL_BQ, SRC_BQ, 3 * B_WIDTH)
    put(COL_CXBC, SRC_CXBC, CONV_DIM)
    put(COL_CZ, SRC_CZ, C_WIDTH)
    o_ref[0, :, COL_MISC:] = jnp.zeros((rows, IN_COLS_PAD - COL_MISC), BF16)
    put(COL_MISC, SRC_IK, IDX_DIM + IDX_HEADS)
    put(COL_MISC + MISC_DT, SRC_DT, C_HEADS)


def _regroup_w_in(w_in):
    depth, d, n = w_in.shape
    assert n == SRC_END
    tk = 256
    return pl.pallas_call(
        _regroup_kernel,
        grid=(depth, d // tk),
        in_specs=[pl.BlockSpec((1, tk, n), lambda l, i: (l, i, 0))],
        out_specs=pl.BlockSpec((1, tk, IN_COLS_PAD), lambda l, i: (l, i, 0)),
        out_shape=jax.ShapeDtypeStruct((depth, d, IN_COLS_PAD), BF16),
        compiler_params=_cparams("parallel", "parallel"),
        name="regroup_w_in",
    )(w_in)


def _rope_block(x, c, sm, sp, half):
    return x * c + pltpu.roll(x, LANES - half, 1) * sm + pltpu.roll(x, half, 1) * sp


def _inproj_kernel(x_ref, nw_ref, w_ref, ta_ref, ti_ref, o_ref, h_scr):
    j = pl.program_id(1)

    def project(tile):
        if tile == 0:
            x = x_ref[...]
            ms = jnp.mean(x * x, axis=-1, keepdims=True)
            h_scr[...] = (x * lax.rsqrt(ms + EPS) * nw_ref[...]).astype(BF16)
        for pair in range(IN_TN // MXU_N):
            cols = slice(pair * MXU_N, (pair + 1) * MXU_N)
            both = jnp.dot(h_scr[...], w_ref[0, :, cols], preferred_element_type=F32)
            for half in range(MXU_N // LANES):
                blk = pair * (MXU_N // LANES) + half
                sl = slice(blk * LANES, (blk + 1) * LANES)
                acc = both[:, half * LANES:(half + 1) * LANES]
                col = -1 if tile is None else tile * IN_TN + blk * LANES
                if 0 <= col < COL_IQ:
                    acc = _rope_block(acc, ta_ref[0], ta_ref[1], ta_ref[2], A_HEAD_DIM // 8)
                elif COL_IQ <= col < COL_AV:
                    acc = _rope_block(acc, ti_ref[0], ti_ref[1], ti_ref[2], IDX_DIM // 8)
                elif col == COL_MISC:
                    lane = lax.broadcasted_iota(I32, acc.shape, 1)
                    rot = _rope_block(acc, ti_ref[0], ti_ref[1], ti_ref[2], IDX_DIM // 8)
                    acc = jnp.where(lane < IDX_DIM, rot, acc)
                o_ref[:, sl] = acc

    rope_tiles = [t for t in range(IN_COLS_PAD // IN_TN)
                  if t * IN_TN < COL_AV or t * IN_TN <= COL_MISC < (t + 1) * IN_TN]
    plain = j >= 0
    for tile in rope_tiles:
        pl.when(j == tile)(functools.partial(project, tile))
        plain = plain & (j != tile)
    pl.when(plain)(functools.partial(project, None))


def _in_proj(x, norm_w, w_all, layer, tab_a, tab_i, *, tm, n_prompt, t_prompt):
    t, d = x.shape
    n = w_all.shape[2]
    per_seq = t_prompt // tm
    tab = pl.BlockSpec((3, tm, LANES),
                       lambda i, j: (0, jnp.where(i < n_prompt // tm, i % per_seq, per_seq), 0))
    return pl.pallas_call(
        _inproj_kernel,
        grid=(t // tm, n // IN_TN),
        in_specs=[
            pl.BlockSpec((tm, d), lambda i, j: (i, 0)),
            pl.BlockSpec((1, d), lambda i, j: (0, 0)),
            pl.BlockSpec((1, d, IN_TN), lambda i, j: (layer, 0, j)),
            tab, tab,
        ],
        out_specs=pl.BlockSpec((tm, IN_TN), lambda i, j: (i, j)),
        out_shape=jax.ShapeDtypeStruct((t, n), F32),
        scratch_shapes=[pltpu.VMEM((tm, d), BF16)],
        compiler_params=_cparams("parallel", "arbitrary"),
        name="in_proj",
    )(x, norm_w.reshape(1, d), w_all, tab_a, tab_i)


def _rope_tables(pos, head_dim):
    rot = head_dim // 4
    half = rot // 2
    inv = ROPE_THETA ** (-jnp.arange(half, dtype=F32) * 2.0 / rot)
    ang = pos.astype(F32)[:, None] * inv[None, :]
    cos, sin = jnp.cos(ang), jnp.sin(ang)
    n = pos.shape[0]
    rest = head_dim - rot
    c = jnp.concatenate([cos, cos, jnp.ones((n, rest), F32)], axis=1)
    sm = jnp.concatenate([-sin, jnp.zeros((n, half + rest), F32)], axis=1)
    sp = jnp.concatenate([jnp.zeros((n, half), F32), sin, jnp.zeros((n, rest), F32)], axis=1)
    tab = jnp.stack([c, sm, sp], axis=0)
    return jnp.tile(tab, (1, 1, LANES // head_dim))


TIE_BLOCK = 64
SUB = 128
DSA_KB_PROMPT = 256
DSA_KB_SAMPLE = 384
ATT_BLOCK = 256


def _tree_sum(parts):
    while len(parts) > 1:
        parts = [parts[i] + parts[i + 1] for i in range(0, len(parts) - 1, 2)] + (
            parts[-1:] if len(parts) % 2 else [])
    return parts[0]


def _count_keys(key_scr, n_keys, qs, pred):
    blocks = []
    for u in range(n_keys // SUB):
        hit = jnp.where(pred(key_scr[u * SUB:(u + 1) * SUB, :]), 1, 0).astype(I32)
        blocks.append(_tree_sum([hit[i * 8:(i + 1) * 8, :] for i in range(SUB // 8)]))
    return jnp.sum(_tree_sum(blocks), axis=0, keepdims=True)


def _kth_largest(key_scr, n_keys, qs, k):
    count = functools.partial(_count_keys, key_scr, n_keys, qs)
    ans0 = jnp.where(count(lambda x: x >= 0) >= k, 0, INT_MIN).astype(I32)

    def bit_step(i, ans):
        cand = ans | (jnp.int32(1) << (30 - i))
        return jnp.where(count(lambda x: x >= cand) >= k, cand, ans)

    return lax.fori_loop(0, 31, bit_step, ans0)


def _dsa_select_attend(q_ref, qi_ref, mq_ref, o_ref, kb_scr, vt_scr, kib_scr, w_scr, score_scr,
                       key_scr, mask_scr, *, n_keys, qs, topk, limit):
    w_scr[...] = mq_ref[...].T * (IDX_HEADS ** -0.5 * IDX_DIM ** -0.5)

    kib = kib_scr[0:n_keys, :]
    for h in range(IDX_HEADS):
        qh = qi_ref[:, h * IDX_DIM:(h + 1) * IDX_DIM].astype(BF16)
        term = jnp.maximum(_nt_dot(kib, qh), 0.0) * w_scr[MISC_IW + h:MISC_IW + h + 1, :]
        if h == 0:
            score_scr[0:n_keys, :] = term
        else:
            score_scr[0:n_keys, :] += term
    sc = score_scr[0:n_keys, :]
    sc = jnp.where(sc == 0.0, 0.0, sc)
    bits = lax.bitcast_convert_type(sc, I32)
    key = bits ^ ((bits >> 31) & 0x7FFFFFFF)
    s_idx = lax.broadcasted_iota(I32, (n_keys, qs), 0)
    key_scr[0:n_keys, :] = jnp.where(s_idx < limit, key, INT_MIN)

    thr = _kth_largest(key_scr, n_keys, qs, topk)
    cnt_gt = _count_keys(key_scr, n_keys, qs, lambda x: x > thr)
    cnt_eq = _count_keys(key_scr, n_keys, qs, lambda x: x == thr)
    room = topk - cnt_gt
    live = thr > INT_MIN
    all_ties = (cnt_eq <= room) & live
    key = key_scr[0:n_keys, :]
    mask_scr[0:n_keys, :] = jnp.where((key > thr) | ((key == thr) & all_ties), 0.0, NEG_INF)
    need_ties = jnp.max(((cnt_eq > room) & live).astype(I32)) > 0

    @pl.when(need_ties)
    def _():
        tri = (lax.broadcasted_iota(I32, (TIE_BLOCK, TIE_BLOCK), 0)
               >= lax.broadcasted_iota(I32, (TIE_BLOCK, TIE_BLOCK), 1)).astype(BF16)

        def blk(b, carry):
            rows = pl.ds(pl.multiple_of(b * TIE_BLOCK, TIE_BLOCK), TIE_BLOCK)
            kblk = key_scr[rows, :]
            eqb = (kblk == thr) & live
            eqf = eqb.astype(F32)
            prefix = jnp.dot(tri, eqf.astype(BF16), preferred_element_type=F32) + carry
            keep = (kblk > thr) | (eqb & (prefix <= room.astype(F32)))
            mask_scr[rows, :] = jnp.where(keep, 0.0, NEG_INF)
            return carry + jnp.sum(eqf, axis=0, keepdims=True)

        lax.fori_loop(0, n_keys // TIE_BLOCK, blk, jnp.zeros((1, qs), F32))

    ab = ATT_BLOCK if n_keys % ATT_BLOCK == 0 else SUB
    for h in range(A_HEADS):
        sl = slice(h * A_HEAD_DIM, (h + 1) * A_HEAD_DIM)
        qh = q_ref[:, sl].astype(BF16)
        parts = []
        for b in range(n_keys // ab):
            rows = slice(b * ab, (b + 1) * ab)
            lg = _nt_dot(kb_scr[rows, sl], qh) * (A_HEAD_DIM ** -0.5) + mask_scr[rows, :]
            m_b = jnp.max(lg, axis=0, keepdims=True)
            e = jnp.exp(lg - jnp.where(m_b == NEG_INF, 0.0, m_b))
            den_b = jnp.sum(e, axis=0, keepdims=True)
            o_b = jnp.dot(vt_scr[sl, rows], e.astype(BF16), preferred_element_type=F32)
            parts.append((m_b, den_b, o_b))
        m = functools.reduce(jnp.maximum, [p[0] for p in parts])
        den = jnp.zeros((1, qs), F32)
        o_t = jnp.zeros((A_HEAD_DIM, qs), F32)
        for m_b, den_b, o_b in parts:
            wgt = jnp.exp(m_b - m)
            den = den + wgt * den_b
            o_t = o_t + wgt * o_b
        o_ref[:, sl] = (o_t / den).T


def _dsa_prompt_kernel(q_ref, qi_ref, mq_ref, k_ref, v_ref, mk_ref, bufk_ref, bufv_ref, bufi_ref,
                       o_ref, ck_ref, cv_ref, ci_ref,
                       kb_scr, vt_scr, kib_scr, w_scr, score_scr, key_scr, mask_scr,
                       *, s_keys, qs, kb, topk):
    del bufk_ref, bufv_ref, bufi_ref
    j = pl.program_id(1)

    @pl.when(j == 0)
    def _():
        kb_scr[...] = k_ref[...].astype(BF16)
        kib_scr[...] = mk_ref[:, 0:IDX_DIM].astype(BF16)
        for i in range(s_keys // kb):
            vt_scr[:, i * kb:(i + 1) * kb] = v_ref[i * kb:(i + 1) * kb, :].T.astype(BF16)
        ci_ref[...] = mk_ref[:, 0:IDX_DIM]
        for h in range(A_HEADS):
            sl = slice(h * A_HEAD_DIM, (h + 1) * A_HEAD_DIM)
            _head_rows(ck_ref, h, s_keys, A_HEADS)[...] = k_ref[:, sl]
            _head_rows(cv_ref, h, s_keys, A_HEADS)[...] = v_ref[:, sl]

    q_pos = j * qs + lax.broadcasted_iota(I32, (1, qs), 1)
    limit = ((q_pos >> 6) + 1) * CHUNK
    nsb = (j * qs + qs + kb - 1) // kb
    for n in range(1, s_keys // kb + 1):
        pl.when(nsb == n)(functools.partial(
            _dsa_select_attend, q_ref, qi_ref, mq_ref, o_ref, kb_scr, vt_scr, kib_scr, w_scr, score_scr,
            key_scr, mask_scr, n_keys=n * kb, qs=qs, topk=topk, limit=limit))


def _dsa_sample_kernel(q_ref, qi_ref, mq_ref, kc_ref, vc_ref, kic_ref, kn_ref, vn_ref, mkn_ref,
                       bufo_ref, bufk_ref, bufv_ref, bufi_ref,
                       o_ref, ck_ref, cv_ref, ci_ref,
                       kb_scr, vt_scr, kib_scr, w_scr, score_scr, key_scr, mask_scr,
                       vrow_scr, *, s_keys, qs, kb, topk, past):
    del bufo_ref, bufk_ref, bufv_ref, bufi_ref
    n_new = kn_ref.shape[0]
    live = past + n_new
    for h in range(A_HEADS):
        sl = slice(h * A_HEAD_DIM, (h + 1) * A_HEAD_DIM)
        kb_scr[0:past, sl] = _head_rows(kc_ref, h, past, A_HEADS)[...].astype(BF16)
        vrow_scr[0:past, sl] = _head_rows(vc_ref, h, past, A_HEADS)[...]
        _head_rows(ck_ref, h, n_new, A_HEADS)[...] = kn_ref[:, sl]
        _head_rows(cv_ref, h, n_new, A_HEADS)[...] = vn_ref[:, sl]
    ci_ref[...] = mkn_ref[:, 0:IDX_DIM]
    kb_scr[past:live, :] = kn_ref[...].astype(BF16)
    kib_scr[0:past, :] = kic_ref[...].astype(BF16)
    kib_scr[past:live, :] = mkn_ref[:, 0:IDX_DIM].astype(BF16)
    vrow_scr[past:live, :] = vn_ref[...]
    if s_keys > live:
        kb_scr[live:, :] = jnp.zeros((s_keys - live, A_WIDTH), BF16)
        kib_scr[live:, :] = jnp.zeros((s_keys - live, IDX_DIM), BF16)
        vrow_scr[live:, :] = jnp.zeros((s_keys - live, A_WIDTH), F32)
    for i in range(s_keys // kb):
        vt_scr[:, i * kb:(i + 1) * kb] = vrow_scr[i * kb:(i + 1) * kb, :].T.astype(BF16)

    limit = jnp.full((1, qs), live, I32)
    _dsa_select_attend(q_ref, qi_ref, mq_ref, o_ref, kb_scr, vt_scr, kib_scr, w_scr, score_scr,
                       key_scr, mask_scr, n_keys=s_keys, qs=qs, topk=topk, limit=limit)


def _dsa_scratch(s_keys, qs):
    return [
        pltpu.VMEM((s_keys, A_WIDTH), BF16),
        pltpu.VMEM((A_WIDTH, s_keys), BF16),
        pltpu.VMEM((s_keys, IDX_DIM), BF16),
        pltpu.VMEM((LANES, qs), F32),
        pltpu.VMEM((s_keys, qs), F32),
        pltpu.VMEM((s_keys, qs), I32),
        pltpu.VMEM((s_keys, qs), F32),
    ]


def _dsa_prompt(proj, out_k, out_v, out_ki, layer, *, nb, tq, qs, kb):
    nsteps = tq // qs
    qcol = lambda w, off: pl.BlockSpec((qs, w), lambda b, j: (b * nsteps + j, off // w))
    kcol = lambda w, off: pl.BlockSpec((tq, w), lambda b, j: (b, off // w))
    cache = lambda rows, w: pl.BlockSpec((rows, w), lambda b, j: (layer * nb + b, 0))
    kern = functools.partial(_dsa_prompt_kernel, s_keys=tq, qs=qs, kb=kb, topk=min(DSA_TOPK, tq // 4))
    return pl.pallas_call(
        kern,
        grid=(nb, nsteps),
        in_specs=[qcol(A_WIDTH, COL_AQ), qcol(IQ_WIDTH, COL_IQ), qcol(LANES, COL_MISC),
                  kcol(A_WIDTH, COL_AK), kcol(A_WIDTH, COL_AV), kcol(LANES, COL_MISC), ANY, ANY, ANY],
        out_specs=[pl.BlockSpec((qs, A_WIDTH), lambda b, j: (b * nsteps + j, 0)),
                   cache(tq * A_HEADS, A_HEAD_DIM), cache(tq * A_HEADS, A_HEAD_DIM), cache(tq, IDX_DIM)],
        out_shape=[jax.ShapeDtypeStruct((proj.shape[0], A_WIDTH), F32),
                   jax.ShapeDtypeStruct(out_k.shape, F32), jax.ShapeDtypeStruct(out_v.shape, F32),
                   jax.ShapeDtypeStruct(out_ki.shape, F32)],
        input_output_aliases={6: 1, 7: 2, 8: 3},
        scratch_shapes=_dsa_scratch(tq, qs),
        compiler_params=_cparams("parallel", "arbitrary"),
        name="dsa_prompt",
    )(proj, proj, proj, proj, proj, proj, out_k, out_v, out_ki)


def _dsa_sample(proj, cache_k, cache_v, cache_ki, buf, out_k, out_v, out_ki, layer,
                *, nb, ts, past, row0, kb):
    rb0 = row0 // ts
    live = past + ts
    s_keys = -(-live // kb) * kb
    qcol = lambda w, off: pl.BlockSpec((ts, w), lambda b, j: (rb0 + b, off // w))
    per_seq = lambda rows, w: pl.BlockSpec((rows, w), lambda b, j: (layer * nb + b, 0))
    kern = functools.partial(_dsa_sample_kernel, s_keys=s_keys, qs=ts, kb=kb,
                             topk=min(DSA_TOPK, live // 4), past=past)
    return pl.pallas_call(
        kern,
        grid=(nb, 1),
        in_specs=[qcol(A_WIDTH, COL_AQ), qcol(IQ_WIDTH, COL_IQ), qcol(LANES, COL_MISC),
                  per_seq(past * A_HEADS, A_HEAD_DIM), per_seq(past * A_HEADS, A_HEAD_DIM),
                  per_seq(past, IDX_DIM),
                  qcol(A_WIDTH, COL_AK), qcol(A_WIDTH, COL_AV), qcol(LANES, COL_MISC),
                  ANY, ANY, ANY, ANY],
        out_specs=[pl.BlockSpec((ts, A_WIDTH), lambda b, j: (rb0 + b, 0)),
                   per_seq(ts * A_HEADS, A_HEAD_DIM), per_seq(ts * A_HEADS, A_HEAD_DIM),
                   per_seq(ts, IDX_DIM)],
        out_shape=[jax.ShapeDtypeStruct(buf.shape, F32), jax.ShapeDtypeStruct(out_k.shape, F32),
                   jax.ShapeDtypeStruct(out_v.shape, F32), jax.ShapeDtypeStruct(out_ki.shape, F32)],
        input_output_aliases={9: 0, 10: 1, 11: 2, 12: 3},
        scratch_shapes=_dsa_scratch(s_keys, ts) + [pltpu.VMEM((s_keys, A_WIDTH), F32)],
        compiler_params=_cparams("parallel", "arbitrary"),
        name="dsa_sample",
    )(proj, proj, proj, cache_k, cache_v, cache_ki, proj, proj, proj, buf, out_k, out_v, out_ki)


BAND_QS_PROMPT = 256


def _band_bias_kernel(tab_ref, o_ref, *, qs, w):
    layer = pl.program_id(0)
    wp = -(-(w + qs) // LANES) * LANES
    jj = lax.broadcasted_iota(I32, (8, wp), 1)
    d = jnp.where(jj < w, jj, jj - wp)
    idx = jnp.clip(B_WIN - d, -REL_CLIP, REL_CLIP) + REL_CLIP
    q = lax.broadcasted_iota(I32, (qs, w), 0)
    jx = lax.broadcasted_iota(I32, (qs, w), 1)
    back = (q >> 6) + B_PREV_CHUNKS - (jx >> 6)
    allowed = (back >= 0) & (back <= B_PREV_CHUNKS)
    for h in range(B_HEADS):
        row = layer * B_HEADS + h

        def body(v, acc):
            return jnp.where(idx == v, tab_ref[row, v], acc)

        base = lax.fori_loop(0, 2 * REL_CLIP + 1, body, jnp.zeros((8, wp), F32))
        rows = jnp.broadcast_to(base[0:1, :], (qs, wp))
        toeplitz = pltpu.roll(rows, 0, 1, stride=1, stride_axis=0)
        o_ref[0, h] = jnp.where(allowed, toeplitz[:, 0:w], NEG_INF)


def _band_bias(b_rel, qs):
    depth = b_rel.shape[0]
    w = qs + B_WIN
    return pl.pallas_call(
        functools.partial(_band_bias_kernel, qs=qs, w=w),
        grid=(depth,),
        in_specs=[pl.BlockSpec(memory_space=pltpu.SMEM)],
        out_specs=pl.BlockSpec((1, B_HEADS, qs, w), lambda l: (l, 0, 0, 0)),
        out_shape=jax.ShapeDtypeStruct((depth, B_HEADS, qs, w), F32),
        compiler_params=_cparams("arbitrary"),
        name=f"band_bias_{qs}",
    )(b_rel.reshape(depth * B_HEADS, 2 * REL_CLIP + 1))


def _band_attend(q_ref, bias_ref, o_ref, kp_scr, vp_scr, start, valid, qs):
    w = qs + B_WIN
    for h in range(B_HEADS):
        sl = slice(h * B_HEAD_DIM, (h + 1) * B_HEAD_DIM)
        qh = q_ref[:, sl].astype(BF16)
        lg = _nt_dot(qh, kp_scr[pl.ds(start, w), sl]) * (B_HEAD_DIM ** -0.5) + bias_ref[0, h]
        if valid is not None:
            lg = jnp.where(valid, lg, NEG_INF)
        m = jnp.max(lg, axis=-1, keepdims=True)
        e = jnp.exp(lg - m)
        den = jnp.sum(e, axis=-1, keepdims=True)
        o = jnp.dot(e.astype(BF16), vp_scr[pl.ds(start, w), sl], preferred_element_type=F32)
        o_ref[:, sl] = o / den


def _band_prompt_kernel(q_ref, k_ref, v_ref, bias_ref, bufk_ref, bufv_ref, o_ref, ck_ref, cv_ref,
                        kp_scr, vp_scr, *, s_keys, qs, nkeep):
    del bufk_ref, bufv_ref
    j = pl.program_id(1)

    @pl.when(j == 0)
    def _():
        kp_scr[0:B_WIN, :] = jnp.zeros((B_WIN, B_WIDTH), BF16)
        vp_scr[0:B_WIN, :] = jnp.zeros((B_WIN, B_WIDTH), BF16)
        kp_scr[B_WIN:B_WIN + s_keys, :] = k_ref[...].astype(BF16)
        vp_scr[B_WIN:B_WIN + s_keys, :] = v_ref[...].astype(BF16)
        for h in range(B_HEADS):
            sl = slice(h * B_HEAD_DIM, (h + 1) * B_HEAD_DIM)
            _head_rows(ck_ref, h, nkeep, B_HEADS)[...] = k_ref[s_keys - nkeep:, sl]
            _head_rows(cv_ref, h, nkeep, B_HEADS)[...] = v_ref[s_keys - nkeep:, sl]

    start = pl.multiple_of(j * qs, qs)
    valid = lax.broadcasted_iota(I32, (qs, qs + B_WIN), 1) >= B_WIN - j * qs
    _band_attend(q_ref, bias_ref, o_ref, kp_scr, vp_scr, start, valid, qs)


def _band_sample_kernel(q_ref, kc_ref, vc_ref, kn_ref, vn_ref, bias_ref, bufo_ref, bufk_ref, bufv_ref,
                        o_ref, ck_ref, cv_ref, kp_scr, vp_scr, *, qs):
    del bufo_ref, bufk_ref, bufv_ref
    for h in range(B_HEADS):
        sl = slice(h * B_HEAD_DIM, (h + 1) * B_HEAD_DIM)
        kp_scr[0:B_WIN, sl] = _head_rows(kc_ref, h, B_WIN, B_HEADS)[...].astype(BF16)
        vp_scr[0:B_WIN, sl] = _head_rows(vc_ref, h, B_WIN, B_HEADS)[...].astype(BF16)
        _head_rows(ck_ref, h, qs, B_HEADS)[...] = kn_ref[:, sl]
        _head_rows(cv_ref, h, qs, B_HEADS)[...] = vn_ref[:, sl]
    kp_scr[B_WIN:B_WIN + qs, :] = kn_ref[...].astype(BF16)
    vp_scr[B_WIN:B_WIN + qs, :] = vn_ref[...].astype(BF16)
    _band_attend(q_ref, bias_ref, o_ref, kp_scr, vp_scr, 0, None, qs)


def _band_prompt(proj, bias, out_k, out_v, layer, *, nb, tq, qs):
    nsteps = tq // qs
    w = qs + B_WIN
    nkeep = min(B_WIN, tq)
    kcol = lambda off: pl.BlockSpec((tq, B_WIDTH), lambda b, j: (b, off // B_WIDTH))
    cache = pl.BlockSpec((nkeep * B_HEADS, B_HEAD_DIM), lambda b, j: (layer * nb + b, 0))
    return pl.pallas_call(
        functools.partial(_band_prompt_kernel, s_keys=tq, qs=qs, nkeep=nkeep),
        grid=(nb, nsteps),
        in_specs=[
            pl.BlockSpec((qs, B_WIDTH), lambda b, j: (b * nsteps + j, COL_BQ // B_WIDTH)),
            kcol(COL_BK), kcol(COL_BV),
            pl.BlockSpec((1, B_HEADS, qs, w), lambda b, j: (layer, 0, 0, 0)),
            ANY, ANY,
        ],
        out_specs=[pl.BlockSpec((qs, B_WIDTH), lambda b, j: (b * nsteps + j, 0)), cache, cache],
        out_shape=[jax.ShapeDtypeStruct((proj.shape[0], B_WIDTH), F32),
                   jax.ShapeDtypeStruct(out_k.shape, F32), jax.ShapeDtypeStruct(out_v.shape, F32)],
        input_output_aliases={4: 1, 5: 2},
        scratch_shapes=[pltpu.VMEM((B_WIN + tq, B_WIDTH), BF16),
                        pltpu.VMEM((B_WIN + tq, B_WIDTH), BF16)],
        compiler_params=_cparams("parallel", "arbitrary"),
        name="band_prompt",
    )(proj, proj, proj, bias, out_k, out_v)


def _band_sample(proj, cache_k, cache_v, bias, buf, out_k, out_v, layer, *, nb, ts, row0):
    rb0 = row0 // ts
    w = ts + B_WIN
    qcol = lambda off: pl.BlockSpec((ts, B_WIDTH), lambda b, j: (rb0 + b, off // B_WIDTH))
    per_seq = lambda rows: pl.BlockSpec((rows * B_HEADS, B_HEAD_DIM), lambda b, j: (layer * nb + b, 0))
    return pl.pallas_call(
        functools.partial(_band_sample_kernel, qs=ts),
        grid=(nb, 1),
        in_specs=[qcol(COL_BQ), per_seq(B_WIN), per_seq(B_WIN), qcol(COL_BK), qcol(COL_BV),
                  pl.BlockSpec((1, B_HEADS, ts, w), lambda b, j: (layer, 0, 0, 0)),
                  ANY, ANY, ANY],
        out_specs=[pl.BlockSpec((ts, B_WIDTH), lambda b, j: (rb0 + b, 0)), per_seq(ts), per_seq(ts)],
        out_shape=[jax.ShapeDtypeStruct(buf.shape, F32), jax.ShapeDtypeStruct(out_k.shape, F32),
                   jax.ShapeDtypeStruct(out_v.shape, F32)],
        input_output_aliases={6: 0, 7: 1, 8: 2},
        scratch_shapes=[pltpu.VMEM((w, B_WIDTH), BF16), pltpu.VMEM((w, B_WIDTH), BF16)],
        compiler_params=_cparams("parallel", "arbitrary"),
        name="band_sample",
    )(proj, cache_k, cache_v, proj, proj, bias, buf, out_k, out_v)


XP_OFF = 8
SSD_CHUNKS_PER_STEP = 4
SSD_SEQS_PER_STEP = 4
_HEAD_SPREAD = np.zeros((LANES, C_WIDTH), np.float32)
for _h in range(C_HEADS):
    _HEAD_SPREAD[MISC_DT + _h, _h * C_HEAD_DIM:(_h + 1) * C_HEAD_DIM] = 1.0


def _ssd_kernel(*refs, zero_init, nsteps, cps, spb):
    if zero_init:
        (z_ref, xbc_ref, misc_ref, cw_ref, cbias_ref, prm_ref, gn_ref, ex_ref, dskip_ref,
         bufh_ref, bufc_ref, o_ref, hout_ref, cnew_ref, st_scr, xp_scr, y_scr) = refs
    else:
        (z_ref, xbc_ref, misc_ref, h0_ref, cb0_ref, cw_ref, cbias_ref, prm_ref, gn_ref, ex_ref, dskip_ref,
         bufo_ref, bufh_ref, bufc_ref, o_ref, hout_ref, cnew_ref, st_scr, xp_scr, y_scr) = refs
        del bufo_ref
    del bufh_ref, bufc_ref
    c = pl.program_id(1)
    tail = CONV_W - 1
    rep = C_HEADS // C_GROUPS
    gw = C_WIDTH // C_GROUPS

    def start(s):
        if zero_init:
            st_scr[s * C_GROUPS:(s + 1) * C_GROUPS] = jnp.zeros((C_GROUPS, C_STATE, gw), F32)
            xp_scr[s, XP_OFF - tail:XP_OFF, :] = jnp.zeros((tail, CONV_DIM), F32)
        else:
            for h in range(C_HEADS):
                st_scr[s * C_GROUPS + h // rep, :, (h % rep) * C_HEAD_DIM:(h % rep + 1) * C_HEAD_DIM] = (
                    h0_ref[s, h].T)
            xp_scr[s, XP_OFF - tail:XP_OFF, :] = cb0_ref[s]

    def chunk(rows, s):
        xp_s, y_s, st_s = xp_scr.at[s], y_scr.at[s], st_scr.at[s * C_GROUPS:(s + 1) * C_GROUPS]
        xp_s[XP_OFF:XP_OFF + CHUNK, :] = xbc_ref[rows, :]
        y = cbias_ref[...] + xp_s[XP_OFF - tail:XP_OFF - tail + CHUNK, :] * cw_ref[0:1, :]
        for t in range(1, CONV_W):
            y = y + xp_s[XP_OFF - tail + t:XP_OFF - tail + t + CHUNK, :] * cw_ref[t:t + 1, :]
        new_tail = xp_s[XP_OFF + CHUNK - tail:XP_OFF + CHUNK, :]
        xp_s[XP_OFF - tail:XP_OFF, :] = new_tail
        cnew_ref[s] = new_tail
        xc = y * (1.0 / (1.0 + jnp.exp(-y)))
        xs = xc[:, 0:C_WIDTH]

        pre = misc_ref[rows, :] + prm_ref[0:1, :]
        dt = jnp.maximum(pre, 0.0) + jnp.log(1.0 + jnp.exp(-jnp.abs(pre)))
        ad = dt * (-jnp.exp(prm_ref[1:2, :]))
        li = lax.broadcasted_iota(I32, (CHUNK, CHUNK), 0)
        si = lax.broadcasted_iota(I32, (CHUNK, CHUNK), 1)
        causal = li >= si
        tri = causal.astype(BF16)
        acs = sum(jnp.dot(tri, part, preferred_element_type=F32) for part in _split3(ad))
        acs_t = acs.T
        ex = ex_ref[...]
        spread = lambda v: sum(jnp.dot(part, ex, preferred_element_type=F32) for part in _split3(v))
        dt_x = spread(dt)
        acs_x = spread(acs)
        acs_last_x = acs_x[CHUNK - 1:CHUNK, :]
        xd_all = xs * dt_x
        xd_b = xd_all.astype(BF16)
        xde_b = (xd_all * jnp.exp(acs_last_x - acs_x)).astype(BF16)
        grow_x = jnp.exp(acs_x)
        chunk_decay_x = jnp.exp(acs_last_x)
        head_of_lane = lax.broadcasted_iota(I32, (CHUNK, gw), 1) >> 6

        for g in range(C_GROUPS):
            gl = slice(g * gw, (g + 1) * gw)
            b_g = xc[:, C_WIDTH + g * C_STATE:C_WIDTH + (g + 1) * C_STATE]
            c_g = xc[:, C_WIDTH + (C_GROUPS + g) * C_STATE:C_WIDTH + (C_GROUPS + g + 1) * C_STATE]
            b_gt = b_g.T.astype(BF16)
            c_gb = c_g.astype(BF16)
            cb = jnp.dot(c_gb, b_gt, preferred_element_type=F32)
            st = st_s[g]
            y_g = jnp.dot(c_gb, st.astype(BF16), preferred_element_type=F32) * grow_x[:, gl]
            xd_g = xd_b[:, gl]
            for r in range(rep):
                col = MISC_DT + g * rep + r
                seg = acs[:, col:col + 1] - acs_t[col:col + 1, :]
                decay_in = jnp.where(causal, jnp.exp(jnp.where(causal, seg, 0.0)), 0.0)
                t_r = jnp.dot((cb * decay_in).astype(BF16), xd_g, preferred_element_type=F32)
                y_g = y_g + jnp.where(head_of_lane == r, t_r, 0.0)
            y_s[:, gl] = y_g + dskip_ref[:, gl] * xs[:, gl]
            st_s[g] = st * chunk_decay_x[:, gl] + jnp.dot(b_gt, xde_b[:, gl], preferred_element_type=F32)

        z = z_ref[rows, :]
        gate = y_s[...] * (z * (1.0 / (1.0 + jnp.exp(-z))))
        for g in range(C_GROUPS):
            gg = gate[:, g * gw:(g + 1) * gw]
            ms = jnp.mean(gg * gg, axis=-1, keepdims=True)
            o_ref[rows, g * gw:(g + 1) * gw] = gg * lax.rsqrt(ms + EPS) * gn_ref[:, g * gw:(g + 1) * gw]

    def finish(s):
        for h in range(C_HEADS):
            hout_ref[s, h] = st_scr[s * C_GROUPS + h // rep, :,
                                    (h % rep) * C_HEAD_DIM:(h % rep + 1) * C_HEAD_DIM].T

    if spb == 1:
        pl.when(c == 0)(functools.partial(start, 0))
        for i in range(cps):
            chunk(slice(i * CHUNK, (i + 1) * CHUNK), 0)
        pl.when(c == nsteps - 1)(functools.partial(finish, 0))
    else:
        assert nsteps == 1 and cps == 1
        for s in range(spb):
            start(s)
            chunk(slice(s * CHUNK, (s + 1) * CHUNK), s)
            finish(s)


def _ssd(proj, conv_w, conv_b, prm, gate_norm, out_h, out_c, layer, *, nb, nchunks, row0, state=None):
    cps = SSD_CHUNKS_PER_STEP if nchunks % SSD_CHUNKS_PER_STEP == 0 else 1
    nsteps = nchunks // cps
    spb = SSD_SEQS_PER_STEP if nchunks == 1 and nb % SSD_SEQS_PER_STEP == 0 else 1
    nblk = nb // spb
    rt = cps * spb * CHUNK
    assert row0 % rt == 0
    rb0 = row0 // rt
    row = lambda b, c: rb0 + b * nsteps + c
    const = lambda shape: pl.BlockSpec(shape, lambda b, c: tuple(0 for _ in shape))
    per_seq_h = pl.BlockSpec((spb, C_HEADS, C_HEAD_DIM, C_STATE), lambda b, c: (layer * nblk + b, 0, 0, 0))
    per_seq_c = pl.BlockSpec((spb, CONV_W - 1, CONV_DIM), lambda b, c: (layer * nblk + b, 0, 0))
    in_specs = [
        pl.BlockSpec((rt, C_WIDTH), lambda b, c: (row(b, c), COL_CZ // C_WIDTH)),
        pl.BlockSpec((rt, CONV_DIM), lambda b, c: (row(b, c), COL_CXBC // CONV_DIM)),
        pl.BlockSpec((rt, LANES), lambda b, c: (row(b, c), COL_MISC // LANES)),
    ]
    args = [proj, proj, proj]
    if state is not None:
        h0, cbuf0, buf = state
        in_specs += [per_seq_h, per_seq_c]
        args += [h0, cbuf0]
    in_specs += [const((CONV_W, CONV_DIM)), const((1, CONV_DIM)), const((8, LANES)), const((1, C_WIDTH)),
                 const((LANES, C_WIDTH)), const((1, C_WIDTH))]
    args += [conv_w, conv_b.reshape(1, CONV_DIM), prm[0], gate_norm.reshape(1, C_WIDTH),
             jnp.asarray(_HEAD_SPREAD, BF16), prm[1]]
    aliases = {}
    if state is not None:
        in_specs.append(ANY)
        args.append(buf)
        aliases[len(args) - 1] = 0
    in_specs += [ANY, ANY]
    args += [out_h, out_c]
    aliases[len(args) - 2] = 1
    aliases[len(args) - 1] = 2
    return pl.pallas_call(
        functools.partial(_ssd_kernel, zero_init=state is None, nsteps=nsteps, cps=cps, spb=spb),
        grid=(nblk, nsteps),
        in_specs=in_specs,
        out_specs=[pl.BlockSpec((rt, C_WIDTH), lambda b, c: (row(b, c), 0)), per_seq_h, per_seq_c],
        out_shape=[
            jax.ShapeDtypeStruct((proj.shape[0], C_WIDTH), F32),
            jax.ShapeDtypeStruct(out_h.shape, F32),
            jax.ShapeDtypeStruct(out_c.shape, F32),
        ],
        input_output_aliases=aliases,
        scratch_shapes=[pltpu.VMEM((spb * C_GROUPS, C_STATE, C_WIDTH // C_GROUPS), F32),
                        pltpu.VMEM((spb, XP_OFF + CHUNK, CONV_DIM), F32),
                        pltpu.VMEM((spb, CHUNK, C_WIDTH), F32)],
        compiler_params=_cparams("parallel", "arbitrary"),
        name="ssd_prompt" if state is None else "ssd_sample",
    )(*args)


def _outproj_kernel(x_ref, oa_ref, ob_ref, oc_ref, w_ref, o_ref):
    acc = x_ref[...]
    acc = acc + jnp.dot(oa_ref[...].astype(BF16), w_ref[0, 0:A_WIDTH, :], preferred_element_type=F32)
    acc = acc + jnp.dot(ob_ref[...].astype(BF16), w_ref[0, A_WIDTH:A_WIDTH + B_WIDTH, :],
                        preferred_element_type=F32)
    acc = acc + jnp.dot(oc_ref[...].astype(BF16), w_ref[0, A_WIDTH + B_WIDTH:, :],
                        preferred_element_type=F32)
    o_ref[...] = acc


def _out_proj(x, oa, ob, oc, w_all, layer):
    t, d = x.shape
    tm = _row_tile(t, 512)
    rows = lambda w: pl.BlockSpec((tm, w), lambda i: (i, 0))
    return pl.pallas_call(
        _outproj_kernel,
        grid=(t // tm,),
        in_specs=[rows(d), rows(A_WIDTH), rows(B_WIDTH), rows(C_WIDTH),
                  pl.BlockSpec((1, d, d), lambda i: (layer, 0, 0))],
        out_specs=rows(d),
        out_shape=jax.ShapeDtypeStruct((t, d), F32),
        compiler_params=_cparams("parallel"),
        name="out_proj",
    )(x, oa, ob, oc, w_all)


def _ffn_kernel(x_ref, nw_ref, wu_ref, wd_ref, o_ref, h_scr):
    def step(first):
        if first:
            x = x_ref[...]
            ms = jnp.mean(x * x, axis=-1, keepdims=True)
            h_scr[...] = (x * lax.rsqrt(ms + EPS) * nw_ref[...]).astype(BF16)
        u = jnp.maximum(jnp.dot(h_scr[...], wu_ref[0], preferred_element_type=F32), 0.0)
        down = jnp.dot((u * u).astype(BF16), wd_ref[0], preferred_element_type=F32)
        if first:
            o_ref[...] = x_ref[...] + down
        else:
            o_ref[...] += down

    f = pl.program_id(1)
    pl.when(f == 0)(functools.partial(step, True))
    pl.when(f != 0)(functools.partial(step, False))


def _ffn(x, norm_w, wu_all, wd_all, layer):
    t, d = x.shape
    ff = wu_all.shape[2]
    tm = _row_tile(t, 1024)
    tf = 512
    return pl.pallas_call(
        _ffn_kernel,
        grid=(t // tm, ff // tf),
        in_specs=[
            pl.BlockSpec((tm, d), lambda i, f: (i, 0)),
            pl.BlockSpec((1, d), lambda i, f: (0, 0)),
            pl.BlockSpec((1, d, tf), lambda i, f: (layer, 0, f)),
            pl.BlockSpec((1, tf, d), lambda i, f: (layer, f, 0)),
        ],
        out_specs=pl.BlockSpec((tm, d), lambda i, f: (i, 0)),
        out_shape=jax.ShapeDtypeStruct((t, d), F32),
        scratch_shapes=[pltpu.VMEM((tm, d), BF16)],
        compiler_params=_cparams("parallel", "arbitrary"),
        name="ffn",
    )(x, norm_w.reshape(1, d), wu_all, wd_all)


def _norm_kernel(x_ref, nw_ref, o_ref):
    x = x_ref[...]
    ms = jnp.mean(x * x, axis=-1, keepdims=True)
    o_ref[...] = x * lax.rsqrt(ms + EPS) * nw_ref[...]


def _final_norm(x, norm_w, row0, nrows):
    d = x.shape[1]
    tm = _row_tile(nrows, 512)
    assert row0 % tm == 0
    return pl.pallas_call(
        _norm_kernel,
        grid=(nrows // tm,),
        in_specs=[pl.BlockSpec((tm, d), lambda i: (row0 // tm + i, 0)),
                  pl.BlockSpec((1, d), lambda i: (0, 0))],
        out_specs=pl.BlockSpec((tm, d), lambda i: (i, 0)),
        out_shape=jax.ShapeDtypeStruct((nrows, d), F32),
        compiler_params=_cparams("parallel"),
        name="final_norm",
    )(x, norm_w.reshape(1, d))


def _lane_row(v, off):
    return jnp.zeros((LANES,), F32).at[off:off + v.shape[0]].set(v)


def _trunk(x_prompt, x_sample, cache_a_k, cache_a_v, cache_a_kidx, cache_b_k, cache_b_v,
           state_ssm, state_conv, norm1, w_in, w_out, b_rel, conv_w, conv_b, dt_bias,
           a_log, d_skip, gate_norm, norm2, w_up, w_down, final_norm):
    bp, tp, d = x_prompt.shape
    bs, ts, _ = x_sample.shape
    depth = w_in.shape[0]
    past = cache_a_k.shape[2]
    nbuf = cache_b_k.shape[2]
    n_p = bp * tp
    n_s = bs * ts
    assert ts == CHUNK and tp % DSA_KB_PROMPT == 0 and tp % BAND_QS_PROMPT == 0 and nbuf == B_WIN
    qs_p = LANES
    nkeep = min(B_WIN, tp)

    x = jnp.concatenate([x_prompt.reshape(n_p, d), x_sample.reshape(n_s, d)], axis=0)

    tm_in = _row_tile(math.gcd(tp, n_p + n_s), 1024)
    assert tm_in % ts == 0
    pos = jnp.concatenate([jnp.arange(tp), jnp.tile(past + jnp.arange(ts), tm_in // ts)])
    tab_a = _rope_tables(pos, A_HEAD_DIM)
    tab_i = _rope_tables(pos, IDX_DIM)
    w_out_b, w_up_b, w_down_b = w_out.astype(BF16), w_up.astype(BF16), w_down.astype(BF16)
    bias_p = _band_bias(b_rel, BAND_QS_PROMPT)
    bias_s = _band_bias(b_rel, ts)
    w_in_r = _regroup_w_in(w_in)
    cak = cache_a_k.reshape(depth * bs * past * A_HEADS, A_HEAD_DIM)
    cav = cache_a_v.reshape(depth * bs * past * A_HEADS, A_HEAD_DIM)
    caki = cache_a_kidx.reshape(depth * bs * past, IDX_DIM)
    cbk = cache_b_k.reshape(depth * bs * nbuf * B_HEADS, B_HEAD_DIM)
    cbv = cache_b_v.reshape(depth * bs * nbuf * B_HEADS, B_HEAD_DIM)
    h0 = state_ssm.reshape(depth * bs, C_HEADS, C_HEAD_DIM, C_STATE)
    cbuf0 = state_conv.reshape(depth * bs, CONV_W - 1, CONV_DIM)

    p_ak = _uninit((depth * bp * tp * A_HEADS, A_HEAD_DIM))
    p_av = _uninit((depth * bp * tp * A_HEADS, A_HEAD_DIM))
    p_aki = _uninit((depth * bp * tp, IDX_DIM))
    p_bk = _uninit((depth * bp * nkeep * B_HEADS, B_HEAD_DIM))
    p_bv = _uninit((depth * bp * nkeep * B_HEADS, B_HEAD_DIM))
    p_h = _uninit((depth * bp, C_HEADS, C_HEAD_DIM, C_STATE))
    p_c = _uninit((depth * bp, CONV_W - 1, CONV_DIM))
    s_ak = _uninit((depth * bs * ts * A_HEADS, A_HEAD_DIM))
    s_av = _uninit((depth * bs * ts * A_HEADS, A_HEAD_DIM))
    s_aki = _uninit((depth * bs * ts, IDX_DIM))
    s_bk = _uninit((depth * bs * ts * B_HEADS, B_HEAD_DIM))
    s_bv = _uninit((depth * bs * ts * B_HEADS, B_HEAD_DIM))
    s_h = _uninit((depth * bs, C_HEADS, C_HEAD_DIM, C_STATE))
    s_c = _uninit((depth * bs, CONV_W - 1, CONV_DIM))

    for l in range(depth):
        proj = _in_proj(x, norm1[l], w_in_r, l, tab_a, tab_i, tm=tm_in, n_prompt=n_p, t_prompt=tp)

        oa, p_ak, p_av, p_aki = _dsa_prompt(proj, p_ak, p_av, p_aki, l, nb=bp, tq=tp, qs=qs_p,
                                            kb=DSA_KB_PROMPT)
        oa, s_ak, s_av, s_aki = _dsa_sample(proj, cak, cav, caki, oa, s_ak, s_av, s_aki, l,
                                            nb=bs, ts=ts, past=past, row0=n_p, kb=DSA_KB_SAMPLE)
        ob, p_bk, p_bv = _band_prompt(proj, bias_p, p_bk, p_bv, l, nb=bp, tq=tp, qs=BAND_QS_PROMPT)
        ob, s_bk, s_bv = _band_sample(proj, cbk, cbv, bias_s, ob, s_bk, s_bv, l, nb=bs, ts=ts, row0=n_p)
        rows = jnp.zeros((8, LANES), F32)
        rows = rows.at[0].set(_lane_row(dt_bias[l], MISC_DT)).at[1].set(_lane_row(a_log[l], MISC_DT))
        prm = (rows, jnp.repeat(d_skip[l], C_HEAD_DIM).reshape(1, C_WIDTH))
        oc, p_h, p_c = _ssd(proj, conv_w[l], conv_b[l], prm, gate_norm[l], p_h, p_c, l,
                            nb=bp, nchunks=tp // CHUNK, row0=0)
        oc, s_h, s_c = _ssd(proj, conv_w[l], conv_b[l], prm, gate_norm[l], s_h, s_c, l,
                            nb=bs, nchunks=1, row0=n_p, state=(h0, cbuf0, oc))

        x = _out_proj(x, oa, ob, oc, w_out_b, l)
        x = _ffn(x, norm2[l], w_up_b, w_down_b, l)

    y_prompt = _final_norm(x, final_norm, 0, n_p).reshape(bp, tp, d)
    y_sample = _final_norm(x, final_norm, n_p, n_s).reshape(bs, ts, d)
    return (y_prompt, y_sample,
            p_ak.reshape(depth, bp, tp, A_HEADS, A_HEAD_DIM),
            p_av.reshape(depth, bp, tp, A_HEADS, A_HEAD_DIM),
            p_aki.reshape(depth, bp, tp, IDX_DIM),
            p_bk.reshape(depth, bp, nkeep, B_HEADS, B_HEAD_DIM),
            p_bv.reshape(depth, bp, nkeep, B_HEADS, B_HEAD_DIM),
            p_h.reshape(depth, bp, C_HEADS, C_HEAD_DIM, C_STATE),
            p_c.reshape(depth, bp, CONV_W - 1, CONV_DIM),
            s_ak.reshape(depth, bs, ts, A_HEADS, A_HEAD_DIM),
            s_av.reshape(depth, bs, ts, A_HEADS, A_HEAD_DIM),
            s_aki.reshape(depth, bs, ts, IDX_DIM),
            s_bk.reshape(depth, bs, ts, B_HEADS, B_HEAD_DIM),
            s_bv.reshape(depth, bs, ts, B_HEADS, B_HEAD_DIM),
            s_h.reshape(depth, bs, C_HEADS, C_HEAD_DIM, C_STATE),
            s_c.reshape(depth, bs, CONV_W - 1, CONV_DIM))


def kernel(x_prompt, x_sample, cache_a_k, cache_a_v, cache_a_kidx, cache_b_k, cache_b_v, state_ssm, state_conv, norm1, w_in, w_out, b_rel, conv_w, conv_b, dt_bias, a_log, d_skip, gate_norm, norm2, w_up, w_down, final_norm):
    return _trunk(x_prompt, x_sample, cache_a_k, cache_a_v, cache_a_kidx, cache_b_k, cache_b_v,
                  state_ssm, state_conv, norm1, w_in, w_out, b_rel, conv_w, conv_b, dt_bias,
                  a_log, d_skip, gate_norm, norm2, w_up, w_down, final_norm)
```

```python
import functools
import math

import jax
import jax.numpy as jnp
import numpy as np
from jax import lax
from jax.experimental import pallas as pl
from jax.experimental.pallas import tpu as pltpu

F32 = jnp.float32
BF16 = jnp.bfloat16
I32 = jnp.int32

D_MODEL = 2048
CHUNK = 64
A_HEADS = 4
A_HEAD_DIM = 128
A_WIDTH = A_HEADS * A_HEAD_DIM
IDX_HEADS = 16
IDX_DIM = 64
IQ_WIDTH = IDX_HEADS * IDX_DIM
DSA_TOPK = 256
B_HEADS = 4
B_HEAD_DIM = 128
B_WIDTH = B_HEADS * B_HEAD_DIM
B_PREV_CHUNKS = 8
B_WIN = B_PREV_CHUNKS * CHUNK
REL_CLIP = 128
C_WIDTH = 1024
C_HEAD_DIM = 64
C_HEADS = C_WIDTH // C_HEAD_DIM
C_GROUPS = 4
C_STATE = 128
CONV_W = 4
CONV_DIM = C_WIDTH + 2 * C_GROUPS * C_STATE
D_FF = 4 * D_MODEL
ROPE_THETA = 500000.0
EPS = 1e-5

LANES = 128
MXU_N = 256
INT_MIN = -(2 ** 31)
NEG_INF = float("-inf")

COL_AQ = 0
COL_AK = 512
COL_IQ = 1024
COL_AV = 2048
COL_BQ = 2560
COL_BK = 3072
COL_BV = 3584
COL_CXBC = 4096
COL_CZ = 6144
COL_MISC = 7168
MISC_IW = 64
MISC_DT = 80
IN_COLS_PAD = 7680
IN_TN = 1280
SRC_AQ, SRC_AV, SRC_IQ, SRC_IK, SRC_BQ, SRC_CZ, SRC_CXBC, SRC_DT, SRC_END = (
    0, 1024, 1536, 2560, 2640, 4176, 5200, 7248, 7264)

VMEM_LIMIT = 56 * 1024 * 1024
ANY = pl.BlockSpec(memory_space=pl.ANY)


def _cparams(*sem):
    return pltpu.CompilerParams(dimension_semantics=sem, vmem_limit_bytes=VMEM_LIMIT)


def _row_tile(t, cap):
    tm = cap
    while t % tm:
        tm //= 2
    return tm


def _nt_dot(a, b):
    return lax.dot_general(a, b, (((1,), (1,)), ((), ())), preferred_element_type=F32)


def _split3(x):
    hi = x.astype(BF16)
    r1 = x - hi.astype(F32)
    mid = r1.astype(BF16)
    lo = (r1 - mid.astype(F32)).astype(BF16)
    return hi, mid, lo


def _head_rows(ref, h, n, heads):
    return ref.at[pl.ds(h, n, stride=heads), :]


def _uninit_kernel(o_ref):
    del o_ref


def _uninit(shape, dtype=F32):
    return pl.pallas_call(_uninit_kernel, out_specs=ANY, out_shape=jax.ShapeDtypeStruct(shape, dtype),
                          name="alloc")()


def _regroup_kernel(w_ref, o_ref):
    def put(dst, src, width):
        o_ref[0, :, dst:dst + width] = w_ref[0, :, src:src + width].astype(BF16)

    rows = w_ref.shape[1]
    put(COL_AQ, SRC_AQ, 2 * A_WIDTH)
    put(COL_IQ, SRC_IQ, IQ_WIDTH)
    put(COL_AV, SRC_AV, A_WIDTH)
    put(COL_BQ, SRC_BQ, 3 * B_WIDTH)
    put(COL_CXBC, SRC_CXBC, CONV_DIM)
    put(COL_CZ, SRC_CZ, C_WIDTH)
    o_ref[0, :, COL_MISC:] = jnp.zeros((rows, IN_COLS_PAD - COL_MISC), BF16)
    put(COL_MISC, SRC_IK, IDX_DIM + IDX_HEADS)
    put(COL_MISC + MISC_DT, SRC_DT, C_HEADS)


def _regroup_w_in(w_in):
    depth, d, n = w_in.shape
    assert n == SRC_END
    tk = 256
    return pl.pallas_call(
        _regroup_kernel,
        grid=(depth, d // tk),
        in_specs=[pl.BlockSpec((1, tk, n), lambda l, i: (l, i, 0))],
        out_specs=pl.BlockSpec((1, tk, IN_COLS_PAD), lambda l, i: (l, i, 0)),
        out_shape=jax.ShapeDtypeStruct((depth, d, IN_COLS_PAD), BF16),
        compiler_params=_cparams("parallel", "parallel"),
        name="regroup_w_in",
    )(w_in)


def _rope_block(x, c, sm, sp, half):
    return x * c + pltpu.roll(x, LANES - half, 1) * sm + pltpu.roll(x, half, 1) * sp


def _inproj_kernel(x_ref, nw_ref, w_ref, ta_ref, ti_ref, o_ref, h_scr):
    j = pl.program_id(1)

    def project(tile):
        if tile == 0:
            x = x_ref[...]
            ms = jnp.mean(x * x, axis=-1, keepdims=True)
            h_scr[...] = (x * lax.rsqrt(ms + EPS) * nw_ref[...]).astype(BF16)
        for pair in range(IN_TN // MXU_N):
            cols = slice(pair * MXU_N, (pair + 1) * MXU_N)
            both = jnp.dot(h_scr[...], w_ref[0, :, cols], preferred_element_type=F32)
            for half in range(MXU_N // LANES):
                blk = pair * (MXU_N // LANES) + half
                sl = slice(blk * LANES, (blk + 1) * LANES)
                acc = both[:, half * LANES:(half + 1) * LANES]
                col = -1 if tile is None else tile * IN_TN + blk * LANES
                if 0 <= col < COL_IQ:
                    acc = _rope_block(acc, ta_ref[0], ta_ref[1], ta_ref[2], A_HEAD_DIM // 8)
                elif COL_IQ <= col < COL_AV:
                    acc = _rope_block(acc, ti_ref[0], ti_ref[1], ti_ref[2], IDX_DIM // 8)
                elif col == COL_MISC:
                    lane = lax.broadcasted_iota(I32, acc.shape, 1)
                    rot = _rope_block(acc, ti_ref[0], ti_ref[1], ti_ref[2], IDX_DIM // 8)
                    acc = jnp.where(lane < IDX_DIM, rot, acc)
                o_ref[:, sl] = acc

    rope_tiles = [t for t in range(IN_COLS_PAD // IN_TN)
                  if t * IN_TN < COL_AV or t * IN_TN <= COL_MISC < (t + 1) * IN_TN]
    plain = j >= 0
    for tile in rope_tiles:
        pl.when(j == tile)(functools.partial(project, tile))
        plain = plain & (j != tile)
    pl.when(plain)(functools.partial(project, None))


def _in_proj(x, norm_w, w_all, layer, tab_a, tab_i, *, tm, n_prompt, t_prompt):
    t, d = x.shape
    n = w_all.shape[2]
    per_seq = t_prompt // tm
    tab = pl.BlockSpec((3, tm, LANES),
                       lambda i, j: (0, jnp.where(i < n_prompt // tm, i % per_seq, per_seq), 0))
    return pl.pallas_call(
        _inproj_kernel,
        grid=(t // tm, n // IN_TN),
        in_specs=[
            pl.BlockSpec((tm, d), lambda i, j: (i, 0)),
            pl.BlockSpec((1, d), lambda i, j: (0, 0)),
            pl.BlockSpec((1, d, IN_TN), lambda i, j: (layer, 0, j)),
            tab, tab,
        ],
        out_specs=pl.BlockSpec((tm, IN_TN), lambda i, j: (i, j)),
        out_shape=jax.ShapeDtypeStruct((t, n), F32),
        scratch_shapes=[pltpu.VMEM((tm, d), BF16)],
        compiler_params=_cparams("parallel", "arbitrary"),
        name="in_proj",
    )(x, norm_w.reshape(1, d), w_all, tab_a, tab_i)


def _rope_tables(pos, head_dim):
    rot = head_dim // 4
    half = rot // 2
    inv = ROPE_THETA ** (-jnp.arange(half, dtype=F32) * 2.0 / rot)
    ang = pos.astype(F32)[:, None] * inv[None, :]
    cos, sin = jnp.cos(ang), jnp.sin(ang)
    n = pos.shape[0]
    rest = head_dim - rot
    c = jnp.concatenate([cos, cos, jnp.ones((n, rest), F32)], axis=1)
    sm = jnp.concatenate([-sin, jnp.zeros((n, half + rest), F32)], axis=1)
    sp = jnp.concatenate([jnp.zeros((n, half), F32), sin, jnp.zeros((n, rest), F32)], axis=1)
    tab = jnp.stack([c, sm, sp], axis=0)
    return jnp.tile(tab, (1, 1, LANES // head_dim))


TIE_BLOCK = 64
SUB = 128
DSA_KB_PROMPT = 256
DSA_KB_SAMPLE = 384
ATT_BLOCK = 256


def _tree_sum(parts):
    while len(parts) > 1:
        parts = [parts[i] + parts[i + 1] for i in range(0, len(parts) - 1, 2)] + (
            parts[-1:] if len(parts) % 2 else [])
    return parts[0]


def _count_keys(key_scr, n_keys, qs, pred):
    blocks = []
    for u in range(n_keys // SUB):
        hit = jnp.where(pred(key_scr[u * SUB:(u + 1) * SUB, :]), 1, 0).astype(I32)
        blocks.append(_tree_sum([hit[i * 8:(i + 1) * 8, :] for i in range(SUB // 8)]))
    return jnp.sum(_tree_sum(blocks), axis=0, keepdims=True)


def _kth_largest(key_scr, n_keys, qs, k):
    count = functools.partial(_count_keys, key_scr, n_keys, qs)
    ans0 = jnp.where(count(lambda x: x >= 0) >= k, 0, INT_MIN).astype(I32)

    def bit_step(i, ans):
        cand = ans | (jnp.int32(1) << (30 - i))
        return jnp.where(count(lambda x: x >= cand) >= k, cand, ans)

    return lax.fori_loop(0, 31, bit_step, ans0)


def _dsa_select_attend(q_ref, qi_ref, mq_ref, o_ref, kb_scr, vt_scr, kib_scr, w_scr, score_scr,
                       key_scr, mask_scr, *, n_keys, qs, topk, limit, nseq=1):
    w_scr[...] = mq_ref[...].T * (IDX_HEADS ** -0.5 * IDX_DIM ** -0.5)
    in_first = lax.broadcasted_iota(I32, (qs, 1), 0) < qs // nseq
    first_lanes = lax.broadcasted_iota(I32, (1, qs), 1) < qs // nseq

    def per_seq(qh):
        if nseq == 1:
            return qh.astype(BF16)
        return jnp.concatenate([jnp.where(in_first, qh, 0.0), jnp.where(in_first, 0.0, qh)],
                               axis=1).astype(BF16)

    kib = kib_scr[0:n_keys, :]
    for h in range(IDX_HEADS):
        qh = per_seq(qi_ref[:, h * IDX_DIM:(h + 1) * IDX_DIM])
        term = jnp.maximum(_nt_dot(kib, qh), 0.0) * w_scr[MISC_IW + h:MISC_IW + h + 1, :]
        if h == 0:
            score_scr[0:n_keys, :] = term
        else:
            score_scr[0:n_keys, :] += term
    sc = score_scr[0:n_keys, :]
    sc = jnp.where(sc == 0.0, 0.0, sc)
    bits = lax.bitcast_convert_type(sc, I32)
    key = bits ^ ((bits >> 31) & 0x7FFFFFFF)
    s_idx = lax.broadcasted_iota(I32, (n_keys, qs), 0)
    key_scr[0:n_keys, :] = jnp.where(s_idx < limit, key, INT_MIN)

    thr = _kth_largest(key_scr, n_keys, qs, topk)
    cnt_gt = _count_keys(key_scr, n_keys, qs, lambda x: x > thr)
    cnt_eq = _count_keys(key_scr, n_keys, qs, lambda x: x == thr)
    room = topk - cnt_gt
    live = thr > INT_MIN
    all_ties = (cnt_eq <= room) & live
    key = key_scr[0:n_keys, :]
    mask_scr[0:n_keys, :] = jnp.where((key > thr) | ((key == thr) & all_ties), 0.0, NEG_INF)
    need_ties = jnp.max(((cnt_eq > room) & live).astype(I32)) > 0

    @pl.when(need_ties)
    def _():
        tri = (lax.broadcasted_iota(I32, (TIE_BLOCK, TIE_BLOCK), 0)
               >= lax.broadcasted_iota(I32, (TIE_BLOCK, TIE_BLOCK), 1)).astype(BF16)

        def blk(b, carry):
            rows = pl.ds(pl.multiple_of(b * TIE_BLOCK, TIE_BLOCK), TIE_BLOCK)
            kblk = key_scr[rows, :]
            eqb = (kblk == thr) & live
            eqf = eqb.astype(F32)
            prefix = jnp.dot(tri, eqf.astype(BF16), preferred_element_type=F32) + carry
            keep = (kblk > thr) | (eqb & (prefix <= room.astype(F32)))
            mask_scr[rows, :] = jnp.where(keep, 0.0, NEG_INF)
            return carry + jnp.sum(eqf, axis=0, keepdims=True)

        lax.fori_loop(0, n_keys // TIE_BLOCK, blk, jnp.zeros((1, qs), F32))

    ab = ATT_BLOCK if n_keys % ATT_BLOCK == 0 else SUB
    kw = nseq * A_HEAD_DIM
    for h in range(A_HEADS):
        sl = slice(h * A_HEAD_DIM, (h + 1) * A_HEAD_DIM)
        ksl = slice(h * kw, (h + 1) * kw)
        qh = per_seq(q_ref[:, sl])
        parts = []
        for b in range(n_keys // ab):
            rows = slice(b * ab, (b + 1) * ab)
            lg = _nt_dot(kb_scr[rows, ksl], qh) * (A_HEAD_DIM ** -0.5) + mask_scr[rows, :]
            m_b = jnp.max(lg, axis=0, keepdims=True)
            e = jnp.exp(lg - jnp.where(m_b == NEG_INF, 0.0, m_b))
            den_b = jnp.sum(e, axis=0, keepdims=True)
            o_b = jnp.dot(vt_scr[ksl, rows], e.astype(BF16), preferred_element_type=F32)
            if nseq == 2:
                o_b = jnp.where(first_lanes, o_b[0:A_HEAD_DIM], o_b[A_HEAD_DIM:])
            parts.append((m_b, den_b, o_b))
        m = functools.reduce(jnp.maximum, [p[0] for p in parts])
        den = jnp.zeros((1, qs), F32)
        o_t = jnp.zeros((A_HEAD_DIM, qs), F32)
        for m_b, den_b, o_b in parts:
            wgt = jnp.exp(m_b - m)
            den = den + wgt * den_b
            o_t = o_t + wgt * o_b
        o_ref[:, sl] = (o_t / den).T


def _dsa_prompt_kernel(q_ref, qi_ref, mq_ref, k_ref, v_ref, mk_ref, bufk_ref, bufv_ref, bufi_ref,
                       o_ref, ck_ref, cv_ref, ci_ref,
                       kb_scr, vt_scr, kib_scr, w_scr, score_scr, key_scr, mask_scr,
                       *, s_keys, qs, kb, topk):
    del bufk_ref, bufv_ref, bufi_ref
    j = pl.program_id(1)

    @pl.when(j == 0)
    def _():
        kb_scr[...] = k_ref[...].astype(BF16)
        kib_scr[...] = mk_ref[:, 0:IDX_DIM].astype(BF16)
        for i in range(s_keys // kb):
            vt_scr[:, i * kb:(i + 1) * kb] = v_ref[i * kb:(i + 1) * kb, :].T.astype(BF16)
        ci_ref[...] = mk_ref[:, 0:IDX_DIM]
        for h in range(A_HEADS):
            sl = slice(h * A_HEAD_DIM, (h + 1) * A_HEAD_DIM)
            _head_rows(ck_ref, h, s_keys, A_HEADS)[...] = k_ref[:, sl]
            _head_rows(cv_ref, h, s_keys, A_HEADS)[...] = v_ref[:, sl]

    q_pos = j * qs + lax.broadcasted_iota(I32, (1, qs), 1)
    limit = ((q_pos >> 6) + 1) * CHUNK
    nsb = (j * qs + qs + kb - 1) // kb
    for n in range(1, s_keys // kb + 1):
        pl.when(nsb == n)(functools.partial(
            _dsa_select_attend, q_ref, qi_ref, mq_ref, o_ref, kb_scr, vt_scr, kib_scr, w_scr, score_scr,
            key_scr, mask_scr, n_keys=n * kb, qs=qs, topk=topk, limit=limit))


def _dsa_sample_kernel(q_ref, qi_ref, mq_ref, kc_ref, vc_ref, kic_ref, kn_ref, vn_ref, mkn_ref,
                       bufo_ref, bufk_ref, bufv_ref, bufi_ref,
                       o_ref, ck_ref, cv_ref, ci_ref,
                       kb_scr, vt_scr, kib_scr, w_scr, score_scr, key_scr, mask_scr,
                       vrow_scr, *, s_keys, qs, kb, topk, past, nseq):
    del bufo_ref, bufk_ref, bufv_ref, bufi_ref
    n_new = qs // nseq
    live = past + n_new
    for s in range(nseq):
        new = slice(s * n_new, (s + 1) * n_new)
        for h in range(A_HEADS):
            sl = slice(h * A_HEAD_DIM, (h + 1) * A_HEAD_DIM)
            dst = slice((h * nseq + s) * A_HEAD_DIM, (h * nseq + s + 1) * A_HEAD_DIM)
            old_rows = pl.ds(s * past * A_HEADS + h, past, stride=A_HEADS)
            new_rows = pl.ds(s * n_new * A_HEADS + h, n_new, stride=A_HEADS)
            kb_scr[0:past, dst] = kc_ref[old_rows, :].astype(BF16)
            kb_scr[past:live, dst] = kn_ref[new, sl].astype(BF16)
            vrow_scr[0:past, dst] = vc_ref[old_rows, :]
            vrow_scr[past:live, dst] = vn_ref[new, sl]
            ck_ref[new_rows, :] = kn_ref[new, sl]
            cv_ref[new_rows, :] = vn_ref[new, sl]
        isl = slice(s * IDX_DIM, (s + 1) * IDX_DIM)
        kib_scr[0:past, isl] = kic_ref[s * past:(s + 1) * past, :].astype(BF16)
        kib_scr[past:live, isl] = mkn_ref[new, 0:IDX_DIM].astype(BF16)
    ci_ref[...] = mkn_ref[:, 0:IDX_DIM]
    if s_keys > live:
        kb_scr[live:, :] = jnp.zeros((s_keys - live, nseq * A_WIDTH), BF16)
        kib_scr[live:, :] = jnp.zeros((s_keys - live, nseq * IDX_DIM), BF16)
        vrow_scr[live:, :] = jnp.zeros((s_keys - live, nseq * A_WIDTH), F32)
    for i in range(s_keys // kb):
        vt_scr[:, i * kb:(i + 1) * kb] = vrow_scr[i * kb:(i + 1) * kb, :].T.astype(BF16)

    limit = jnp.full((1, qs), live, I32)
    _dsa_select_attend(q_ref, qi_ref, mq_ref, o_ref, kb_scr, vt_scr, kib_scr, w_scr, score_scr,
                       key_scr, mask_scr, n_keys=s_keys, qs=qs, topk=topk, limit=limit, nseq=nseq)


def _dsa_scratch(s_keys, qs, nseq=1):
    return [
        pltpu.VMEM((s_keys, nseq * A_WIDTH), BF16),
        pltpu.VMEM((nseq * A_WIDTH, s_keys), BF16),
        pltpu.VMEM((s_keys, nseq * IDX_DIM), BF16),
        pltpu.VMEM((LANES, qs), F32),
        pltpu.VMEM((s_keys, qs), F32),
        pltpu.VMEM((s_keys, qs), I32),
        pltpu.VMEM((s_keys, qs), F32),
    ]


def _dsa_prompt(proj, out_k, out_v, out_ki, layer, *, nb, tq, qs, kb):
    nsteps = tq // qs
    qcol = lambda w, off: pl.BlockSpec((qs, w), lambda b, j: (b * nsteps + j, off // w))
    kcol = lambda w, off: pl.BlockSpec((tq, w), lambda b, j: (b, off // w))
    cache = lambda rows, w: pl.BlockSpec((rows, w), lambda b, j: (layer * nb + b, 0))
    kern = functools.partial(_dsa_prompt_kernel, s_keys=tq, qs=qs, kb=kb, topk=min(DSA_TOPK, tq // 4))
    return pl.pallas_call(
        kern,
        grid=(nb, nsteps),
        in_specs=[qcol(A_WIDTH, COL_AQ), qcol(IQ_WIDTH, COL_IQ), qcol(LANES, COL_MISC),
                  kcol(A_WIDTH, COL_AK), kcol(A_WIDTH, COL_AV), kcol(LANES, COL_MISC), ANY, ANY, ANY],
        out_specs=[pl.BlockSpec((qs, A_WIDTH), lambda b, j: (b * nsteps + j, 0)),
                   cache(tq * A_HEADS, A_HEAD_DIM), cache(tq * A_HEADS, A_HEAD_DIM), cache(tq, IDX_DIM)],
        out_shape=[jax.ShapeDtypeStruct((proj.shape[0], A_WIDTH), F32),
                   jax.ShapeDtypeStruct(out_k.shape, F32), jax.ShapeDtypeStruct(out_v.shape, F32),
                   jax.ShapeDtypeStruct(out_ki.shape, F32)],
        input_output_aliases={6: 1, 7: 2, 8: 3},
        scratch_shapes=_dsa_scratch(tq, qs),
        compiler_params=_cparams("parallel", "arbitrary"),
        name="dsa_prompt",
    )(proj, proj, proj, proj, proj, proj, out_k, out_v, out_ki)


def _dsa_sample(proj, cache_k, cache_v, cache_ki, buf, out_k, out_v, out_ki, layer,
                *, nb, ts, past, row0, kb):
    nseq = 2 if nb % 2 == 0 and (2 * ts) % LANES == 0 else 1
    qs = nseq * ts
    ngrp = nb // nseq
    assert row0 % qs == 0
    rb0 = row0 // qs
    live = past + ts
    s_keys = -(-live // kb) * kb
    qcol = lambda w, off: pl.BlockSpec((qs, w), lambda b, j: (rb0 + b, off // w))
    per_seq = lambda rows, w: pl.BlockSpec((nseq * rows, w), lambda b, j: (layer * ngrp + b, 0))
    kern = functools.partial(_dsa_sample_kernel, s_keys=s_keys, qs=qs, kb=kb,
                             topk=min(DSA_TOPK, live // 4), past=past, nseq=nseq)
    return pl.pallas_call(
        kern,
        grid=(ngrp, 1),
        in_specs=[qcol(A_WIDTH, COL_AQ), qcol(IQ_WIDTH, COL_IQ), qcol(LANES, COL_MISC),
                  per_seq(past * A_HEADS, A_HEAD_DIM), per_seq(past * A_HEADS, A_HEAD_DIM),
                  per_seq(past, IDX_DIM),
                  qcol(A_WIDTH, COL_AK), qcol(A_WIDTH, COL_AV), qcol(LANES, COL_MISC),
                  ANY, ANY, ANY, ANY],
        out_specs=[pl.BlockSpec((qs, A_WIDTH), lambda b, j: (rb0 + b, 0)),
                   per_seq(ts * A_HEADS, A_HEAD_DIM), per_seq(ts * A_HEADS, A_HEAD_DIM),
                   per_seq(ts, IDX_DIM)],
        out_shape=[jax.ShapeDtypeStruct(buf.shape, F32), jax.ShapeDtypeStruct(out_k.shape, F32),
                   jax.ShapeDtypeStruct(out_v.shape, F32), jax.ShapeDtypeStruct(out_ki.shape, F32)],
        input_output_aliases={9: 0, 10: 1, 11: 2, 12: 3},
        scratch_shapes=_dsa_scratch(s_keys, qs, nseq) + [pltpu.VMEM((s_keys, nseq * A_WIDTH), F32)],
        compiler_params=_cparams("parallel", "arbitrary"),
        name="dsa_sample",
    )(proj, proj, proj, cache_k, cache_v, cache_ki, proj, proj, proj, buf, out_k, out_v, out_ki)


BAND_QS_PROMPT = 256


def _band_bias_kernel(tab_ref, o_ref, *, qs, w):
    layer = pl.program_id(0)
    wp = -(-(w + qs) // LANES) * LANES
    jj = lax.broadcasted_iota(I32, (8, wp), 1)
    d = jnp.where(jj < w, jj, jj - wp)
    idx = jnp.clip(B_WIN - d, -REL_CLIP, REL_CLIP) + REL_CLIP
    q = lax.broadcasted_iota(I32, (qs, w), 0)
    jx = lax.broadcasted_iota(I32, (qs, w), 1)
    back = (q >> 6) + B_PREV_CHUNKS - (jx >> 6)
    allowed = (back >= 0) & (back <= B_PREV_CHUNKS)
    for h in range(B_HEADS):
        row = layer * B_HEADS + h

        def body(v, acc):
            return jnp.where(idx == v, tab_ref[row, v], acc)

        base = lax.fori_loop(0, 2 * REL_CLIP + 1, body, jnp.zeros((8, wp), F32))
        rows = jnp.broadcast_to(base[0:1, :], (qs, wp))
        toeplitz = pltpu.roll(rows, 0, 1, stride=1, stride_axis=0)
        o_ref[0, h] = jnp.where(allowed, toeplitz[:, 0:w], NEG_INF)


def _band_bias(b_rel, qs):
    depth = b_rel.shape[0]
    w = qs + B_WIN
    return pl.pallas_call(
        functools.partial(_band_bias_kernel, qs=qs, w=w),
        grid=(depth,),
        in_specs=[pl.BlockSpec(memory_space=pltpu.SMEM)],
        out_specs=pl.BlockSpec((1, B_HEADS, qs, w), lambda l: (l, 0, 0, 0)),
        out_shape=jax.ShapeDtypeStruct((depth, B_HEADS, qs, w), F32),
        compiler_params=_cparams("arbitrary"),
        name=f"band_bias_{qs}",
    )(b_rel.reshape(depth * B_HEADS, 2 * REL_CLIP + 1))


def _band_attend(q_ref, bias_ref, o_ref, kp_scr, vp_scr, start, valid, qs):
    w = qs + B_WIN
    for h in range(B_HEADS):
        sl = slice(h * B_HEAD_DIM, (h + 1) * B_HEAD_DIM)
        qh = q_ref[:, sl].astype(BF16)
        lg = _nt_dot(qh, kp_scr[pl.ds(start, w), sl]) * (B_HEAD_DIM ** -0.5) + bias_ref[0, h]
        if valid is not None:
            lg = jnp.where(valid, lg, NEG_INF)
        m = jnp.max(lg, axis=-1, keepdims=True)
        e = jnp.exp(lg - m)
        den = jnp.sum(e, axis=-1, keepdims=True)
        o = jnp.dot(e.astype(BF16), vp_scr[pl.ds(start, w), sl], preferred_element_type=F32)
        o_ref[:, sl] = o / den


def _band_prompt_kernel(q_ref, k_ref, v_ref, bias_ref, bufk_ref, bufv_ref, o_ref, ck_ref, cv_ref,
                        kp_scr, vp_scr, *, s_keys, qs, nkeep):
    del bufk_ref, bufv_ref
    j = pl.program_id(1)

    @pl.when(j == 0)
    def _():
        kp_scr[0:B_WIN, :] = jnp.zeros((B_WIN, B_WIDTH), BF16)
        vp_scr[0:B_WIN, :] = jnp.zeros((B_WIN, B_WIDTH), BF16)
        kp_scr[B_WIN:B_WIN + s_keys, :] = k_ref[...].astype(BF16)
        vp_scr[B_WIN:B_WIN + s_keys, :] = v_ref[...].astype(BF16)
        for h in range(B_HEADS):
            sl = slice(h * B_HEAD_DIM, (h + 1) * B_HEAD_DIM)
            _head_rows(ck_ref, h, nkeep, B_HEADS)[...] = k_ref[s_keys - nkeep:, sl]
            _head_rows(cv_ref, h, nkeep, B_HEADS)[...] = v_ref[s_keys - nkeep:, sl]

    start = pl.multiple_of(j * qs, qs)
    valid = lax.broadcasted_iota(I32, (qs, qs + B_WIN), 1) >= B_WIN - j * qs
    _band_attend(q_ref, bias_ref, o_ref, kp_scr, vp_scr, start, valid, qs)


def _band_sample_kernel(q_ref, kc_ref, vc_ref, kn_ref, vn_ref, bias_ref, bufo_ref, bufk_ref, bufv_ref,
                        o_ref, ck_ref, cv_ref, kp_scr, vp_scr, *, qs):
    del bufo_ref, bufk_ref, bufv_ref
    for h in range(B_HEADS):
        sl = slice(h * B_HEAD_DIM, (h + 1) * B_HEAD_DIM)
        kp_scr[0:B_WIN, sl] = _head_rows(kc_ref, h, B_WIN, B_HEADS)[...].astype(BF16)
        vp_scr[0:B_WIN, sl] = _head_rows(vc_ref, h, B_WIN, B_HEADS)[...].astype(BF16)
        _head_rows(ck_ref, h, qs, B_HEADS)[...] = kn_ref[:, sl]
        _head_rows(cv_ref, h, qs, B_HEADS)[...] = vn_ref[:, sl]
    kp_scr[B_WIN:B_WIN + qs, :] = kn_ref[...].astype(BF16)
    vp_scr[B_WIN:B_WIN + qs, :] = vn_ref[...].astype(BF16)
    _band_attend(q_ref, bias_ref, o_ref, kp_scr, vp_scr, 0, None, qs)


def _band_prompt(proj, bias, out_k, out_v, layer, *, nb, tq, qs):
    nsteps = tq // qs
    w = qs + B_WIN
    nkeep = min(B_WIN, tq)
    kcol = lambda off: pl.BlockSpec((tq, B_WIDTH), lambda b, j: (b, off // B_WIDTH))
    cache = pl.BlockSpec((nkeep * B_HEADS, B_HEAD_DIM), lambda b, j: (layer * nb + b, 0))
    return pl.pallas_call(
        functools.partial(_band_prompt_kernel, s_keys=tq, qs=qs, nkeep=nkeep),
        grid=(nb, nsteps),
        in_specs=[
            pl.BlockSpec((qs, B_WIDTH), lambda b, j: (b * nsteps + j, COL_BQ // B_WIDTH)),
            kcol(COL_BK), kcol(COL_BV),
            pl.BlockSpec((1, B_HEADS, qs, w), lambda b, j: (layer, 0, 0, 0)),
            ANY, ANY,
        ],
        out_specs=[pl.BlockSpec((qs, B_WIDTH), lambda b, j: (b * nsteps + j, 0)), cache, cache],
        out_shape=[jax.ShapeDtypeStruct((proj.shape[0], B_WIDTH), F32),
                   jax.ShapeDtypeStruct(out_k.shape, F32), jax.ShapeDtypeStruct(out_v.shape, F32)],
        input_output_aliases={4: 1, 5: 2},
        scratch_shapes=[pltpu.VMEM((B_WIN + tq, B_WIDTH), BF16),
                        pltpu.VMEM((B_WIN + tq, B_WIDTH), BF16)],
        compiler_params=_cparams("parallel", "arbitrary"),
        name="band_prompt",
    )(proj, proj, proj, bias, out_k, out_v)


def _band_sample(proj, cache_k, cache_v, bias, buf, out_k, out_v, layer, *, nb, ts, row0):
    rb0 = row0 // ts
    w = ts + B_WIN
    qcol = lambda off: pl.BlockSpec((ts, B_WIDTH), lambda b, j: (rb0 + b, off // B_WIDTH))
    per_seq = lambda rows: pl.BlockSpec((rows * B_HEADS, B_HEAD_DIM), lambda b, j: (layer * nb + b, 0))
    return pl.pallas_call(
        functools.partial(_band_sample_kernel, qs=ts),
        grid=(nb, 1),
        in_specs=[qcol(COL_BQ), per_seq(B_WIN), per_seq(B_WIN), qcol(COL_BK), qcol(COL_BV),
                  pl.BlockSpec((1, B_HEADS, ts, w), lambda b, j: (layer, 0, 0, 0)),
                  ANY, ANY, ANY],
        out_specs=[pl.BlockSpec((ts, B_WIDTH), lambda b, j: (rb0 + b, 0)), per_seq(ts), per_seq(ts)],
        out_shape=[jax.ShapeDtypeStruct(buf.shape, F32), jax.ShapeDtypeStruct(out_k.shape, F32),
                   jax.ShapeDtypeStruct(out_v.shape, F32)],
        input_output_aliases={6: 0, 7: 1, 8: 2},
        scratch_shapes=[pltpu.VMEM((w, B_WIDTH), BF16), pltpu.VMEM((w, B_WIDTH), BF16)],
        compiler_params=_cparams("parallel", "arbitrary"),
        name="band_sample",
    )(proj, cache_k, cache_v, proj, proj, bias, buf, out_k, out_v)


XP_OFF = 8
SSD_CHUNKS_PER_STEP = 4
SSD_SEQS_PER_STEP = 4
_HEAD_SPREAD = np.zeros((LANES, C_WIDTH), np.float32)
for _h in range(C_HEADS):
    _HEAD_SPREAD[MISC_DT + _h, _h * C_HEAD_DIM:(_h + 1) * C_HEAD_DIM] = 1.0


def _ssd_kernel(*refs, zero_init, nsteps, cps, spb):
    if zero_init:
        (z_ref, xbc_ref, misc_ref, cw_ref, cbias_ref, prm_ref, gn_ref, ex_ref, dskip_ref,
         bufh_ref, bufc_ref, o_ref, hout_ref, cnew_ref, st_scr, xp_scr, y_scr) = refs
    else:
        (z_ref, xbc_ref, misc_ref, h0_ref, cb0_ref, cw_ref, cbias_ref, prm_ref, gn_ref, ex_ref, dskip_ref,
         bufo_ref, bufh_ref, bufc_ref, o_ref, hout_ref, cnew_ref, st_scr, xp_scr, y_scr) = refs
        del bufo_ref
    del bufh_ref, bufc_ref
    c = pl.program_id(1)
    tail = CONV_W - 1
    rep = C_HEADS // C_GROUPS
    gw = C_WIDTH // C_GROUPS

    def start(s):
        if zero_init:
            st_scr[s * C_GROUPS:(s + 1) * C_GROUPS] = jnp.zeros((C_GROUPS, C_STATE, gw), F32)
            xp_scr[s, XP_OFF - tail:XP_OFF, :] = jnp.zeros((tail, CONV_DIM), F32)
        else:
            for h in range(C_HEADS):
                st_scr[s * C_GROUPS + h // rep, :, (h % rep) * C_HEAD_DIM:(h % rep + 1) * C_HEAD_DIM] = (
                    h0_ref[s, h].T)
            xp_scr[s, XP_OFF - tail:XP_OFF, :] = cb0_ref[s]

    def chunk(rows, s):
        xp_s, y_s, st_s = xp_scr.at[s], y_scr.at[s], st_scr.at[s * C_GROUPS:(s + 1) * C_GROUPS]
        xp_s[XP_OFF:XP_OFF + CHUNK, :] = xbc_ref[rows, :]
        y = cbias_ref[...] + xp_s[XP_OFF - tail:XP_OFF - tail + CHUNK, :] * cw_ref[0:1, :]
        for t in range(1, CONV_W):
            y = y + xp_s[XP_OFF - tail + t:XP_OFF - tail + t + CHUNK, :] * cw_ref[t:t + 1, :]
        new_tail = xp_s[XP_OFF + CHUNK - tail:XP_OFF + CHUNK, :]
        xp_s[XP_OFF - tail:XP_OFF, :] = new_tail
        cnew_ref[s] = new_tail
        xc = y * (1.0 / (1.0 + jnp.exp(-y)))
        xs = xc[:, 0:C_WIDTH]

        pre = misc_ref[rows, :] + prm_ref[0:1, :]
        dt = jnp.maximum(pre, 0.0) + jnp.log(1.0 + jnp.exp(-jnp.abs(pre)))
        ad = dt * (-jnp.exp(prm_ref[1:2, :]))
        li = lax.broadcasted_iota(I32, (CHUNK, CHUNK), 0)
        si = lax.broadcasted_iota(I32, (CHUNK, CHUNK), 1)
        causal = li >= si
        tri = causal.astype(BF16)
        acs = sum(jnp.dot(tri, part, preferred_element_type=F32) for part in _split3(ad))
        acs_t = acs.T
        ex = ex_ref[...]
        spread = lambda v: sum(jnp.dot(part, ex, preferred_element_type=F32) for part in _split3(v))
        dt_x = spread(dt)
        acs_x = spread(acs)
        acs_last_x = acs_x[CHUNK - 1:CHUNK, :]
        xd_all = xs * dt_x
        xd_b = xd_all.astype(BF16)
        xde_b = (xd_all * jnp.exp(acs_last_x - acs_x)).astype(BF16)
        grow_x = jnp.exp(acs_x)
        chunk_decay_x = jnp.exp(acs_last_x)
        head_of_lane = lax.broadcasted_iota(I32, (CHUNK, gw), 1) >> 6

        for g in range(C_GROUPS):
            gl = slice(g * gw, (g + 1) * gw)
            b_g = xc[:, C_WIDTH + g * C_STATE:C_WIDTH + (g + 1) * C_STATE]
            c_g = xc[:, C_WIDTH + (C_GROUPS + g) * C_STATE:C_WIDTH + (C_GROUPS + g + 1) * C_STATE]
            b_gt = b_g.T.astype(BF16)
            c_gb = c_g.astype(BF16)
            cb = jnp.dot(c_gb, b_gt, preferred_element_type=F32)
            st = st_s[g]
            y_g = jnp.dot(c_gb, st.astype(BF16), preferred_element_type=F32) * grow_x[:, gl]
            xd_g = xd_b[:, gl]
            for r in range(rep):
                col = MISC_DT + g * rep + r
                seg = acs[:, col:col + 1] - acs_t[col:col + 1, :]
                decay_in = jnp.where(causal, jnp.exp(jnp.where(causal, seg, 0.0)), 0.0)
                t_r = jnp.dot((cb * decay_in).astype(BF16), xd_g, preferred_element_type=F32)
                y_g = y_g + jnp.where(head_of_lane == r, t_r, 0.0)
            y_s[:, gl] = y_g + dskip_ref[:, gl] * xs[:, gl]
            st_s[g] = st * chunk_decay_x[:, gl] + jnp.dot(b_gt, xde_b[:, gl], preferred_element_type=F32)

        z = z_ref[rows, :]
        gate = y_s[...] * (z * (1.0 / (1.0 + jnp.exp(-z))))
        for g in range(C_GROUPS):
            gg = gate[:, g * gw:(g + 1) * gw]
            ms = jnp.mean(gg * gg, axis=-1, keepdims=True)
            o_ref[rows, g * gw:(g + 1) * gw] = gg * lax.rsqrt(ms + EPS) * gn_ref[:, g * gw:(g + 1) * gw]

    def finish(s):
        for h in range(C_HEADS):
            hout_ref[s, h] = st_scr[s * C_GROUPS + h // rep, :,
                                    (h % rep) * C_HEAD_DIM:(h % rep + 1) * C_HEAD_DIM].T

    if spb == 1:
        pl.when(c == 0)(functools.partial(start, 0))
        for i in range(cps):
            chunk(slice(i * CHUNK, (i + 1) * CHUNK), 0)
        pl.when(c == nsteps - 1)(functools.partial(finish, 0))
    else:
        assert nsteps == 1 and cps == 1
        for s in range(spb):
            start(s)
            chunk(slice(s * CHUNK, (s + 1) * CHUNK), s)
            finish(s)


def _ssd(proj, conv_w, conv_b, prm, gate_norm, out_h, out_c, layer, *, nb, nchunks, row0, state=None):
    cps = SSD_CHUNKS_PER_STEP if nchunks % SSD_CHUNKS_PER_STEP == 0 else 1
    nsteps = nchunks // cps
    spb = SSD_SEQS_PER_STEP if nchunks == 1 and nb % SSD_SEQS_PER_STEP == 0 else 1
    nblk = nb // spb
    rt = cps * spb * CHUNK
    assert row0 % rt == 0
    rb0 = row0 // rt
    row = lambda b, c: rb0 + b * nsteps + c
    const = lambda shape: pl.BlockSpec(shape, lambda b, c: tuple(0 for _ in shape))
    per_seq_h = pl.BlockSpec((spb, C_HEADS, C_HEAD_DIM, C_STATE), lambda b, c: (layer * nblk + b, 0, 0, 0))
    per_seq_c = pl.BlockSpec((spb, CONV_W - 1, CONV_DIM), lambda b, c: (layer * nblk + b, 0, 0))
    in_specs = [
        pl.BlockSpec((rt, C_WIDTH), lambda b, c: (row(b, c), COL_CZ // C_WIDTH)),
        pl.BlockSpec((rt, CONV_DIM), lambda b, c: (row(b, c), COL_CXBC // CONV_DIM)),
        pl.BlockSpec((rt, LANES), lambda b, c: (row(b, c), COL_MISC // LANES)),
    ]
    args = [proj, proj, proj]
    if state is not None:
        h0, cbuf0, buf = state
        in_specs += [per_seq_h, per_seq_c]
        args += [h0, cbuf0]
    in_specs += [const((CONV_W, CONV_DIM)), const((1, CONV_DIM)), const((8, LANES)), const((1, C_WIDTH)),
                 const((LANES, C_WIDTH)), const((1, C_WIDTH))]
    args += [conv_w, conv_b.reshape(1, CONV_DIM), prm[0], gate_norm.reshape(1, C_WIDTH),
             jnp.asarray(_HEAD_SPREAD, BF16), prm[1]]
    aliases = {}
    if state is not None:
        in_specs.append(ANY)
        args.append(buf)
        aliases[len(args) - 1] = 0
    in_specs += [ANY, ANY]
    args += [out_h, out_c]
    aliases[len(args) - 2] = 1
    aliases[len(args) - 1] = 2
    return pl.pallas_call(
        functools.partial(_ssd_kernel, zero_init=state is None, nsteps=nsteps, cps=cps, spb=spb),
        grid=(nblk, nsteps),
        in_specs=in_specs,
        out_specs=[pl.BlockSpec((rt, C_WIDTH), lambda b, c: (row(b, c), 0)), per_seq_h, per_seq_c],
        out_shape=[
            jax.ShapeDtypeStruct((proj.shape[0], C_WIDTH), F32),
            jax.ShapeDtypeStruct(out_h.shape, F32),
            jax.ShapeDtypeStruct(out_c.shape, F32),
        ],
        input_output_aliases=aliases,
        scratch_shapes=[pltpu.VMEM((spb * C_GROUPS, C_STATE, C_WIDTH // C_GROUPS), F32),
                        pltpu.VMEM((spb, XP_OFF + CHUNK, CONV_DIM), F32),
                        pltpu.VMEM((spb, CHUNK, C_WIDTH), F32)],
        compiler_params=_cparams("parallel", "arbitrary"),
        name="ssd_prompt" if state is None else "ssd_sample",
    )(*args)


def _outproj_kernel(x_ref, oa_ref, ob_ref, oc_ref, w_ref, o_ref):
    acc = x_ref[...]
    acc = acc + jnp.dot(oa_ref[...].astype(BF16), w_ref[0, 0:A_WIDTH, :], preferred_element_type=F32)
    acc = acc + jnp.dot(ob_ref[...].astype(BF16), w_ref[0, A_WIDTH:A_WIDTH + B_WIDTH, :],
                        preferred_element_type=F32)
    acc = acc + jnp.dot(oc_ref[...].astype(BF16), w_ref[0, A_WIDTH + B_WIDTH:, :],
                        preferred_element_type=F32)
    o_ref[...] = acc


def _out_proj(x, oa, ob, oc, w_all, layer):
    t, d = x.shape
    tm = _row_tile(t, 512)
    rows = lambda w: pl.BlockSpec((tm, w), lambda i: (i, 0))
    return pl.pallas_call(
        _outproj_kernel,
        grid=(t // tm,),
        in_specs=[rows(d), rows(A_WIDTH), rows(B_WIDTH), rows(C_WIDTH),
                  pl.BlockSpec((1, d, d), lambda i: (layer, 0, 0))],
        out_specs=rows(d),
        out_shape=jax.ShapeDtypeStruct((t, d), F32),
        compiler_params=_cparams("parallel"),
        name="out_proj",
    )(x, oa, ob, oc, w_all)


def _ffn_kernel(x_ref, nw_ref, wu_ref, wd_ref, o_ref, h_scr):
    def step(first):
        if first:
            x = x_ref[...]
            ms = jnp.mean(x * x, axis=-1, keepdims=True)
            h_scr[...] = (x * lax.rsqrt(ms + EPS) * nw_ref[...]).astype(BF16)
        u = jnp.maximum(jnp.dot(h_scr[...], wu_ref[0], preferred_element_type=F32), 0.0)
        down = jnp.dot((u * u).astype(BF16), wd_ref[0], preferred_element_type=F32)
        if first:
            o_ref[...] = x_ref[...] + down
        else:
            o_ref[...] += down

    f = pl.program_id(1)
    pl.when(f == 0)(functools.partial(step, True))
    pl.when(f != 0)(functools.partial(step, False))


def _ffn(x, norm_w, wu_all, wd_all, layer):
    t, d = x.shape
    ff = wu_all.shape[2]
    tm = _row_tile(t, 1024)
    tf = 512
    return pl.pallas_call(
        _ffn_kernel,
        grid=(t // tm, ff // tf),
        in_specs=[
            pl.BlockSpec((tm, d), lambda i, f: (i, 0)),
            pl.BlockSpec((1, d), lambda i, f: (0, 0)),
            pl.BlockSpec((1, d, tf), lambda i, f: (layer, 0, f)),
            pl.BlockSpec((1, tf, d), lambda i, f: (layer, f, 0)),
        ],
        out_specs=pl.BlockSpec((tm, d), lambda i, f: (i, 0)),
        out_shape=jax.ShapeDtypeStruct((t, d), F32),
        scratch_shapes=[pltpu.VMEM((tm, d), BF16)],
        compiler_params=_cparams("parallel", "arbitrary"),
        name="ffn",
    )(x, norm_w.reshape(1, d), wu_all, wd_all)


def _norm_kernel(x_ref, nw_ref, o_ref):
    x = x_ref[...]
    ms = jnp.mean(x * x, axis=-1, keepdims=True)
    o_ref[...] = x * lax.rsqrt(ms + EPS) * nw_ref[...]


def _final_norm(x, norm_w, row0, nrows):
    d = x.shape[1]
    tm = _row_tile(nrows, 512)
    assert row0 % tm == 0
    return pl.pallas_call(
        _norm_kernel,
        grid=(nrows // tm,),
        in_specs=[pl.BlockSpec((tm, d), lambda i: (row0 // tm + i, 0)),
                  pl.BlockSpec((1, d), lambda i: (0, 0))],
        out_specs=pl.BlockSpec((tm, d), lambda i: (i, 0)),
        out_shape=jax.ShapeDtypeStruct((nrows, d), F32),
        compiler_params=_cparams("parallel"),
        name="final_norm",
    )(x, norm_w.reshape(1, d))


def _lane_row(v, off):
    return jnp.zeros((LANES,), F32).at[off:off + v.shape[0]].set(v)


def _trunk(x_prompt, x_sample, cache_a_k, cache_a_v, cache_a_kidx, cache_b_k, cache_b_v,
           state_ssm, state_conv, norm1, w_in, w_out, b_rel, conv_w, conv_b, dt_bias,
           a_log, d_skip, gate_norm, norm2, w_up, w_down, final_norm):
    bp, tp, d = x_prompt.shape
    bs, ts, _ = x_sample.shape
    depth = w_in.shape[0]
    past = cache_a_k.shape[2]
    nbuf = cache_b_k.shape[2]
    n_p = bp * tp
    n_s = bs * ts
    assert ts == CHUNK and tp % DSA_KB_PROMPT == 0 and tp % BAND_QS_PROMPT == 0 and nbuf == B_WIN
    qs_p = LANES
    nkeep = min(B_WIN, tp)

    x = jnp.concatenate([x_prompt.reshape(n_p, d), x_sample.reshape(n_s, d)], axis=0)

    tm_in = _row_tile(math.gcd(tp, n_p + n_s), 1024)
    assert tm_in % ts == 0
    pos = jnp.concatenate([jnp.arange(tp), jnp.tile(past + jnp.arange(ts), tm_in // ts)])
    tab_a = _rope_tables(pos, A_HEAD_DIM)
    tab_i = _rope_tables(pos, IDX_DIM)
    w_out_b, w_up_b, w_down_b = w_out.astype(BF16), w_up.astype(BF16), w_down.astype(BF16)
    bias_p = _band_bias(b_rel, BAND_QS_PROMPT)
    bias_s = _band_bias(b_rel, ts)
    w_in_r = _regroup_w_in(w_in)
    cak = cache_a_k.reshape(depth * bs * past * A_HEADS, A_HEAD_DIM)
    cav = cache_a_v.reshape(depth * bs * past * A_HEADS, A_HEAD_DIM)
    caki = cache_a_kidx.reshape(depth * bs * past, IDX_DIM)
    cbk = cache_b_k.reshape(depth * bs * nbuf * B_HEADS, B_HEAD_DIM)
    cbv = cache_b_v.reshape(depth * bs * nbuf * B_HEADS, B_HEAD_DIM)
    h0 = state_ssm.reshape(depth * bs, C_HEADS, C_HEAD_DIM, C_STATE)
    cbuf0 = state_conv.reshape(depth * bs, CONV_W - 1, CONV_DIM)

    p_ak = _uninit((depth * bp * tp * A_HEADS, A_HEAD_DIM))
    p_av = _uninit((depth * bp * tp * A_HEADS, A_HEAD_DIM))
    p_aki = _uninit((depth * bp * tp, IDX_DIM))
    p_bk = _uninit((depth * bp * nkeep * B_HEADS, B_HEAD_DIM))
    p_bv = _uninit((depth * bp * nkeep * B_HEADS, B_HEAD_DIM))
    p_h = _uninit((depth * bp, C_HEADS, C_HEAD_DIM, C_STATE))
    p_c = _uninit((depth * bp, CONV_W - 1, CONV_DIM))
    s_ak = _uninit((depth * bs * ts * A_HEADS, A_HEAD_DIM))
    s_av = _uninit((depth * bs * ts * A_HEADS, A_HEAD_DIM))
    s_aki = _uninit((depth * bs * ts, IDX_DIM))
    s_bk = _uninit((depth * bs * ts * B_HEADS, B_HEAD_DIM))
    s_bv = _uninit((depth * bs * ts * B_HEADS, B_HEAD_DIM))
    s_h = _uninit((depth * bs, C_HEADS, C_HEAD_DIM, C_STATE))
    s_c = _uninit((depth * bs, CONV_W - 1, CONV_DIM))

    for l in range(depth):
        proj = _in_proj(x, norm1[l], w_in_r, l, tab_a, tab_i, tm=tm_in, n_prompt=n_p, t_prompt=tp)

        oa, p_ak, p_av, p_aki = _dsa_prompt(proj, p_ak, p_av, p_aki, l, nb=bp, tq=tp, qs=qs_p,
                                            kb=DSA_KB_PROMPT)
        oa, s_ak, s_av, s_aki = _dsa_sample(proj, cak, cav, caki, oa, s_ak, s_av, s_aki, l,
                                            nb=bs, ts=ts, past=past, row0=n_p, kb=DSA_KB_SAMPLE)
        ob, p_bk, p_bv = _band_prompt(proj, bias_p, p_bk, p_bv, l, nb=bp, tq=tp, qs=BAND_QS_PROMPT)
        ob, s_bk, s_bv = _band_sample(proj, cbk, cbv, bias_s, ob, s_bk, s_bv, l, nb=bs, ts=ts, row0=n_p)
        rows = jnp.zeros((8, LANES), F32)
        rows = rows.at[0].set(_lane_row(dt_bias[l], MISC_DT)).at[1].set(_lane_row(a_log[l], MISC_DT))
        prm = (rows, jnp.repeat(d_skip[l], C_HEAD_DIM).reshape(1, C_WIDTH))
        oc, p_h, p_c = _ssd(proj, conv_w[l], conv_b[l], prm, gate_norm[l], p_h, p_c, l,
                            nb=bp, nchunks=tp // CHUNK, row0=0)
        oc, s_h, s_c = _ssd(proj, conv_w[l], conv_b[l], prm, gate_norm[l], s_h, s_c, l,
                            nb=bs, nchunks=1, row0=n_p, state=(h0, cbuf0, oc))

        x = _out_proj(x, oa, ob, oc, w_out_b, l)
        x = _ffn(x, norm2[l], w_up_b, w_down_b, l)

    y_prompt = _final_norm(x, final_norm, 0, n_p).reshape(bp, tp, d)
    y_sample = _final_norm(x, final_norm, n_p, n_s).reshape(bs, ts, d)
    return (y_prompt, y_sample,
            p_ak.reshape(depth, bp, tp, A_HEADS, A_HEAD_DIM),
            p_av.reshape(depth, bp, tp, A_HEADS, A_HEAD_DIM),
            p_aki.reshape(depth, bp, tp, IDX_DIM),
            p_bk.reshape(depth, bp, nkeep, B_HEADS, B_HEAD_DIM),
            p_bv.reshape(depth, bp, nkeep, B_HEADS, B_HEAD_DIM),
            p_h.reshape(depth, bp, C_HEADS, C_HEAD_DIM, C_STATE),
            p_c.reshape(depth, bp, CONV_W - 1, CONV_DIM),
            s_ak.reshape(depth, bs, ts, A_HEADS, A_HEAD_DIM),
            s_av.reshape(depth, bs, ts, A_HEADS, A_HEAD_DIM),
            s_aki.reshape(depth, bs, ts, IDX_DIM),
            s_bk.reshape(depth, bs, ts, B_HEADS, B_HEAD_DIM),
            s_bv.reshape(depth, bs, ts, B_HEADS, B_HEAD_DIM),
            s_h.reshape(depth, bs, C_HEADS, C_HEAD_DIM, C_STATE),
            s_c.reshape(depth, bs, CONV_W - 1, CONV_DIM))


def kernel(x_prompt, x_sample, cache_a_k, cache_a_v, cache_a_kidx, cache_b_k, cache_b_v, state_ssm, state_conv, norm1, w_in, w_out, b_rel, conv_w, conv_b, dt_bias, a_log, d_skip, gate_norm, norm2, w_up, w_down, final_norm):
    return _trunk(x_prompt, x_sample, cache_a_k, cache_a_v, cache_a_kidx, cache_b_k, cache_b_v,
                  state_ssm, state_conv, norm1, w_in, w_out, b_rel, conv_w, conv_b, dt_bias,
                  a_log, d_skip, gate_norm, norm2, w_up, w_down, final_norm)
```

```python
import functools
import math

import jax
import jax.numpy as jnp
import numpy as np
from jax import lax
from jax.experimental import pallas as pl
from jax.experimental.pallas import tpu as pltpu

F32 = jnp.float32
BF16 = jnp.bfloat16
I32 = jnp.int32

D_MODEL = 2048
CHUNK = 64
A_HEADS = 4
A_HEAD_DIM = 128
A_WIDTH = A_HEADS * A_HEAD_DIM
IDX_HEADS = 16
IDX_DIM = 64
IQ_WIDTH = IDX_HEADS * IDX_DIM
DSA_TOPK = 256
B_HEADS = 4
B_HEAD_DIM = 128
B_WIDTH = B_HEADS * B_HEAD_DIM
B_PREV_CHUNKS = 8
B_WIN = B_PREV_CHUNKS * CHUNK
REL_CLIP = 128
C_WIDTH = 1024
C_HEAD_DIM = 64
C_HEADS = C_WIDTH // C_HEAD_DIM
C_GROUPS = 4
C_STATE = 128
CONV_W = 4
CONV_DIM = C_WIDTH + 2 * C_GROUPS * C_STATE
D_FF = 4 * D_MODEL
ROPE_THETA = 500000.0
EPS = 1e-5

LANES = 128
MXU_N = 256
INT_MIN = -(2 ** 31)
NEG_INF = float("-inf")

COL_AQ = 0
COL_AK = 512
COL_IQ = 1024
COL_AV = 2048
COL_BQ = 2560
COL_BK = 3072
COL_BV = 3584
COL_CXBC = 4096
COL_CZ = 6144
COL_MISC = 7168
MISC_IW = 64
MISC_DT = 80
IN_COLS_PAD = 7680
IN_TN = 1280
SRC_AQ, SRC_AV, SRC_IQ, SRC_IK, SRC_BQ, SRC_CZ, SRC_CXBC, SRC_DT, SRC_END = (
    0, 1024, 1536, 2560, 2640, 4176, 5200, 7248, 7264)

VMEM_LIMIT = 56 * 1024 * 1024
ANY = pl.BlockSpec(memory_space=pl.ANY)


def _cparams(*sem):
    return pltpu.CompilerParams(dimension_semantics=sem, vmem_limit_bytes=VMEM_LIMIT)


def _row_tile(t, cap):
    tm = cap
    while t % tm:
        tm //= 2
    return tm


def _nt_dot(a, b):
    return lax.dot_general(a, b, (((1,), (1,)), ((), ())), preferred_element_type=F32)


def _split3(x):
    hi = x.astype(BF16)
    r1 = x - hi.astype(F32)
    mid = r1.astype(BF16)
    lo = (r1 - mid.astype(F32)).astype(BF16)
    return hi, mid, lo


def _head_rows(ref, h, n, heads):
    return ref.at[pl.ds(h, n, stride=heads), :]


def _uninit_kernel(o_ref):
    del o_ref


def _uninit(shape, dtype=F32):
    return pl.pallas_call(_uninit_kernel, out_specs=ANY, out_shape=jax.ShapeDtypeStruct(shape, dtype),
                          name="alloc")()


def _regroup_kernel(w_ref, o_ref):
    def put(dst, src, width):
        o_ref[0, :, dst:dst + width] = w_ref[0, :, src:src + width].astype(BF16)

    rows = w_ref.shape[1]
    put(COL_AQ, SRC_AQ, 2 * A_WIDTH)
    put(COL_IQ, SRC_IQ, IQ_WIDTH)
    put(COL_AV, SRC_AV, A_WIDTH)
    put(COL_BQ, SRC_BQ, 3 * B_WIDTH)
    put(COL_CXBC, SRC_CXBC, CONV_DIM)
    put(COL_CZ, SRC_CZ, C_WIDTH)
    o_ref[0, :, COL_MISC:] = jnp.zeros((rows, IN_COLS_PAD - COL_MISC), BF16)
    put(COL_MISC, SRC_IK, IDX_DIM + IDX_HEADS)
    put(COL_MISC + MISC_DT, SRC_DT, C_HEADS)


def _regroup_w_in(w_in):
    depth, d, n = w_in.shape
    assert n == SRC_END
    tk = 256
    return pl.pallas_call(
        _regroup_kernel,
        grid=(depth, d // tk),
        in_specs=[pl.BlockSpec((1, tk, n), lambda l, i: (l, i, 0))],
        out_specs=pl.BlockSpec((1, tk, IN_COLS_PAD), lambda l, i: (l, i, 0)),
        out_shape=jax.ShapeDtypeStruct((depth, d, IN_COLS_PAD), BF16),
        compiler_params=_cparams("parallel", "parallel"),
        name="regroup_w_in",
    )(w_in)


def _rope_block(x, c, sm, sp, half):
    return x * c + pltpu.roll(x, LANES - half, 1) * sm + pltpu.roll(x, half, 1) * sp


def _inproj_kernel(x_ref, nw_ref, w_ref, ta_ref, ti_ref, o_ref, h_scr):
    j = pl.program_id(1)

    def project(tile):
        if tile == 0:
            x = x_ref[...]
            ms = jnp.mean(x * x, axis=-1, keepdims=True)
            h_scr[...] = (x * lax.rsqrt(ms + EPS) * nw_ref[...]).astype(BF16)
        for pair in range(IN_TN // MXU_N):
            cols = slice(pair * MXU_N, (pair + 1) * MXU_N)
            both = jnp.dot(h_scr[...], w_ref[0, :, cols], preferred_element_type=F32)
            for half in range(MXU_N // LANES):
                blk = pair * (MXU_N // LANES) + half
                sl = slice(blk * LANES, (blk + 1) * LANES)
                acc = both[:, half * LANES:(half + 1) * LANES]
                col = -1 if tile is None else tile * IN_TN + blk * LANES
                if 0 <= col < COL_IQ:
                    acc = _rope_block(acc, ta_ref[0], ta_ref[1], ta_ref[2], A_HEAD_DIM // 8)
                elif COL_IQ <= col < COL_AV:
                    acc = _rope_block(acc, ti_ref[0], ti_ref[1], ti_ref[2], IDX_DIM // 8)
                elif col == COL_MISC:
                    lane = lax.broadcasted_iota(I32, acc.shape, 1)
                    rot = _rope_block(acc, ti_ref[0], ti_ref[1], ti_ref[2], IDX_DIM // 8)
                    acc = jnp.where(lane < IDX_DIM, rot, acc)
                o_ref[:, sl] = acc

    rope_tiles = [t for t in range(IN_COLS_PAD // IN_TN)
                  if t * IN_TN < COL_AV or t * IN_TN <= COL_MISC < (t + 1) * IN_TN]
    plain = j >= 0
    for tile in rope_tiles:
        pl.when(j == tile)(functools.partial(project, tile))
        plain = plain & (j != tile)
    pl.when(plain)(functools.partial(project, None))


def _in_proj(x, norm_w, w_all, layer, tab_a, tab_i, *, tm, n_prompt, t_prompt):
    t, d = x.shape
    n = w_all.shape[2]
    per_seq = t_prompt // tm
    tab = pl.BlockSpec((3, tm, LANES),
                       lambda i, j: (0, jnp.where(i < n_prompt // tm, i % per_seq, per_seq), 0))
    return pl.pallas_call(
        _inproj_kernel,
        grid=(t // tm, n // IN_TN),
        in_specs=[
            pl.BlockSpec((tm, d), lambda i, j: (i, 0)),
            pl.BlockSpec((1, d), lambda i, j: (0, 0)),
            pl.BlockSpec((1, d, IN_TN), lambda i, j: (layer, 0, j)),
            tab, tab,
        ],
        out_specs=pl.BlockSpec((tm, IN_TN), lambda i, j: (i, j)),
        out_shape=jax.ShapeDtypeStruct((t, n), F32),
        scratch_shapes=[pltpu.VMEM((tm, d), BF16)],
        compiler_params=_cparams("parallel", "arbitrary"),
        name="in_proj",
    )(x, norm_w.reshape(1, d), w_all, tab_a, tab_i)


def _rope_tables(pos, head_dim):
    rot = head_dim // 4
    half = rot // 2
    inv = ROPE_THETA ** (-jnp.arange(half, dtype=F32) * 2.0 / rot)
    ang = pos.astype(F32)[:, None] * inv[None, :]
    cos, sin = jnp.cos(ang), jnp.sin(ang)
    n = pos.shape[0]
    rest = head_dim - rot
    c = jnp.concatenate([cos, cos, jnp.ones((n, rest), F32)], axis=1)
    sm = jnp.concatenate([-sin, jnp.zeros((n, half + rest), F32)], axis=1)
    sp = jnp.concatenate([jnp.zeros((n, half), F32), sin, jnp.zeros((n, rest), F32)], axis=1)
    tab = jnp.stack([c, sm, sp], axis=0)
    return jnp.tile(tab, (1, 1, LANES // head_dim))


TIE_BLOCK = 64
SUB = 128
DSA_KB_PROMPT = 256
DSA_KB_SAMPLE = 384
ATT_BLOCK = 256


def _tree_sum(parts):
    while len(parts) > 1:
        parts = [parts[i] + parts[i + 1] for i in range(0, len(parts) - 1, 2)] + (
            parts[-1:] if len(parts) % 2 else [])
    return parts[0]


def _count_keys(key_scr, n_keys, qs, pred):
    blocks = []
    for u in range(n_keys // SUB):
        hit = jnp.where(pred(key_scr[u * SUB:(u + 1) * SUB, :]), 1, 0).astype(I32)
        blocks.append(_tree_sum([hit[i * 8:(i + 1) * 8, :] for i in range(SUB // 8)]))
    return jnp.sum(_tree_sum(blocks), axis=0, keepdims=True)


def _kth_largest(key_scr, n_keys, qs, k):
    count = functools.partial(_count_keys, key_scr, n_keys, qs)
    ans0 = jnp.where(count(lambda x: x >= 0) >= k, 0, INT_MIN).astype(I32)

    def bit_step(i, ans):
        cand = ans | (jnp.int32(1) << (30 - i))
        return jnp.where(count(lambda x: x >= cand) >= k, cand, ans)

    return lax.fori_loop(0, 31, bit_step, ans0)


def _dsa_select_attend(q_ref, qi_ref, mq_ref, o_ref, kb_scr, vt_scr, kib_scr, w_scr, score_scr,
                       key_scr, mask_scr, *, n_keys, qs, topk, limit, nseq=1):
    w_scr[...] = mq_ref[...].T * (IDX_HEADS ** -0.5 * IDX_DIM ** -0.5)
    in_first = lax.broadcasted_iota(I32, (qs, 1), 0) < qs // nseq
    first_lanes = lax.broadcasted_iota(I32, (1, qs), 1) < qs // nseq

    def per_seq(qh):
        if nseq == 1:
            return qh.astype(BF16)
        return jnp.concatenate([jnp.where(in_first, qh, 0.0), jnp.where(in_first, 0.0, qh)],
                               axis=1).astype(BF16)

    kib = kib_scr[0:n_keys, :]
    for h in range(IDX_HEADS):
        qh = per_seq(qi_ref[:, h * IDX_DIM:(h + 1) * IDX_DIM])
        term = jnp.maximum(_nt_dot(kib, qh), 0.0) * w_scr[MISC_IW + h:MISC_IW + h + 1, :]
        if h == 0:
            score_scr[0:n_keys, :] = term
        else:
            score_scr[0:n_keys, :] += term
    sc = score_scr[0:n_keys, :]
    sc = jnp.where(sc == 0.0, 0.0, sc)
    bits = lax.bitcast_convert_type(sc, I32)
    key = bits ^ ((bits >> 31) & 0x7FFFFFFF)
    s_idx = lax.broadcasted_iota(I32, (n_keys, qs), 0)
    key_scr[0:n_keys, :] = jnp.where(s_idx < limit, key, INT_MIN)

    thr = _kth_largest(key_scr, n_keys, qs, topk)
    cnt_gt = _count_keys(key_scr, n_keys, qs, lambda x: x > thr)
    cnt_eq = _count_keys(key_scr, n_keys, qs, lambda x: x == thr)
    room = topk - cnt_gt
    live = thr > INT_MIN
    all_ties = (cnt_eq <= room) & live
    key = key_scr[0:n_keys, :]
    mask_scr[0:n_keys, :] = jnp.where((key > thr) | ((key == thr) & all_ties), 0.0, NEG_INF)
    need_ties = jnp.max(((cnt_eq > room) & live).astype(I32)) > 0

    @pl.when(need_ties)
    def _():
        tri = (lax.broadcasted_iota(I32, (TIE_BLOCK, TIE_BLOCK), 0)
               >= lax.broadcasted_iota(I32, (TIE_BLOCK, TIE_BLOCK), 1)).astype(BF16)

        def blk(b, carry):
            rows = pl.ds(pl.multiple_of(b * TIE_BLOCK, TIE_BLOCK), TIE_BLOCK)
            kblk = key_scr[rows, :]
            eqb = (kblk == thr) & live
            eqf = eqb.astype(F32)
            prefix = jnp.dot(tri, eqf.astype(BF16), preferred_element_type=F32) + carry
            keep = (kblk > thr) | (eqb & (prefix <= room.astype(F32)))
            mask_scr[rows, :] = jnp.where(keep, 0.0, NEG_INF)
            return carry + jnp.sum(eqf, axis=0, keepdims=True)

        lax.fori_loop(0, n_keys // TIE_BLOCK, blk, jnp.zeros((1, qs), F32))

    ab = ATT_BLOCK if n_keys % ATT_BLOCK == 0 else SUB
    kw = nseq * A_HEAD_DIM
    for h in range(A_HEADS):
        sl = slice(h * A_HEAD_DIM, (h + 1) * A_HEAD_DIM)
        ksl = slice(h * kw, (h + 1) * kw)
        qh = per_seq(q_ref[:, sl])
        parts = []
        for b in range(n_keys // ab):
            rows = slice(b * ab, (b + 1) * ab)
            lg = _nt_dot(kb_scr[rows, ksl], qh) * (A_HEAD_DIM ** -0.5) + mask_scr[rows, :]
            m_b = jnp.max(lg, axis=0, keepdims=True)
            e = jnp.exp(lg - jnp.where(m_b == NEG_INF, 0.0, m_b))
            den_b = jnp.sum(e, axis=0, keepdims=True)
            o_b = jnp.dot(vt_scr[ksl, rows], e.astype(BF16), preferred_element_type=F32)
            if nseq == 2:
                o_b = jnp.where(first_lanes, o_b[0:A_HEAD_DIM], o_b[A_HEAD_DIM:])
            parts.append((m_b, den_b, o_b))
        m = functools.reduce(jnp.maximum, [p[0] for p in parts])
        den = jnp.zeros((1, qs), F32)
        o_t = jnp.zeros((A_HEAD_DIM, qs), F32)
        for m_b, den_b, o_b in parts:
            wgt = jnp.exp(m_b - m)
            den = den + wgt * den_b
            o_t = o_t + wgt * o_b
        o_ref[:, sl] = (o_t / den).T


def _dsa_prompt_kernel(q_ref, qi_ref, mq_ref, k_ref, v_ref, mk_ref, bufk_ref, bufv_ref, bufi_ref,
                       o_ref, ck_ref, cv_ref, ci_ref,
                       kb_scr, vt_scr, kib_scr, w_scr, score_scr, key_scr, mask_scr,
                       *, s_keys, qs, kb, topk):
    del bufk_ref, bufv_ref, bufi_ref
    j = pl.program_id(1)

    @pl.when(j == 0)
    def _():
        kb_scr[...] = k_ref[...].astype(BF16)
        kib_scr[...] = mk_ref[:, 0:IDX_DIM].astype(BF16)
        for i in range(s_keys // kb):
            vt_scr[:, i * kb:(i + 1) * kb] = v_ref[i * kb:(i + 1) * kb, :].T.astype(BF16)
        ci_ref[...] = mk_ref[:, 0:IDX_DIM]
        for h in range(A_HEADS):
            sl = slice(h * A_HEAD_DIM, (h + 1) * A_HEAD_DIM)
            _head_rows(ck_ref, h, s_keys, A_HEADS)[...] = k_ref[:, sl]
            _head_rows(cv_ref, h, s_keys, A_HEADS)[...] = v_ref[:, sl]

    q_pos = j * qs + lax.broadcasted_iota(I32, (1, qs), 1)
    limit = ((q_pos >> 6) + 1) * CHUNK
    nsb = (j * qs + qs + kb - 1) // kb
    for n in range(1, s_keys // kb + 1):
        pl.when(nsb == n)(functools.partial(
            _dsa_select_attend, q_ref, qi_ref, mq_ref, o_ref, kb_scr, vt_scr, kib_scr, w_scr, score_scr,
            key_scr, mask_scr, n_keys=n * kb, qs=qs, topk=topk, limit=limit))


def _dsa_sample_kernel(q_ref, qi_ref, mq_ref, kc_ref, vc_ref, kic_ref, kn_ref, vn_ref, mkn_ref,
                       bufo_ref, bufk_ref, bufv_ref, bufi_ref,
                       o_ref, ck_ref, cv_ref, ci_ref,
                       kb_scr, vt_scr, kib_scr, w_scr, score_scr, key_scr, mask_scr,
                       vrow_scr, *, s_keys, qs, kb, topk, past, nseq):
    del bufo_ref, bufk_ref, bufv_ref, bufi_ref
    n_new = qs // nseq
    live = past + n_new
    for s in range(nseq):
        new = slice(s * n_new, (s + 1) * n_new)
        for h in range(A_HEADS):
            sl = slice(h * A_HEAD_DIM, (h + 1) * A_HEAD_DIM)
            dst = slice((h * nseq + s) * A_HEAD_DIM, (h * nseq + s + 1) * A_HEAD_DIM)
            old_rows = pl.ds(s * past * A_HEADS + h, past, stride=A_HEADS)
            new_rows = pl.ds(s * n_new * A_HEADS + h, n_new, stride=A_HEADS)
            kb_scr[0:past, dst] = kc_ref[old_rows, :].astype(BF16)
            kb_scr[past:live, dst] = kn_ref[new, sl].astype(BF16)
            vrow_scr[0:past, dst] = vc_ref[old_rows, :]
            vrow_scr[past:live, dst] = vn_ref[new, sl]
            ck_ref[new_rows, :] = kn_ref[new, sl]
            cv_ref[new_rows, :] = vn_ref[new, sl]
        isl = slice(s * IDX_DIM, (s + 1) * IDX_DIM)
        kib_scr[0:past, isl] = kic_ref[s * past:(s + 1) * past, :].astype(BF16)
        kib_scr[past:live, isl] = mkn_ref[new, 0:IDX_DIM].astype(BF16)
    ci_ref[...] = mkn_ref[:, 0:IDX_DIM]
    if s_keys > live:
        kb_scr[live:, :] = jnp.zeros((s_keys - live, nseq * A_WIDTH), BF16)
        kib_scr[live:, :] = jnp.zeros((s_keys - live, nseq * IDX_DIM), BF16)
        vrow_scr[live:, :] = jnp.zeros((s_keys - live, nseq * A_WIDTH), F32)
    for i in range(s_keys // kb):
        vt_scr[:, i * kb:(i + 1) * kb] = vrow_scr[i * kb:(i + 1) * kb, :].T.astype(BF16)

    limit = jnp.full((1, qs), live, I32)
    _dsa_select_attend(q_ref, qi_ref, mq_ref, o_ref, kb_scr, vt_scr, kib_scr, w_scr, score_scr,
                       key_scr, mask_scr, n_keys=s_keys, qs=qs, topk=topk, limit=limit, nseq=nseq)


def _dsa_scratch(s_keys, qs, nseq=1):
    return [
        pltpu.VMEM((s_keys, nseq * A_WIDTH), BF16),
        pltpu.VMEM((nseq * A_WIDTH, s_keys), BF16),
        pltpu.VMEM((s_keys, nseq * IDX_DIM), BF16),
        pltpu.VMEM((LANES, qs), F32),
        pltpu.VMEM((s_keys, qs), F32),
        pltpu.VMEM((s_keys, qs), I32),
        pltpu.VMEM((s_keys, qs), F32),
    ]


def _dsa_prompt(proj, out_k, out_v, out_ki, layer, *, nb, tq, qs, kb):
    nsteps = tq // qs
    qcol = lambda w, off: pl.BlockSpec((qs, w), lambda b, j: (b * nsteps + j, off // w))
    kcol = lambda w, off: pl.BlockSpec((tq, w), lambda b, j: (b, off // w))
    cache = lambda rows, w: pl.BlockSpec((rows, w), lambda b, j: (layer * nb + b, 0))
    kern = functools.partial(_dsa_prompt_kernel, s_keys=tq, qs=qs, kb=kb, topk=min(DSA_TOPK, tq // 4))
    return pl.pallas_call(
        kern,
        grid=(nb, nsteps),
        in_specs=[qcol(A_WIDTH, COL_AQ), qcol(IQ_WIDTH, COL_IQ), qcol(LANES, COL_MISC),
                  kcol(A_WIDTH, COL_AK), kcol(A_WIDTH, COL_AV), kcol(LANES, COL_MISC), ANY, ANY, ANY],
        out_specs=[pl.BlockSpec((qs, A_WIDTH), lambda b, j: (b * nsteps + j, 0)),
                   cache(tq * A_HEADS, A_HEAD_DIM), cache(tq * A_HEADS, A_HEAD_DIM), cache(tq, IDX_DIM)],
        out_shape=[jax.ShapeDtypeStruct((proj.shape[0], A_WIDTH), F32),
                   jax.ShapeDtypeStruct(out_k.shape, F32), jax.ShapeDtypeStruct(out_v.shape, F32),
                   jax.ShapeDtypeStruct(out_ki.shape, F32)],
        input_output_aliases={6: 1, 7: 2, 8: 3},
        scratch_shapes=_dsa_scratch(tq, qs),
        compiler_params=_cparams("parallel", "arbitrary"),
        name="dsa_prompt",
    )(proj, proj, proj, proj, proj, proj, out_k, out_v, out_ki)


def _dsa_sample(proj, cache_k, cache_v, cache_ki, buf, out_k, out_v, out_ki, layer,
                *, nb, ts, past, row0, kb):
    nseq = 2 if nb % 2 == 0 and (2 * ts) % LANES == 0 else 1
    qs = nseq * ts
    ngrp = nb // nseq
    assert row0 % qs == 0
    rb0 = row0 // qs
    live = past + ts
    s_keys = -(-live // kb) * kb
    qcol = lambda w, off: pl.BlockSpec((qs, w), lambda b, j: (rb0 + b, off // w))
    per_seq = lambda rows, w: pl.BlockSpec((nseq * rows, w), lambda b, j: (layer * ngrp + b, 0))
    kern = functools.partial(_dsa_sample_kernel, s_keys=s_keys, qs=qs, kb=kb,
                             topk=min(DSA_TOPK, live // 4), past=past, nseq=nseq)
    return pl.pallas_call(
        kern,
        grid=(ngrp, 1),
        in_specs=[qcol(A_WIDTH, COL_AQ), qcol(IQ_WIDTH, COL_IQ), qcol(LANES, COL_MISC),
                  per_seq(past * A_HEADS, A_HEAD_DIM), per_seq(past * A_HEADS, A_HEAD_DIM),
                  per_seq(past, IDX_DIM),
                  qcol(A_WIDTH, COL_AK), qcol(A_WIDTH, COL_AV), qcol(LANES, COL_MISC),
                  ANY, ANY, ANY, ANY],
        out_specs=[pl.BlockSpec((qs, A_WIDTH), lambda b, j: (rb0 + b, 0)),
                   per_seq(ts * A_HEADS, A_HEAD_DIM), per_seq(ts * A_HEADS, A_HEAD_DIM),
                   per_seq(ts, IDX_DIM)],
        out_shape=[jax.ShapeDtypeStruct(buf.shape, F32), jax.ShapeDtypeStruct(out_k.shape, F32),
                   jax.ShapeDtypeStruct(out_v.shape, F32), jax.ShapeDtypeStruct(out_ki.shape, F32)],
        input_output_aliases={9: 0, 10: 1, 11: 2, 12: 3},
        scratch_shapes=_dsa_scratch(s_keys, qs, nseq) + [pltpu.VMEM((s_keys, nseq * A_WIDTH), F32)],
        compiler_params=_cparams("parallel", "arbitrary"),
        name="dsa_sample",
    )(proj, proj, proj, cache_k, cache_v, cache_ki, proj, proj, proj, buf, out_k, out_v, out_ki)


BAND_QS_PROMPT = 256
BAND_SEQS_PER_STEP = 4


def _band_bias_kernel(base_ref, o_ref, *, qs, w, wp):
    q = lax.broadcasted_iota(I32, (qs, w), 0)
    jx = lax.broadcasted_iota(I32, (qs, w), 1)
    back = (q >> 6) + B_PREV_CHUNKS - (jx >> 6)
    allowed = (back >= 0) & (back <= B_PREV_CHUNKS)
    for h in range(B_HEADS):
        rows = jnp.broadcast_to(base_ref[0, h:h + 1, :], (qs, wp))
        toeplitz = pltpu.roll(rows, 0, 1, stride=1, stride_axis=0)
        o_ref[0, h] = jnp.where(allowed, toeplitz[:, 0:w], NEG_INF)


def _band_bias(b_rel, qs):
    depth = b_rel.shape[0]
    w = qs + B_WIN
    wp = -(-(w + qs) // LANES) * LANES
    jj = np.arange(wp)
    d = np.where(jj < w, jj, jj - wp)
    base = b_rel[:, :, np.clip(B_WIN - d, -REL_CLIP, REL_CLIP) + REL_CLIP]
    return pl.pallas_call(
        functools.partial(_band_bias_kernel, qs=qs, w=w, wp=wp),
        grid=(depth,),
        in_specs=[pl.BlockSpec((1, B_HEADS, wp), lambda l: (l, 0, 0))],
        out_specs=pl.BlockSpec((1, B_HEADS, qs, w), lambda l: (l, 0, 0, 0)),
        out_shape=jax.ShapeDtypeStruct((depth, B_HEADS, qs, w), F32),
        compiler_params=_cparams("arbitrary"),
        name=f"band_bias_{qs}",
    )(base)


def _band_attend(q_ref, bias_ref, o_ref, kp_scr, vp_scr, start, valid, qs):
    w = qs + B_WIN
    for h in range(B_HEADS):
        sl = slice(h * B_HEAD_DIM, (h + 1) * B_HEAD_DIM)
        qh = q_ref[:, sl].astype(BF16)
        lg = _nt_dot(qh, kp_scr[pl.ds(start, w), sl]) * (B_HEAD_DIM ** -0.5) + bias_ref[0, h]
        if valid is not None:
            lg = jnp.where(valid, lg, NEG_INF)
        m = jnp.max(lg, axis=-1, keepdims=True)
        e = jnp.exp(lg - m)
        den = jnp.sum(e, axis=-1, keepdims=True)
        o = jnp.dot(e.astype(BF16), vp_scr[pl.ds(start, w), sl], preferred_element_type=F32)
        o_ref[:, sl] = o / den


def _band_prompt_kernel(q_ref, k_ref, v_ref, bias_ref, bufk_ref, bufv_ref, o_ref, ck_ref, cv_ref,
                        kp_scr, vp_scr, *, s_keys, qs, nkeep):
    del bufk_ref, bufv_ref
    j = pl.program_id(1)

    @pl.when(j == 0)
    def _():
        kp_scr[0:B_WIN, :] = jnp.zeros((B_WIN, B_WIDTH), BF16)
        vp_scr[0:B_WIN, :] = jnp.zeros((B_WIN, B_WIDTH), BF16)
        kp_scr[B_WIN:B_WIN + s_keys, :] = k_ref[...].astype(BF16)
        vp_scr[B_WIN:B_WIN + s_keys, :] = v_ref[...].astype(BF16)
        for h in range(B_HEADS):
            sl = slice(h * B_HEAD_DIM, (h + 1) * B_HEAD_DIM)
            _head_rows(ck_ref, h, nkeep, B_HEADS)[...] = k_ref[s_keys - nkeep:, sl]
            _head_rows(cv_ref, h, nkeep, B_HEADS)[...] = v_ref[s_keys - nkeep:, sl]

    start = pl.multiple_of(j * qs, qs)
    valid = lax.broadcasted_iota(I32, (qs, qs + B_WIN), 1) >= B_WIN - j * qs
    _band_attend(q_ref, bias_ref, o_ref, kp_scr, vp_scr, start, valid, qs)


def _band_sample_kernel(q_ref, kc_ref, vc_ref, kn_ref, vn_ref, bias_ref, bufo_ref, bufk_ref, bufv_ref,
                        o_ref, ck_ref, cv_ref, kp_scr, vp_scr, *, qs, spb):
    del bufo_ref, bufk_ref, bufv_ref
    for s in range(spb):
        new = slice(s * qs, (s + 1) * qs)
        for h in range(B_HEADS):
            sl = slice(h * B_HEAD_DIM, (h + 1) * B_HEAD_DIM)
            old_rows = pl.ds(s * B_WIN * B_HEADS + h, B_WIN, stride=B_HEADS)
            new_rows = pl.ds(s * qs * B_HEADS + h, qs, stride=B_HEADS)
            kp_scr[s, 0:B_WIN, sl] = kc_ref[old_rows, :].astype(BF16)
            vp_scr[s, 0:B_WIN, sl] = vc_ref[old_rows, :].astype(BF16)
            ck_ref[new_rows, :] = kn_ref[new, sl]
            cv_ref[new_rows, :] = vn_ref[new, sl]
        kp_scr[s, B_WIN:B_WIN + qs, :] = kn_ref[new, :].astype(BF16)
        vp_scr[s, B_WIN:B_WIN + qs, :] = vn_ref[new, :].astype(BF16)
        _band_attend(q_ref.at[new], bias_ref, o_ref.at[new], kp_scr.at[s], vp_scr.at[s], 0, None, qs)


def _band_prompt(proj, bias, out_k, out_v, layer, *, nb, tq, qs):
    nsteps = tq // qs
    w = qs + B_WIN
    nkeep = min(B_WIN, tq)
    kcol = lambda off: pl.BlockSpec((tq, B_WIDTH), lambda b, j: (b, off // B_WIDTH))
    cache = pl.BlockSpec((nkeep * B_HEADS, B_HEAD_DIM), lambda b, j: (layer * nb + b, 0))
    return pl.pallas_call(
        functools.partial(_band_prompt_kernel, s_keys=tq, qs=qs, nkeep=nkeep),
        grid=(nb, nsteps),
        in_specs=[
            pl.BlockSpec((qs, B_WIDTH), lambda b, j: (b * nsteps + j, COL_BQ // B_WIDTH)),
            kcol(COL_BK), kcol(COL_BV),
            pl.BlockSpec((1, B_HEADS, qs, w), lambda b, j: (layer, 0, 0, 0)),
            ANY, ANY,
        ],
        out_specs=[pl.BlockSpec((qs, B_WIDTH), lambda b, j: (b * nsteps + j, 0)), cache, cache],
        out_shape=[jax.ShapeDtypeStruct((proj.shape[0], B_WIDTH), F32),
                   jax.ShapeDtypeStruct(out_k.shape, F32), jax.ShapeDtypeStruct(out_v.shape, F32)],
        input_output_aliases={4: 1, 5: 2},
        scratch_shapes=[pltpu.VMEM((B_WIN + tq, B_WIDTH), BF16),
                        pltpu.VMEM((B_WIN + tq, B_WIDTH), BF16)],
        compiler_params=_cparams("parallel", "arbitrary"),
        name="band_prompt",
    )(proj, proj, proj, bias, out_k, out_v)


def _band_sample(proj, cache_k, cache_v, bias, buf, out_k, out_v, layer, *, nb, ts, row0):
    spb = BAND_SEQS_PER_STEP if nb % BAND_SEQS_PER_STEP == 0 else 1
    ngrp = nb // spb
    assert row0 % (spb * ts) == 0
    rb0 = row0 // (spb * ts)
    w = ts + B_WIN
    qcol = lambda off: pl.BlockSpec((spb * ts, B_WIDTH), lambda b, j: (rb0 + b, off // B_WIDTH))
    per_seq = lambda rows: pl.BlockSpec((spb * rows * B_HEADS, B_HEAD_DIM),
                                        lambda b, j: (layer * ngrp + b, 0))
    return pl.pallas_call(
        functools.partial(_band_sample_kernel, qs=ts, spb=spb),
        grid=(ngrp, 1),
        in_specs=[qcol(COL_BQ), per_seq(B_WIN), per_seq(B_WIN), qcol(COL_BK), qcol(COL_BV),
                  pl.BlockSpec((1, B_HEADS, ts, w), lambda b, j: (layer, 0, 0, 0)),
                  ANY, ANY, ANY],
        out_specs=[pl.BlockSpec((spb * ts, B_WIDTH), lambda b, j: (rb0 + b, 0)), per_seq(ts), per_seq(ts)],
        out_shape=[jax.ShapeDtypeStruct(buf.shape, F32), jax.ShapeDtypeStruct(out_k.shape, F32),
                   jax.ShapeDtypeStruct(out_v.shape, F32)],
        input_output_aliases={6: 0, 7: 1, 8: 2},
        scratch_shapes=[pltpu.VMEM((spb, w, B_WIDTH), BF16), pltpu.VMEM((spb, w, B_WIDTH), BF16)],
        compiler_params=_cparams("parallel", "arbitrary"),
        name="band_sample",
    )(proj, cache_k, cache_v, proj, proj, bias, buf, out_k, out_v)


XP_OFF = 8
SSD_CHUNKS_PER_STEP = 4
SSD_SEQS_PER_STEP = 4
_HEAD_SPREAD = np.zeros((LANES, C_WIDTH), np.float32)
for _h in range(C_HEADS):
    _HEAD_SPREAD[MISC_DT + _h, _h * C_HEAD_DIM:(_h + 1) * C_HEAD_DIM] = 1.0


def _ssd_kernel(*refs, zero_init, nsteps, cps, spb):
    if zero_init:
        (z_ref, xbc_ref, misc_ref, cw_ref, cbias_ref, prm_ref, gn_ref, ex_ref, dskip_ref,
         bufh_ref, bufc_ref, o_ref, hout_ref, cnew_ref, st_scr, xp_scr, y_scr) = refs
    else:
        (z_ref, xbc_ref, misc_ref, h0_ref, cb0_ref, cw_ref, cbias_ref, prm_ref, gn_ref, ex_ref, dskip_ref,
         bufo_ref, bufh_ref, bufc_ref, o_ref, hout_ref, cnew_ref, st_scr, xp_scr, y_scr) = refs
        del bufo_ref
    del bufh_ref, bufc_ref
    c = pl.program_id(1)
    tail = CONV_W - 1
    rep = C_HEADS // C_GROUPS
    gw = C_WIDTH // C_GROUPS

    def start(s):
        if zero_init:
            st_scr[s * C_GROUPS:(s + 1) * C_GROUPS] = jnp.zeros((C_GROUPS, C_STATE, gw), F32)
            xp_scr[s, XP_OFF - tail:XP_OFF, :] = jnp.zeros((tail, CONV_DIM), F32)
        else:
            for h in range(C_HEADS):
                st_scr[s * C_GROUPS + h // rep, :, (h % rep) * C_HEAD_DIM:(h % rep + 1) * C_HEAD_DIM] = (
                    h0_ref[s, h].T)
            xp_scr[s, XP_OFF - tail:XP_OFF, :] = cb0_ref[s]

    def chunk(rows, s):
        xp_s, y_s, st_s = xp_scr.at[s], y_scr.at[s], st_scr.at[s * C_GROUPS:(s + 1) * C_GROUPS]
        xp_s[XP_OFF:XP_OFF + CHUNK, :] = xbc_ref[rows, :]
        y = cbias_ref[...] + xp_s[XP_OFF - tail:XP_OFF - tail + CHUNK, :] * cw_ref[0:1, :]
        for t in range(1, CONV_W):
            y = y + xp_s[XP_OFF - tail + t:XP_OFF - tail + t + CHUNK, :] * cw_ref[t:t + 1, :]
        new_tail = xp_s[XP_OFF + CHUNK - tail:XP_OFF + CHUNK, :]
        xp_s[XP_OFF - tail:XP_OFF, :] = new_tail
        cnew_ref[s] = new_tail
        xc = y * (1.0 / (1.0 + jnp.exp(-y)))
        xs = xc[:, 0:C_WIDTH]

        pre = misc_ref[rows, :] + prm_ref[0:1, :]
        dt = jnp.maximum(pre, 0.0) + jnp.log(1.0 + jnp.exp(-jnp.abs(pre)))
        ad = dt * (-jnp.exp(prm_ref[1:2, :]))
        li = lax.broadcasted_iota(I32, (CHUNK, CHUNK), 0)
        si = lax.broadcasted_iota(I32, (CHUNK, CHUNK), 1)
        causal = li >= si
        tri = causal.astype(BF16)
        acs = sum(jnp.dot(tri, part, preferred_element_type=F32) for part in _split3(ad))
        acs_t = acs.T
        ex = ex_ref[...]
        spread = lambda v: sum(jnp.dot(part, ex, preferred_element_type=F32) for part in _split3(v))
        dt_x = spread(dt)
        acs_x = spread(acs)
        acs_last_x = acs_x[CHUNK - 1:CHUNK, :]
        xd_all = xs * dt_x
        xd_b = xd_all.astype(BF16)
        xde_b = (xd_all * jnp.exp(acs_last_x - acs_x)).astype(BF16)
        grow_x = jnp.exp(acs_x)
        chunk_decay_x = jnp.exp(acs_last_x)
        head_of_lane = lax.broadcasted_iota(I32, (CHUNK, gw), 1) >> 6

        for g in range(C_GROUPS):
            gl = slice(g * gw, (g + 1) * gw)
            b_g = xc[:, C_WIDTH + g * C_STATE:C_WIDTH + (g + 1) * C_STATE]
            c_g = xc[:, C_WIDTH + (C_GROUPS + g) * C_STATE:C_WIDTH + (C_GROUPS + g + 1) * C_STATE]
            b_gt = b_g.T.astype(BF16)
            c_gb = c_g.astype(BF16)
            cb = jnp.dot(c_gb, b_gt, preferred_element_type=F32)
            st = st_s[g]
            y_g = jnp.dot(c_gb, st.astype(BF16), preferred_element_type=F32) * grow_x[:, gl]
            xd_g = xd_b[:, gl]
            for r in range(rep):
                col = MISC_DT + g * rep + r
                seg = acs[:, col:col + 1] - acs_t[col:col + 1, :]
                decay_in = jnp.where(causal, jnp.exp(jnp.where(causal, seg, 0.0)), 0.0)
                t_r = jnp.dot((cb * decay_in).astype(BF16), xd_g, preferred_element_type=F32)
                y_g = y_g + jnp.where(head_of_lane == r, t_r, 0.0)
            y_s[:, gl] = y_g + dskip_ref[:, gl] * xs[:, gl]
            st_s[g] = st * chunk_decay_x[:, gl] + jnp.dot(b_gt, xde_b[:, gl], preferred_element_type=F32)

        z = z_ref[rows, :]
        gate = y_s[...] * (z * (1.0 / (1.0 + jnp.exp(-z))))
        for g in range(C_GROUPS):
            gg = gate[:, g * gw:(g + 1) * gw]
            ms = jnp.mean(gg * gg, axis=-1, keepdims=True)
            o_ref[rows, g * gw:(g + 1) * gw] = gg * lax.rsqrt(ms + EPS) * gn_ref[:, g * gw:(g + 1) * gw]

    def finish(s):
        for h in range(C_HEADS):
            hout_ref[s, h] = st_scr[s * C_GROUPS + h // rep, :,
                                    (h % rep) * C_HEAD_DIM:(h % rep + 1) * C_HEAD_DIM].T

    if spb == 1:
        pl.when(c == 0)(functools.partial(start, 0))
        for i in range(cps):
            chunk(slice(i * CHUNK, (i + 1) * CHUNK), 0)
        pl.when(c == nsteps - 1)(functools.partial(finish, 0))
    else:
        assert nsteps == 1 and cps == 1
        for s in range(spb):
            start(s)
            chunk(slice(s * CHUNK, (s + 1) * CHUNK), s)
            finish(s)


def _ssd(proj, conv_w, conv_b, prm, gate_norm, out_h, out_c, layer, *, nb, nchunks, row0, state=None):
    cps = SSD_CHUNKS_PER_STEP if nchunks % SSD_CHUNKS_PER_STEP == 0 else 1
    nsteps = nchunks // cps
    spb = SSD_SEQS_PER_STEP if nchunks == 1 and nb % SSD_SEQS_PER_STEP == 0 else 1
    nblk = nb // spb
    rt = cps * spb * CHUNK
    assert row0 % rt == 0
    rb0 = row0 // rt
    row = lambda b, c: rb0 + b * nsteps + c
    const = lambda shape: pl.BlockSpec(shape, lambda b, c: tuple(0 for _ in shape))
    per_seq_h = pl.BlockSpec((spb, C_HEADS, C_HEAD_DIM, C_STATE), lambda b, c: (layer * nblk + b, 0, 0, 0))
    per_seq_c = pl.BlockSpec((spb, CONV_W - 1, CONV_DIM), lambda b, c: (layer * nblk + b, 0, 0))
    in_specs = [
        pl.BlockSpec((rt, C_WIDTH), lambda b, c: (row(b, c), COL_CZ // C_WIDTH)),
        pl.BlockSpec((rt, CONV_DIM), lambda b, c: (row(b, c), COL_CXBC // CONV_DIM)),
        pl.BlockSpec((rt, LANES), lambda b, c: (row(b, c), COL_MISC // LANES)),
    ]
    args = [proj, proj, proj]
    if state is not None:
        h0, cbuf0, buf = state
        in_specs += [per_seq_h, per_seq_c]
        args += [h0, cbuf0]
    in_specs += [const((CONV_W, CONV_DIM)), const((1, CONV_DIM)), const((8, LANES)), const((1, C_WIDTH)),
                 const((LANES, C_WIDTH)), const((1, C_WIDTH))]
    args += [conv_w, conv_b.reshape(1, CONV_DIM), prm[0], gate_norm.reshape(1, C_WIDTH),
             jnp.asarray(_HEAD_SPREAD, BF16), prm[1]]
    aliases = {}
    if state is not None:
        in_specs.append(ANY)
        args.append(buf)
        aliases[len(args) - 1] = 0
    in_specs += [ANY, ANY]
    args += [out_h, out_c]
    aliases[len(args) - 2] = 1
    aliases[len(args) - 1] = 2
    return pl.pallas_call(
        functools.partial(_ssd_kernel, zero_init=state is None, nsteps=nsteps, cps=cps, spb=spb),
        grid=(nblk, nsteps),
        in_specs=in_specs,
        out_specs=[pl.BlockSpec((rt, C_WIDTH), lambda b, c: (row(b, c), 0)), per_seq_h, per_seq_c],
        out_shape=[
            jax.ShapeDtypeStruct((proj.shape[0], C_WIDTH), F32),
            jax.ShapeDtypeStruct(out_h.shape, F32),
            jax.ShapeDtypeStruct(out_c.shape, F32),
        ],
        input_output_aliases=aliases,
        scratch_shapes=[pltpu.VMEM((spb * C_GROUPS, C_STATE, C_WIDTH // C_GROUPS), F32),
                        pltpu.VMEM((spb, XP_OFF + CHUNK, CONV_DIM), F32),
                        pltpu.VMEM((spb, CHUNK, C_WIDTH), F32)],
        compiler_params=_cparams("parallel", "arbitrary"),
        name="ssd_prompt" if state is None else "ssd_sample",
    )(*args)


def _outproj_kernel(x_ref, oa_ref, ob_ref, oc_ref, w_ref, o_ref):
    acc = x_ref[...]
    acc = acc + jnp.dot(oa_ref[...].astype(BF16), w_ref[0, 0:A_WIDTH, :], preferred_element_type=F32)
    acc = acc + jnp.dot(ob_ref[...].astype(BF16), w_ref[0, A_WIDTH:A_WIDTH + B_WIDTH, :],
                        preferred_element_type=F32)
    acc = acc + jnp.dot(oc_ref[...].astype(BF16), w_ref[0, A_WIDTH + B_WIDTH:, :],
                        preferred_element_type=F32)
    o_ref[...] = acc


def _out_proj(x, oa, ob, oc, w_all, layer):
    t, d = x.shape
    tm = _row_tile(t, 512)
    rows = lambda w: pl.BlockSpec((tm, w), lambda i: (i, 0))
    return pl.pallas_call(
        _outproj_kernel,
        grid=(t // tm,),
        in_specs=[rows(d), rows(A_WIDTH), rows(B_WIDTH), rows(C_WIDTH),
                  pl.BlockSpec((1, d, d), lambda i: (layer, 0, 0))],
        out_specs=rows(d),
        out_shape=jax.ShapeDtypeStruct((t, d), F32),
        compiler_params=_cparams("parallel"),
        name="out_proj",
    )(x, oa, ob, oc, w_all)


def _ffn_kernel(x_ref, nw_ref, wu_ref, wd_ref, o_ref, h_scr):
    def step(first):
        if first:
            x = x_ref[...]
            ms = jnp.mean(x * x, axis=-1, keepdims=True)
            h_scr[...] = (x * lax.rsqrt(ms + EPS) * nw_ref[...]).astype(BF16)
        u = jnp.maximum(jnp.dot(h_scr[...], wu_ref[0], preferred_element_type=F32), 0.0)
        down = jnp.dot((u * u).astype(BF16), wd_ref[0], preferred_element_type=F32)
        if first:
            o_ref[...] = x_ref[...] + down
        else:
            o_ref[...] += down

    f = pl.program_id(1)
    pl.when(f == 0)(functools.partial(step, True))
    pl.when(f != 0)(functools.partial(step, False))


def _ffn(x, norm_w, wu_all, wd_all, layer):
    t, d = x.shape
    ff = wu_all.shape[2]
    tm = _row_tile(t, 1024)
    tf = 512
    return pl.pallas_call(
        _ffn_kernel,
        grid=(t // tm, ff // tf),
        in_specs=[
            pl.BlockSpec((tm, d), lambda i, f: (i, 0)),
            pl.BlockSpec((1, d), lambda i, f: (0, 0)),
            pl.BlockSpec((1, d, tf), lambda i, f: (layer, 0, f)),
            pl.BlockSpec((1, tf, d), lambda i, f: (layer, f, 0)),
        ],
        out_specs=pl.BlockSpec((tm, d), lambda i, f: (i, 0)),
        out_shape=jax.ShapeDtypeStruct((t, d), F32),
        scratch_shapes=[pltpu.VMEM((tm, d), BF16)],
        compiler_params=_cparams("parallel", "arbitrary"),
        name="ffn",
    )(x, norm_w.reshape(1, d), wu_all, wd_all)


def _norm_kernel(x_ref, nw_ref, o_ref):
    x = x_ref[...]
    ms = jnp.mean(x * x, axis=-1, keepdims=True)
    o_ref[...] = x * lax.rsqrt(ms + EPS) * nw_ref[...]


def _final_norm(x, norm_w, row0, nrows):
    d = x.shape[1]
    tm = _row_tile(nrows, 512)
    assert row0 % tm == 0
    return pl.pallas_call(
        _norm_kernel,
        grid=(nrows // tm,),
        in_specs=[pl.BlockSpec((tm, d), lambda i: (row0 // tm + i, 0)),
                  pl.BlockSpec((1, d), lambda i: (0, 0))],
        out_specs=pl.BlockSpec((tm, d), lambda i: (i, 0)),
        out_shape=jax.ShapeDtypeStruct((nrows, d), F32),
        compiler_params=_cparams("parallel"),
        name="final_norm",
    )(x, norm_w.reshape(1, d))


def _lane_row(v, off):
    return jnp.zeros((LANES,), F32).at[off:off + v.shape[0]].set(v)


def _trunk(x_prompt, x_sample, cache_a_k, cache_a_v, cache_a_kidx, cache_b_k, cache_b_v,
           state_ssm, state_conv, norm1, w_in, w_out, b_rel, conv_w, conv_b, dt_bias,
           a_log, d_skip, gate_norm, norm2, w_up, w_down, final_norm):
    bp, tp, d = x_prompt.shape
    bs, ts, _ = x_sample.shape
    depth = w_in.shape[0]
    past = cache_a_k.shape[2]
    nbuf = cache_b_k.shape[2]
    n_p = bp * tp
    n_s = bs * ts
    assert ts == CHUNK and tp % DSA_KB_PROMPT == 0 and tp % BAND_QS_PROMPT == 0 and nbuf == B_WIN
    qs_p = LANES
    nkeep = min(B_WIN, tp)

    x = jnp.concatenate([x_prompt.reshape(n_p, d), x_sample.reshape(n_s, d)], axis=0)

    tm_in = _row_tile(math.gcd(tp, n_p + n_s), 1024)
    assert tm_in % ts == 0
    pos = jnp.concatenate([jnp.arange(tp), jnp.tile(past + jnp.arange(ts), tm_in // ts)])
    tab_a = _rope_tables(pos, A_HEAD_DIM)
    tab_i = _rope_tables(pos, IDX_DIM)
    w_out_b, w_up_b, w_down_b = w_out.astype(BF16), w_up.astype(BF16), w_down.astype(BF16)
    bias_p = _band_bias(b_rel, BAND_QS_PROMPT)
    bias_s = _band_bias(b_rel, ts)
    w_in_r = _regroup_w_in(w_in)
    cak = cache_a_k.reshape(depth * bs * past * A_HEADS, A_HEAD_DIM)
    cav = cache_a_v.reshape(depth * bs * past * A_HEADS, A_HEAD_DIM)
    caki = cache_a_kidx.reshape(depth * bs * past, IDX_DIM)
    cbk = cache_b_k.reshape(depth * bs * nbuf * B_HEADS, B_HEAD_DIM)
    cbv = cache_b_v.reshape(depth * bs * nbuf * B_HEADS, B_HEAD_DIM)
    h0 = state_ssm.reshape(depth * bs, C_HEADS, C_HEAD_DIM, C_STATE)
    cbuf0 = state_conv.reshape(depth * bs, CONV_W - 1, CONV_DIM)

    p_ak = _uninit((depth * bp * tp * A_HEADS, A_HEAD_DIM))
    p_av = _uninit((depth * bp * tp * A_HEADS, A_HEAD_DIM))
    p_aki = _uninit((depth * bp * tp, IDX_DIM))
    p_bk = _uninit((depth * bp * nkeep * B_HEADS, B_HEAD_DIM))
    p_bv = _uninit((depth * bp * nkeep * B_HEADS, B_HEAD_DIM))
    p_h = _uninit((depth * bp, C_HEADS, C_HEAD_DIM, C_STATE))
    p_c = _uninit((depth * bp, CONV_W - 1, CONV_DIM))
    s_ak = _uninit((depth * bs * ts * A_HEADS, A_HEAD_DIM))
    s_av = _uninit((depth * bs * ts * A_HEADS, A_HEAD_DIM))
    s_aki = _uninit((depth * bs * ts, IDX_DIM))
    s_bk = _uninit((depth * bs * ts * B_HEADS, B_HEAD_DIM))
    s_bv = _uninit((depth * bs * ts * B_HEADS, B_HEAD_DIM))
    s_h = _uninit((depth * bs, C_HEADS, C_HEAD_DIM, C_STATE))
    s_c = _uninit((depth * bs, CONV_W - 1, CONV_DIM))

    for l in range(depth):
        proj = _in_proj(x, norm1[l], w_in_r, l, tab_a, tab_i, tm=tm_in, n_prompt=n_p, t_prompt=tp)

        oa, p_ak, p_av, p_aki = _dsa_prompt(proj, p_ak, p_av, p_aki, l, nb=bp, tq=tp, qs=qs_p,
                                            kb=DSA_KB_PROMPT)
        oa, s_ak, s_av, s_aki = _dsa_sample(proj, cak, cav, caki, oa, s_ak, s_av, s_aki, l,
                                            nb=bs, ts=ts, past=past, row0=n_p, kb=DSA_KB_SAMPLE)
        ob, p_bk, p_bv = _band_prompt(proj, bias_p, p_bk, p_bv, l, nb=bp, tq=tp, qs=BAND_QS_PROMPT)
        ob, s_bk, s_bv = _band_sample(proj, cbk, cbv, bias_s, ob, s_bk, s_bv, l, nb=bs, ts=ts, row0=n_p)
        rows = jnp.zeros((8, LANES), F32)
        rows = rows.at[0].set(_lane_row(dt_bias[l], MISC_DT)).at[1].set(_lane_row(a_log[l], MISC_DT))
        prm = (rows, jnp.repeat(d_skip[l], C_HEAD_DIM).reshape(1, C_WIDTH))
        oc, p_h, p_c = _ssd(proj, conv_w[l], conv_b[l], prm, gate_norm[l], p_h, p_c, l,
                            nb=bp, nchunks=tp // CHUNK, row0=0)
        oc, s_h, s_c = _ssd(proj, conv_w[l], conv_b[l], prm, gate_norm[l], s_h, s_c, l,
                            nb=bs, nchunks=1, row0=n_p, state=(h0, cbuf0, oc))

        x = _out_proj(x, oa, ob, oc, w_out_b, l)
        x = _ffn(x, norm2[l], w_up_b, w_down_b, l)

    y_prompt = _final_norm(x, final_norm, 0, n_p).reshape(bp, tp, d)
    y_sample = _final_norm(x, final_norm, n_p, n_s).reshape(bs, ts, d)
    return (y_prompt, y_sample,
            p_ak.reshape(depth, bp, tp, A_HEADS, A_HEAD_DIM),
            p_av.reshape(depth, bp, tp, A_HEADS, A_HEAD_DIM),
            p_aki.reshape(depth, bp, tp, IDX_DIM),
            p_bk.reshape(depth, bp, nkeep, B_HEADS, B_HEAD_DIM),
            p_bv.reshape(depth, bp, nkeep, B_HEADS, B_HEAD_DIM),
            p_h.reshape(depth, bp, C_HEADS, C_HEAD_DIM, C_STATE),
            p_c.reshape(depth, bp, CONV_W - 1, CONV_DIM),
            s_ak.reshape(depth, bs, ts, A_HEADS, A_HEAD_DIM),
            s_av.reshape(depth, bs, ts, A_HEADS, A_HEAD_DIM),
            s_aki.reshape(depth, bs, ts, IDX_DIM),
            s_bk.reshape(depth, bs, ts, B_HEADS, B_HEAD_DIM),
            s_bv.reshape(depth, bs, ts, B_HEADS, B_HEAD_DIM),
            s_h.reshape(depth, bs, C_HEADS, C_HEAD_DIM, C_STATE),
            s_c.reshape(depth, bs, CONV_W - 1, CONV_DIM))


def kernel(x_prompt, x_sample, cache_a_k, cache_a_v, cache_a_kidx, cache_b_k, cache_b_v, state_ssm, state_conv, norm1, w_in, w_out, b_rel, conv_w, conv_b, dt_bias, a_log, d_skip, gate_norm, norm2, w_up, w_down, final_norm):
    return _trunk(x_prompt, x_sample, cache_a_k, cache_a_v, cache_a_kidx, cache_b_k, cache_b_v,
                  state_ssm, state_conv, norm1, w_in, w_out, b_rel, conv_w, conv_b, dt_bias,
                  a_log, d_skip, gate_norm, norm2, w_up, w_down, final_norm)
```
